```python
import math
import jax, jax.numpy as jnp
from jax import lax
import numpy as np

D_MODEL = 1024
BATCH = 2
SEQ = 8192
DEPTH = 1

CHUNK = 64
Q_BLOCK = 128
EPS = 1e-6

DA_HEADS = 4
DA_HEAD_DIM = 64
DA_V_DIM = 2 * DA_HEAD_DIM
DA_QK = DA_HEADS * 2 * DA_HEAD_DIM
DA_WIDTH = DA_HEADS * DA_V_DIM

GDN_HEADS = 4
GDN_K_DIM = 128
GDN_V_DIM = 128
GDN_QK = GDN_HEADS * GDN_K_DIM
GDN_WIDTH = GDN_HEADS * GDN_V_DIM
CONV_WIDTH = 4

N_BRANCHES = 2
D_IN = 2 * DA_QK + 2 * DA_WIDTH + 2 * GDN_QK + 2 * GDN_WIDTH + 2 * GDN_HEADS + N_BRANCHES * D_MODEL

kernel_name = "hybrid_diffattn_gdn_gated_merge"


def rms_norm(x, w):
    xf = x.astype(jnp.float32)
    y = xf * lax.rsqrt(jnp.mean(xf * xf, axis=-1, keepdims=True) + EPS)
    return (y * w.astype(jnp.float32)).astype(x.dtype)


def l2_norm(x):
    return x * lax.rsqrt(jnp.sum(x * x, axis=-1, keepdims=True) + EPS)


def lambda_init_for(layer):
    return 0.8 - 0.6 * math.exp(-0.3 * layer)


def diff_attention(q, k, v, q_gain, k_gain, lam_qk, out_gain, lambda_init):
    B, S, _ = q.shape
    nb = S // Q_BLOCK
    q = rms_norm(q.reshape(B, S, DA_HEADS, 2, DA_HEAD_DIM), q_gain)
    k = rms_norm(k.reshape(B, S, DA_HEADS, 2, DA_HEAD_DIM), k_gain)
    lam_f = lam_qk.astype(jnp.float32)
    lam = (jnp.exp(jnp.sum(lam_f[0] * lam_f[1])) - jnp.exp(jnp.sum(lam_f[2] * lam_f[3]))
           + lambda_init)
    qb = q.reshape(B, nb, Q_BLOCK, DA_HEADS, 2, DA_HEAD_DIM).transpose(1, 0, 3, 4, 2, 5)
    kf = k.transpose(0, 2, 3, 1, 4).astype(jnp.float32)
    vf = v.reshape(B, S, DA_HEADS, DA_V_DIM).transpose(0, 2, 1, 3).astype(jnp.float32)
    key_chunk = jnp.arange(S) // CHUNK
    scale = DA_HEAD_DIM ** -0.5

    def block(args):
        q_blk, blk = args
        q_chunk = (blk * Q_BLOCK + jnp.arange(Q_BLOCK)) // CHUNK
        mask = key_chunk[None, :] <= q_chunk[:, None]
        s = jnp.einsum('bhmqd,bhmkd->bhmqk', q_blk.astype(jnp.float32), kf) * scale
        p = jax.nn.softmax(jnp.where(mask, s, -jnp.inf), axis=-1)
        p = p[:, :, 0] - lam * p[:, :, 1]
        return jnp.einsum('bhqk,bhkd->bhqd', p, vf)

    o = lax.map(block, (qb, jnp.arange(nb)))
    o = o.transpose(1, 0, 3, 2, 4).reshape(B, S, DA_HEADS, DA_V_DIM)
    o = rms_norm(o, out_gain) * (1.0 - lambda_init)
    return o.reshape(B, S, DA_WIDTH).astype(v.dtype)


def short_conv(x, w):
    C = x.shape[-1]
    y = lax.conv_general_dilated(x, w[:, None, :].astype(x.dtype), window_strides=(1,),
                                 padding=[(CONV_WIDTH - 1, 0)],
                                 dimension_numbers=('NWC', 'WIO', 'NWC'),
                                 feature_group_count=C)
    return jax.nn.silu(y)


def gated_delta_rule(q, k, v, g, beta):
    B, H, S, dk = q.shape
    dv = v.shape[-1]
    n = S // CHUNK
    q = q * dk ** -0.5

    def rs(t):
        return t.reshape(B, H, n, CHUNK, *t.shape[3:])

    q, k, v, g, beta = rs(q), rs(k), rs(v), rs(g), rs(beta)
    g = jnp.cumsum(g, axis=-1)
    tri_incl = jnp.tril(jnp.ones((CHUNK, CHUNK), dtype=bool))
    tri_strict = jnp.tril(jnp.ones((CHUNK, CHUNK), dtype=bool), -1)
    gamma = jnp.exp(jnp.where(tri_incl, g[..., :, None] - g[..., None, :], -jnp.inf))
    k_beta = k * beta[..., None]
    v_beta = v * beta[..., None]
    a_strict = jnp.where(tri_strict, jnp.einsum('bhnid,bhnjd->bhnij', k_beta, k) * gamma, 0.0)
    eye = jnp.eye(CHUNK, dtype=q.dtype)
    rhs = jnp.concatenate([v_beta, k_beta * jnp.exp(g)[..., None]], axis=-1)
    sol = lax.linalg.triangular_solve(a_strict + eye, rhs, left_side=True, lower=True)
    u = sol[..., :dv]
    w = sol[..., dv:]
    a_intra = jnp.where(tri_incl, jnp.einsum('bhnid,bhnjd->bhnij', q, k) * gamma, 0.0)
    q_dec = q * jnp.exp(g)[..., None]
    k_tail = k * jnp.exp(g[..., -1:] - g)[..., None]
    g_last = jnp.exp(g[..., -1])

    def nf(t):
        return jnp.moveaxis(t, 2, 0)

    def step(state, inp):
        u_c, w_c, qd_c, kt_c, a_c, gl_c = inp
        v_new = u_c - jnp.einsum('bhcd,bhde->bhce', w_c, state)
        o_c = (jnp.einsum('bhcd,bhde->bhce', qd_c, state)
               + jnp.einsum('bhij,bhje->bhie', a_c, v_new))
        state = state * gl_c[..., None, None] + jnp.einsum('bhcd,bhce->bhde', kt_c, v_new)
        return state, o_c

    s0 = jnp.zeros((B, H, dk, dv), dtype=q.dtype)
    _, o = lax.scan(step, s0, (nf(u), nf(w), nf(q_dec), nf(k_tail), nf(a_intra), nf(g_last)))
    return o.transpose(1, 2, 0, 3, 4).reshape(B, H, S, dv)


def gated_deltanet(q, k, v, a, b, conv_w, a_log, dt_bias):
    B, S, _ = q.shape
    qkv = short_conv(jnp.concatenate([q, k, v], axis=-1), conv_w).astype(jnp.float32)
    q, k, v = jnp.split(qkv, [GDN_QK, 2 * GDN_QK], axis=-1)
    q = l2_norm(q.reshape(B, S, GDN_HEADS, GDN_K_DIM)).transpose(0, 2, 1, 3)
    k = l2_norm(k.reshape(B, S, GDN_HEADS, GDN_K_DIM)).transpose(0, 2, 1, 3)
    v = v.reshape(B, S, GDN_HEADS, GDN_V_DIM).transpose(0, 2, 1, 3)
    af = a.astype(jnp.float32)
    g = -jnp.exp(a_log.astype(jnp.float32)) * jax.nn.softplus(af + dt_bias.astype(jnp.float32))
    beta = jax.nn.sigmoid(b.astype(jnp.float32))
    o = gated_delta_rule(q, k, v, g.transpose(0, 2, 1), beta.transpose(0, 2, 1))
    return o.transpose(0, 2, 1, 3)


def setup_inputs(seed: int = 0) -> dict:
    key = jax.random.key(seed)
    ks = jax.random.split(key, 16)
    f32 = jnp.float32
    x = jax.random.normal(ks[0], (BATCH, SEQ, D_MODEL), f32)
    norm_w = 1.0 + 0.02 * jax.random.normal(ks[1], (DEPTH, D_MODEL), f32)
    w_in = jax.random.normal(ks[2], (DEPTH, D_MODEL, D_IN), f32) * D_MODEL ** -0.5
    da_q_gain = 1.0 + 0.02 * jax.random.normal(ks[3], (DEPTH, DA_HEAD_DIM), f32)
    da_k_gain = 1.0 + 0.02 * jax.random.normal(ks[4], (DEPTH, DA_HEAD_DIM), f32)
    da_lambda = 0.1 * jax.random.normal(ks[5], (DEPTH, 4, DA_HEAD_DIM), f32)
    da_out_gain = 1.0 + 0.02 * jax.random.normal(ks[6], (DEPTH, DA_V_DIM), f32)
    gdn_conv_w = jax.random.normal(ks[7], (DEPTH, CONV_WIDTH, 2 * GDN_QK + GDN_WIDTH), f32) * CONV_WIDTH ** -0.5
    gdn_a_log = jnp.log(jax.random.uniform(ks[8], (DEPTH, GDN_HEADS), f32, 1.0, 16.0))
    dt = jnp.exp(jax.random.uniform(ks[9], (DEPTH, GDN_HEADS), f32, math.log(1e-3), math.log(1e-1)))
    gdn_dt_bias = dt + jnp.log(-jnp.expm1(-dt))
    gdn_norm_w = 1.0 + 0.02 * jax.random.normal(ks[10], (DEPTH, GDN_V_DIM), f32)
    w_up_da = jax.random.normal(ks[11], (DEPTH, DA_WIDTH, D_MODEL), f32) * DA_WIDTH ** -0.5
    w_up_gdn = jax.random.normal(ks[12], (DEPTH, GDN_WIDTH, D_MODEL), f32) * GDN_WIDTH ** -0.5
    w_out = jax.random.normal(ks[13], (DEPTH, D_MODEL, D_MODEL), f32) * D_MODEL ** -0.5
    return {"x": x, "norm_w": norm_w, "w_in": w_in, "da_q_gain": da_q_gain,
            "da_k_gain": da_k_gain, "da_lambda": da_lambda, "da_out_gain": da_out_gain,
            "gdn_conv_w": gdn_conv_w, "gdn_a_log": gdn_a_log, "gdn_dt_bias": gdn_dt_bias,
            "gdn_norm_w": gdn_norm_w, "w_up_da": w_up_da, "w_up_gdn": w_up_gdn, "w_out": w_out}


def reference(x, norm_w, w_in, da_q_gain, da_k_gain, da_lambda, da_out_gain,
              gdn_conv_w, gdn_a_log, gdn_dt_bias, gdn_norm_w, w_up_da, w_up_gdn, w_out):
    B, S, _ = x.shape
    sizes = [DA_QK, DA_QK, DA_WIDTH, DA_WIDTH, GDN_QK, GDN_QK, GDN_WIDTH, GDN_WIDTH,
             GDN_HEADS, GDN_HEADS, N_BRANCHES * D_MODEL]
    split_points = [int(s) for s in np.cumsum(sizes)[:-1]]
    for l in range(DEPTH):
        lambda_init = lambda_init_for(l)
        h = rms_norm(x, norm_w[l])
        proj = jnp.einsum('bsd,de->bse', h, w_in[l])
        qa, ka, va, za, qb, kb, vb, zb, ab, bb, gate = jnp.split(proj, split_points, axis=-1)
        ya = diff_attention(qa, ka, va, da_q_gain[l], da_k_gain[l], da_lambda[l],
                            da_out_gain[l], lambda_init) * jax.nn.silu(za)
        ob = gated_deltanet(qb, kb, vb, ab, bb, gdn_conv_w[l], gdn_a_log[l], gdn_dt_bias[l])
        ob = rms_norm(ob, gdn_norm_w[l]).astype(x.dtype)
        yb = ob.reshape(B, S, GDN_WIDTH) * jax.nn.silu(zb)
        gates = jax.nn.sigmoid(gate.reshape(B, S, N_BRANCHES, D_MODEL))
        merged = (gates[:, :, 0] * jnp.einsum('bsw,wd->bsd', ya, w_up_da[l])
                  + gates[:, :, 1] * jnp.einsum('bsw,wd->bsd', yb, w_up_gdn[l]))
        x = x + jnp.einsum('bsd,de->bse', merged, w_out[l]).astype(x.dtype)
    return x
```

```python
import functools
import math

import jax
import jax.numpy as jnp
from jax import lax
from jax.experimental import pallas as pl
from jax.experimental.pallas import tpu as pltpu

F32 = jnp.float32
BF16 = jnp.bfloat16

CHUNK = 64
EPS = 1e-6
DA_HEADS = 4
DA_HEAD_DIM = 64
DA_V_DIM = 2 * DA_HEAD_DIM
GDN_HEADS = 4
GDN_K_DIM = 128
GDN_V_DIM = 128
CONV_WIDTH = 4
BRANCH = 512
LANES = 128
NEG_BIG = -1e30
VMEM_LIMIT = 56 * 1024 * 1024

HIGHEST = lax.Precision.HIGHEST


def _dot(a, b, precision=None):
    return jnp.dot(a, b, preferred_element_type=F32, precision=precision)


def _dot_nt(a, b, precision=None):
    return lax.dot_general(a, b, (((1,), (1,)), ((), ())),
                           preferred_element_type=F32, precision=precision)


def _dot_tn(a, b, precision=None):
    return lax.dot_general(a, b, (((0,), (0,)), ((), ())),
                           preferred_element_type=F32, precision=precision)


def _sigmoid(x):
    return 1.0 / (1.0 + jnp.exp(-x))


def _silu(x):
    return x * _sigmoid(x)


def _in_proj_kernel(x_ref, nw_ref, w_ref, wab_ref, qg_ref, kg_ref, gsum_ref,
                    qa_ref, ka_ref, va_ref, za_ref, qb_ref, kb_ref, vb_ref, zb_ref,
                    ab_ref, gate_ref):
    x = x_ref[...]
    ms = jnp.mean(x * x, axis=-1, keepdims=True)
    h = (x * lax.rsqrt(ms + EPS) * nw_ref[...]).astype(BF16)

    def proj(j, n=1):
        return _dot(h, w_ref[:, j * BRANCH:(j + n) * BRANCH])

    def qk_norm(t, gain):
        ssq = _dot((t * t).astype(BF16), gsum_ref[...])
        return (t * lax.rsqrt(ssq * (1.0 / DA_HEAD_DIM) + EPS) * gain).astype(BF16)

    qa_ref[...] = qk_norm(proj(0), qg_ref[...])
    ka_ref[...] = qk_norm(proj(1), kg_ref[...])
    va_ref[...] = proj(2).astype(BF16)
    za_ref[...] = _silu(proj(3)).astype(BF16)
    qb_ref[...] = proj(4).astype(BF16)
    kb_ref[...] = proj(5).astype(BF16)
    vb_ref[...] = proj(6).astype(BF16)
    zb_ref[...] = _silu(proj(7)).astype(BF16)
    gate_ref[...] = _sigmoid(proj(8, 4)).astype(BF16)
    ab_ref[...] = _dot(h, wab_ref[...])


def _in_proj(x2d, norm_w, w_main, w_ab, q_gain, k_gain, gsum, tm):
    n, d = x2d.shape
    const = lambda i: (0, 0)
    row = lambda i: (i, 0)
    seg = jax.ShapeDtypeStruct((n, BRANCH), BF16)
    out_shape = [seg] * 8 + [jax.ShapeDtypeStruct((n, LANES), F32),
                             jax.ShapeDtypeStruct((n, 2 * d), BF16)]
    out_specs = [pl.BlockSpec((tm, BRANCH), row)] * 8 + [
        pl.BlockSpec((tm, LANES), row), pl.BlockSpec((tm, 2 * d), row)]
    return pl.pallas_call(
        _in_proj_kernel,
        grid=(n // tm,),
        in_specs=[
            pl.BlockSpec((tm, d), row),
            pl.BlockSpec((1, d), const),
            pl.BlockSpec(w_main.shape, const, pipeline_mode=pl.Buffered(1)),
            pl.BlockSpec(w_ab.shape, const, pipeline_mode=pl.Buffered(1)),
            pl.BlockSpec((1, BRANCH), const),
            pl.BlockSpec((1, BRANCH), const),
            pl.BlockSpec((BRANCH, BRANCH), const, pipeline_mode=pl.Buffered(1)),
        ],
        out_specs=out_specs,
        out_shape=out_shape,
        compiler_params=pltpu.CompilerParams(
            dimension_semantics=("arbitrary",), vmem_limit_bytes=VMEM_LIMIT),
        name="in_proj",
    )(x2d, norm_w, w_main, w_ab, q_gain, k_gain, gsum)


def _diff_attn_kernel(lam_ref, q_ref, k_ref, v_ref, z_ref, og_ref, o_ref, *, tile, lambda_init):
    i = pl.program_id(2)
    q = q_ref[...]
    lane = lax.broadcasted_iota(jnp.int32, q.shape, 1)
    zero = jnp.zeros_like(q)
    q1 = jnp.where(lane < DA_HEAD_DIM, q, zero)
    q2 = jnp.where(lane >= DA_HEAD_DIM, q, zero)

    row_chunk = lax.broadcasted_iota(jnp.int32, (tile, tile), 0) // CHUNK
    col_chunk = lax.broadcasted_iota(jnp.int32, (tile, tile), 1) // CHUNK
    diag_mask = col_chunk <= row_chunk

    def step(j, carry, masked):
        start = pl.multiple_of(j * tile, tile)
        k = k_ref[pl.ds(start, tile), :]
        v = v_ref[pl.ds(start, tile), :]
        new = []
        for qm, (m, l, acc) in zip((q1, q2), carry):
            s = _dot_nt(qm, k)
            if masked:
                s = jnp.where(diag_mask, s, NEG_BIG)
            m_new = jnp.maximum(m, jnp.max(s, axis=-1, keepdims=True))
            alpha = jnp.exp(m - m_new)
            p = jnp.exp(s - m_new)
            l = alpha * l + jnp.sum(p, axis=-1, keepdims=True)
            acc = alpha * acc + _dot(p.astype(BF16), v)
            new.append((m_new, l, acc))
        return tuple(new)

    init_one = (jnp.full((tile, 1), NEG_BIG, F32), jnp.zeros((tile, 1), F32),
                jnp.zeros((tile, DA_V_DIM), F32))
    carry = lax.fori_loop(0, i, functools.partial(step, masked=False), (init_one, init_one))
    (_, l1, acc1), (_, l2, acc2) = step(i, carry, masked=True)

    lam_p = lam_ref[...]
    lam = (jnp.exp(jnp.sum(lam_p[0:1] * lam_p[1:2], axis=-1, keepdims=True))
           - jnp.exp(jnp.sum(lam_p[2:3] * lam_p[3:4], axis=-1, keepdims=True))
           + lambda_init)
    o = acc1 / l1 - lam * (acc2 / l2)
    ms = jnp.mean(o * o, axis=-1, keepdims=True)
    y = o * lax.rsqrt(ms + EPS) * og_ref[...] * (1.0 - lambda_init)
    o_ref[...] = (y * z_ref[...].astype(F32)).astype(o_ref.dtype)


def _diff_attn(qa, ka, va, za, da_lambda, out_gain, lambda_init, tile):
    b, s, _ = qa.shape
    blk = lambda bi, h, i: (bi, i, h)
    full = lambda bi, h, i: (bi, 0, h)
    const = lambda bi, h, i: (0, 0)
    return pl.pallas_call(
        functools.partial(_diff_attn_kernel, tile=tile, lambda_init=lambda_init),
        grid=(b, DA_HEADS, s // tile),
        in_specs=[
            pl.BlockSpec(da_lambda.shape, const),
            pl.BlockSpec((None, tile, DA_V_DIM), blk),
            pl.BlockSpec((None, s, DA_V_DIM), full),
            pl.BlockSpec((None, s, DA_V_DIM), full),
            pl.BlockSpec((None, tile, DA_V_DIM), blk),
            pl.BlockSpec((1, DA_V_DIM), const),
        ],
        out_specs=pl.BlockSpec((None, tile, DA_V_DIM), blk),
        out_shape=jax.ShapeDtypeStruct((b, s, DA_HEADS * DA_V_DIM), BF16),
        compiler_params=pltpu.CompilerParams(
            dimension_semantics=("arbitrary", "arbitrary", "arbitrary"),
            vmem_limit_bytes=VMEM_LIMIT),
        name="diff_attn",
    )(da_lambda, qa, ka, va, za, out_gain)


def _unit_lower_inverse(a_strict, eye, blk_mask):
    mm = functools.partial(_dot, precision=HIGHEST)
    d = jnp.where(blk_mask, a_strict, 0.0)
    off = a_strict - d
    x = -d
    t = eye + x
    p = x
    for _ in range(3):
        p = mm(p, p)
        t = t + mm(t, p)
    e = mm(t, off)
    e2 = mm(e, e)
    f = eye - e + e2 - mm(e, e2)
    return mm(f, t)


def _gdn_kernel(q_ref, k_ref, v_ref, ab_ref, z_ref, cw_ref, alog_ref, dtb_ref, nw_ref,
                o_ref, state_ref, prev_ref):
    c = pl.program_id(1)

    @pl.when(c == 0)
    def _():
        state_ref[...] = jnp.zeros_like(state_ref)
        prev_ref[...] = jnp.zeros_like(prev_ref)

    cur = jnp.concatenate([q_ref[...], k_ref[...], v_ref[...]], axis=-1).astype(F32)
    ext = jnp.concatenate([prev_ref[...], cur], axis=0)
    hist = prev_ref.shape[0]
    conv = jnp.zeros_like(cur)
    for j in range(CONV_WIDTH):
        lo = hist - (CONV_WIDTH - 1) + j
        conv = conv + ext[lo:lo + CHUNK, :] * cw_ref[j:j + 1, :]
    prev_ref[...] = cur[CHUNK - hist:, :]
    qkv = _silu(conv)

    ab = ab_ref[...]
    xg = ab + dtb_ref[...]
    softplus = jnp.maximum(xg, 0.0) + jnp.log1p(jnp.exp(-jnp.abs(xg)))
    g_all = -jnp.exp(alog_ref[...]) * softplus
    beta_all = _sigmoid(ab)

    row = lax.broadcasted_iota(jnp.int32, (CHUNK, CHUNK), 0)
    col = lax.broadcasted_iota(jnp.int32, (CHUNK, CHUNK), 1)
    tri_incl = col <= row
    tri_strict = col < row
    eye = jnp.where(row == col, 1.0, 0.0).astype(F32)
    blk_mask = (row // 16) == (col // 16)
    lower_ones = jnp.where(tri_incl, 1.0, 0.0).astype(F32)
    gc_all = _dot(lower_ones, g_all, precision=HIGHEST)
    lane = lax.broadcasted_iota(jnp.int32, (CHUNK, LANES), 1)

    for h in range(GDN_HEADS):
        sl = slice(h * GDN_K_DIM, (h + 1) * GDN_K_DIM)
        qh = qkv[:, sl]
        kh = qkv[:, BRANCH + h * GDN_K_DIM:BRANCH + (h + 1) * GDN_K_DIM]
        vh = qkv[:, 2 * BRANCH + h * GDN_V_DIM:2 * BRANCH + (h + 1) * GDN_V_DIM]
        qh = qh * lax.rsqrt(jnp.sum(qh * qh, axis=-1, keepdims=True) + EPS) * (GDN_K_DIM ** -0.5)
        kh = kh * lax.rsqrt(jnp.sum(kh * kh, axis=-1, keepdims=True) + EPS)
        beta = beta_all[:, GDN_HEADS + h:GDN_HEADS + h + 1]
        gc = gc_all[:, h:h + 1]
        gc_last = gc_all[CHUNK - 1:CHUNK, h:h + 1]
        onehot = jnp.where(lane == h, 1.0, 0.0).astype(F32)
        gc_row = _dot_nt(onehot, gc_all, precision=HIGHEST)
        gamma = jnp.exp(jnp.where(tri_incl, gc - gc_row, NEG_BIG))

        k_beta = kh * beta
        v_beta = vh * beta
        kb16 = kh.astype(BF16)
        a_strict = jnp.where(tri_strict, _dot_nt(k_beta.astype(BF16), kb16) * gamma, 0.0)
        t_inv = _unit_lower_inverse(a_strict, eye, blk_mask)
        exp_gc = jnp.exp(gc)
        u = _dot(t_inv, v_beta, precision=HIGHEST)
        w = _dot(t_inv, k_beta * exp_gc, precision=HIGHEST)
        a_intra = jnp.where(tri_incl, _dot_nt(qh.astype(BF16), kb16) * gamma, 0.0)
        q_dec = qh * exp_gc
        k_tail = kh * jnp.exp(gc_last - gc)

        state = state_ref[h]
        s16 = state.astype(BF16)
        v_new = u - _dot(w.astype(BF16), s16)
        v16 = v_new.astype(BF16)
        o = _dot(q_dec.astype(BF16), s16) + _dot(a_intra.astype(BF16), v16)
        state_ref[h] = state * jnp.exp(gc_last) + _dot_tn(k_tail.astype(BF16), v16)

        ms = jnp.mean(o * o, axis=-1, keepdims=True)
        y = o * lax.rsqrt(ms + EPS) * nw_ref[...]
        o_ref[:, sl] = (y * z_ref[:, sl].astype(F32)).astype(o_ref.dtype)


def _gdn(qb, kb, vb, ab, zb, conv_w, alog_pad, dtb_pad, norm_w):
    b, s, _ = qb.shape
    blk = lambda bi, c: (bi, c, 0)
    const = lambda bi, c: (0, 0)
    return pl.pallas_call(
        _gdn_kernel,
        grid=(b, s // CHUNK),
        in_specs=[
            pl.BlockSpec((None, CHUNK, BRANCH), blk),
            pl.BlockSpec((None, CHUNK, BRANCH), blk),
            pl.BlockSpec((None, CHUNK, BRANCH), blk),
            pl.BlockSpec((None, CHUNK, LANES), blk),
            pl.BlockSpec((None, CHUNK, BRANCH), blk),
            pl.BlockSpec(conv_w.shape, const),
            pl.BlockSpec((1, LANES), const),
            pl.BlockSpec((1, LANES), const),
            pl.BlockSpec((1, GDN_V_DIM), const),
        ],
        out_specs=pl.BlockSpec((None, CHUNK, BRANCH), blk),
        out_shape=jax.ShapeDtypeStruct((b, s, BRANCH), BF16),
        scratch_shapes=[pltpu.VMEM((GDN_HEADS, GDN_K_DIM, GDN_V_DIM), F32),
                        pltpu.VMEM((8, 3 * BRANCH), F32)],
        compiler_params=pltpu.CompilerParams(
            dimension_semantics=("arbitrary", "arbitrary"), vmem_limit_bytes=VMEM_LIMIT),
        name="gdn",
    )(qb, kb, vb, ab, zb, conv_w, alog_pad, dtb_pad, norm_w)


def _out_proj_kernel(x_ref, ya_ref, yb_ref, gate_ref, wa_ref, wb_ref, wo_ref, o_ref):
    d = x_ref.shape[-1]
    up_a = _dot(ya_ref[...], wa_ref[...])
    up_b = _dot(yb_ref[...], wb_ref[...])
    merged = (gate_ref[:, :d].astype(F32) * up_a + gate_ref[:, d:].astype(F32) * up_b)
    o_ref[...] = x_ref[...] + _dot(merged.astype(BF16), wo_ref[...])


def _out_proj(x2d, ya, yb, gates, w_up_a, w_up_b, w_out, tm):
    n, d = x2d.shape
    const = lambda i: (0, 0)
    row = lambda i: (i, 0)
    return pl.pallas_call(
        _out_proj_kernel,
        grid=(n // tm,),
        in_specs=[
            pl.BlockSpec((tm, d), row),
            pl.BlockSpec((tm, BRANCH), row),
            pl.BlockSpec((tm, BRANCH), row),
            pl.BlockSpec((tm, 2 * d), row),
            pl.BlockSpec(w_up_a.shape, const),
            pl.BlockSpec(w_up_b.shape, const),
            pl.BlockSpec(w_out.shape, const),
        ],
        out_specs=pl.BlockSpec((tm, d), row),
        out_shape=jax.ShapeDtypeStruct((n, d), x2d.dtype),
        compiler_params=pltpu.CompilerParams(
            dimension_semantics=("arbitrary",), vmem_limit_bytes=VMEM_LIMIT),
        name="out_proj",
    )(x2d, ya, yb, gates, w_up_a, w_up_b, w_out)


def _lambda_init_for(layer):
    return 0.8 - 0.6 * math.exp(-0.3 * layer)


def kernel(x, norm_w, w_in, da_q_gain, da_k_gain, da_lambda, da_out_gain, gdn_conv_w, gdn_a_log,
           gdn_dt_bias, gdn_norm_w, w_up_da, w_up_gdn, w_out):
    b, s, d = x.shape
    depth = norm_w.shape[0]
    n_main = 8 * BRANCH
    row_tile = 512
    attn_tile = 256
    group = jnp.arange(BRANCH) // DA_HEAD_DIM
    gsum = (group[:, None] == group[None, :]).astype(BF16)
    groups_per_seg = BRANCH // DA_HEAD_DIM

    x2d = x.reshape(b * s, d)
    for l in range(depth):
        w = w_in[l]
        w_main = jnp.concatenate([w[:, :n_main], w[:, n_main + 2 * GDN_HEADS:]], axis=1).astype(BF16)
        w_ab = jnp.pad(w[:, n_main:n_main + 2 * GDN_HEADS],
                       ((0, 0), (0, LANES - 2 * GDN_HEADS))).astype(BF16)
        q_gain = jnp.tile(da_q_gain[l] * (DA_HEAD_DIM ** -0.5), groups_per_seg)[None, :]
        k_gain = jnp.tile(da_k_gain[l], groups_per_seg)[None, :]
        qa, ka, va, za, qb, kb, vb, zb, ab, gates = _in_proj(
            x2d, norm_w[l][None, :], w_main, w_ab, q_gain, k_gain, gsum, row_tile)

        r3 = lambda t: t.reshape(b, s, t.shape[-1])
        ya = _diff_attn(r3(qa), r3(ka), r3(va), r3(za), da_lambda[l], da_out_gain[l][None, :],
                        _lambda_init_for(l), attn_tile)

        alog_pad = jnp.pad(gdn_a_log[l], (0, LANES - GDN_HEADS))[None, :]
        dtb_pad = jnp.pad(gdn_dt_bias[l], (0, LANES - GDN_HEADS))[None, :]
        yb = _gdn(r3(qb), r3(kb), r3(vb), r3(ab), r3(zb), gdn_conv_w[l], alog_pad, dtb_pad,
                  gdn_norm_w[l][None, :])

        x2d = _out_proj(x2d, ya.reshape(b * s, -1), yb.reshape(b * s, -1), gates,
                        w_up_da[l].astype(BF16), w_up_gdn[l].astype(BF16), w_out[l].astype(BF16),
                        row_tile)
    return x2d.reshape(b, s, d)
```

```python
import functools
import math

import jax
import jax.numpy as jnp
from jax import lax
from jax.experimental import pallas as pl
from jax.experimental.pallas import tpu as pltpu

F32 = jnp.float32
BF16 = jnp.bfloat16

CHUNK = 64
EPS = 1e-6
DA_HEADS = 4
DA_HEAD_DIM = 64
DA_V_DIM = 2 * DA_HEAD_DIM
GDN_HEADS = 4
GDN_K_DIM = 128
GDN_V_DIM = 128
CONV_WIDTH = 4
BRANCH = 512
LANES = 128
NEG_BIG = -1e30
ONES_ROWS = 16
VMEM_LIMIT = 56 * 1024 * 1024

HIGHEST = lax.Precision.HIGHEST


def _dot(a, b, precision=None):
    return jnp.dot(a, b, preferred_element_type=F32, precision=precision)


def _dot_nt(a, b, precision=None):
    return lax.dot_general(a, b, (((1,), (1,)), ((), ())),
                           preferred_element_type=F32, precision=precision)


def _dot_tn(a, b, precision=None):
    return lax.dot_general(a, b, (((0,), (0,)), ((), ())),
                           preferred_element_type=F32, precision=precision)


def _sigmoid(x):
    return 1.0 / (1.0 + jnp.exp(-x))


def _silu(x):
    return x * _sigmoid(x)


def _in_proj_kernel(x_ref, nw_ref, w_ref, wvt_ref, wab_ref, qg_ref, kg_ref, gsum_ref,
                    qa_ref, ka_ref, vat_ref, za_ref, qb_ref, kb_ref, vb_ref, zb_ref,
                    ab_ref, gate_ref):
    x = x_ref[...]
    ms = jnp.mean(x * x, axis=-1, keepdims=True)
    h = (x * lax.rsqrt(ms + EPS) * nw_ref[...]).astype(BF16)

    def proj(j, n=1):
        return _dot(h, w_ref[:, j * BRANCH:(j + n) * BRANCH])

    def qk_norm(t, gain):
        ssq = _dot((t * t).astype(BF16), gsum_ref[...])
        return (t * lax.rsqrt(ssq * (1.0 / DA_HEAD_DIM) + EPS) * gain).astype(BF16)

    qa_ref[...] = qk_norm(proj(0), qg_ref[...])
    ka_ref[...] = qk_norm(proj(1), kg_ref[...])
    vat_ref[...] = _dot_nt(wvt_ref[...], h).astype(BF16)
    za_ref[...] = _silu(proj(2)).astype(BF16)
    qb_ref[...] = proj(3).astype(BF16)
    kb_ref[...] = proj(4).astype(BF16)
    vb_ref[...] = proj(5).astype(BF16)
    zb_ref[...] = _silu(proj(6)).astype(BF16)
    gate_ref[...] = _sigmoid(proj(7, 4)).astype(BF16)
    ab_ref[...] = _dot(h, wab_ref[...])


def _in_proj(x2d, norm_w, w_main, w_vt, w_ab, q_gain, k_gain, gsum, tm):
    n, d = x2d.shape
    const = lambda i: (0, 0)
    row = lambda i: (i, 0)
    seg = jax.ShapeDtypeStruct((n, BRANCH), BF16)
    seg_spec = pl.BlockSpec((tm, BRANCH), row)
    out_shape = [seg, seg, jax.ShapeDtypeStruct((BRANCH, n), BF16)] + [seg] * 5 + [
        jax.ShapeDtypeStruct((n, LANES), F32), jax.ShapeDtypeStruct((n, 2 * d), BF16)]
    out_specs = ([seg_spec, seg_spec, pl.BlockSpec((BRANCH, tm), lambda i: (0, i))]
                 + [seg_spec] * 5
                 + [pl.BlockSpec((tm, LANES), row), pl.BlockSpec((tm, 2 * d), row)])
    return pl.pallas_call(
        _in_proj_kernel,
        grid=(n // tm,),
        in_specs=[
            pl.BlockSpec((tm, d), row),
            pl.BlockSpec((1, d), const),
            pl.BlockSpec(w_main.shape, const, pipeline_mode=pl.Buffered(1)),
            pl.BlockSpec(w_vt.shape, const, pipeline_mode=pl.Buffered(1)),
            pl.BlockSpec(w_ab.shape, const, pipeline_mode=pl.Buffered(1)),
            pl.BlockSpec((1, BRANCH), const),
            pl.BlockSpec((1, BRANCH), const),
            pl.BlockSpec((BRANCH, BRANCH), const, pipeline_mode=pl.Buffered(1)),
        ],
        out_specs=out_specs,
        out_shape=out_shape,
        compiler_params=pltpu.CompilerParams(
            dimension_semantics=("arbitrary",), vmem_limit_bytes=VMEM_LIMIT),
        name="in_proj",
    )(x2d, norm_w, w_main, w_vt, w_ab, q_gain, k_gain, gsum)


def _diff_attn_kernel(lam_ref, q_ref, k_ref, vt_ref, z_ref, og_ref, o_ref, s_scr, m_scr, acc_scr,
                      *, tq, tk, lambda_init):
    i = pl.program_id(2)
    q = q_ref[...]
    lane = lax.broadcasted_iota(jnp.int32, q.shape, 1)
    zero = jnp.zeros_like(q)
    qq = jnp.concatenate([jnp.where(lane < DA_HEAD_DIM, q, zero),
                          jnp.where(lane >= DA_HEAD_DIM, q, zero)], axis=0)
    sub_tiles = s_scr.shape[0]
    ones_rows = jnp.ones((ONES_ROWS, tk), BF16)
    m_scr[...] = jnp.full(m_scr.shape, NEG_BIG, F32)
    acc_scr[...] = jnp.zeros(acc_scr.shape, F32)

    def issue(start, slot):
        s = _dot_nt(k_ref[pl.ds(start, tk), :], qq)
        s_scr[slot] = s
        return jnp.max(s, axis=0, keepdims=True)

    def consume(start, slot, tile_max, masked):
        s = s_scr[slot]
        if masked:
            k_chunk = (lax.broadcasted_iota(jnp.int32, s.shape, 0) + slot * tk) // CHUNK
            q_chunk = (lax.broadcasted_iota(jnp.int32, s.shape, 1) % tq) // CHUNK
            valid = k_chunk <= q_chunk
            s = jnp.where(valid, s, NEG_BIG)
            tile_max = jnp.max(s, axis=0, keepdims=True)
        m = m_scr[...]
        m_new = jnp.maximum(m, tile_max)
        m_scr[...] = m_new
        alpha = jnp.exp2(m - m_new)
        p = jnp.exp2(s - m_new)
        if masked:
            p = jnp.where(valid, p, 0.0)
        pb = p.astype(BF16)
        vt = jnp.concatenate([vt_ref[:, pl.ds(start, tk)], ones_rows], axis=0)
        for g in range(2):
            cols = slice(g * tq, (g + 1) * tq)
            acc_scr[g] = alpha[:, cols] * acc_scr[g] + _dot(vt, pb[:, cols])

    def body(j, tile_max):
        for t in range(sub_tiles):
            start = pl.multiple_of(j * tq + t * tk, tk)
            next_max = issue(start + tk, (t + 1) % sub_tiles)
            consume(start, t, tile_max, masked=False)
            tile_max = next_max
        return tile_max

    tile_max = lax.fori_loop(0, i, body, issue(0, 0))
    for t in range(sub_tiles):
        start = pl.multiple_of(i * tq + t * tk, tk)
        if t + 1 < sub_tiles:
            issue(start + tk, t + 1)
        consume(start, t, None, masked=True)
    acc1 = acc_scr[0]
    acc2 = acc_scr[1]

    lam_p = lam_ref[...]
    lam = (jnp.exp(jnp.sum(lam_p[0:1] * lam_p[1:2], axis=-1, keepdims=True))
           - jnp.exp(jnp.sum(lam_p[2:3] * lam_p[3:4], axis=-1, keepdims=True))
           + lambda_init)
    ot = (acc1[:DA_V_DIM] / acc1[DA_V_DIM:DA_V_DIM + 1]
          - lam * (acc2[:DA_V_DIM] / acc2[DA_V_DIM:DA_V_DIM + 1]))
    ms = jnp.mean(ot * ot, axis=0, keepdims=True)
    y = (ot * lax.rsqrt(ms + EPS)).T * og_ref[...] * (1.0 - lambda_init)
    o_ref[...] = (y * z_ref[...].astype(F32)).astype(o_ref.dtype)


def _diff_attn(qa, ka, vat, za, da_lambda, out_gain, lambda_init, tq, tk):
    b, s, _ = qa.shape
    sub_tiles = tq // tk
    assert sub_tiles >= 2 and sub_tiles * tk == tq
    blk = lambda bi, h, i: (bi, i, h)
    const = lambda bi, h, i: (0, 0)
    return pl.pallas_call(
        functools.partial(_diff_attn_kernel, tq=tq, tk=tk, lambda_init=lambda_init),
        scratch_shapes=[pltpu.VMEM((sub_tiles, tk, 2 * tq), F32),
                        pltpu.VMEM((1, 2 * tq), F32),
                        pltpu.VMEM((2, DA_V_DIM + ONES_ROWS, tq), F32)],
        grid=(b, DA_HEADS, s // tq),
        in_specs=[
            pl.BlockSpec(da_lambda.shape, const),
            pl.BlockSpec((None, tq, DA_V_DIM), blk),
            pl.BlockSpec((None, s, DA_V_DIM), lambda bi, h, i: (bi, 0, h)),
            pl.BlockSpec((DA_V_DIM, s), lambda bi, h, i: (h, bi)),
            pl.BlockSpec((None, tq, DA_V_DIM), blk),
            pl.BlockSpec((1, DA_V_DIM), const),
        ],
        out_specs=pl.BlockSpec((None, tq, DA_V_DIM), blk),
        out_shape=jax.ShapeDtypeStruct((b, s, DA_HEADS * DA_V_DIM), BF16),
        compiler_params=pltpu.CompilerParams(
            dimension_semantics=("arbitrary", "arbitrary", "arbitrary"),
            vmem_limit_bytes=VMEM_LIMIT),
        name="diff_attn",
    )(da_lambda, qa, ka, vat, za, out_gain)


def _unit_lower_inverse(a_strict, eye, blk_mask):
    mm = functools.partial(_dot, precision=HIGHEST)
    d = jnp.where(blk_mask, a_strict, 0.0)
    off = a_strict - d
    x = -d
    t = eye + x
    p = x
    for _ in range(3):
        p = mm(p, p)
        t = t + mm(t, p)
    e = mm(t, off)
    e2 = mm(e, e)
    f = eye - e + e2 - mm(e, e2)
    return mm(f, t)


def _gdn_kernel(q_ref, k_ref, v_ref, ab_ref, z_ref, cw_ref, alog_ref, dtb_ref, nw_ref,
                o_ref, state_ref, prev_ref):
    c = pl.program_id(1)

    @pl.when(c == 0)
    def _():
        state_ref[...] = jnp.zeros_like(state_ref)
        prev_ref[...] = jnp.zeros_like(prev_ref)

    cur = jnp.concatenate([q_ref[...], k_ref[...], v_ref[...]], axis=-1).astype(F32)
    ext = jnp.concatenate([prev_ref[...], cur], axis=0)
    hist = prev_ref.shape[0]
    conv = jnp.zeros_like(cur)
    for j in range(CONV_WIDTH):
        lo = hist - (CONV_WIDTH - 1) + j
        conv = conv + ext[lo:lo + CHUNK, :] * cw_ref[j:j + 1, :]
    prev_ref[...] = cur[CHUNK - hist:, :]
    qkv = _silu(conv)

    ab = ab_ref[...]
    xg = ab + dtb_ref[...]
    softplus = jnp.maximum(xg, 0.0) + jnp.log1p(jnp.exp(-jnp.abs(xg)))
    g_all = -jnp.exp(alog_ref[...]) * softplus
    beta_all = _sigmoid(ab)

    row = lax.broadcasted_iota(jnp.int32, (CHUNK, CHUNK), 0)
    col = lax.broadcasted_iota(jnp.int32, (CHUNK, CHUNK), 1)
    tri_incl = col <= row
    tri_strict = col < row
    eye = jnp.where(row == col, 1.0, 0.0).astype(F32)
    blk_mask = (row // 16) == (col // 16)
    lower_ones = jnp.where(tri_incl, 1.0, 0.0).astype(F32)
    gc_all = _dot(lower_ones, g_all, precision=HIGHEST)
    lane = lax.broadcasted_iota(jnp.int32, (CHUNK, LANES), 1)

    for h in range(GDN_HEADS):
        sl = slice(h * GDN_K_DIM, (h + 1) * GDN_K_DIM)
        qh = qkv[:, sl]
        kh = qkv[:, BRANCH + h * GDN_K_DIM:BRANCH + (h + 1) * GDN_K_DIM]
        vh = qkv[:, 2 * BRANCH + h * GDN_V_DIM:2 * BRANCH + (h + 1) * GDN_V_DIM]
        qh = qh * lax.rsqrt(jnp.sum(qh * qh, axis=-1, keepdims=True) + EPS) * (GDN_K_DIM ** -0.5)
        kh = kh * lax.rsqrt(jnp.sum(kh * kh, axis=-1, keepdims=True) + EPS)
        beta = beta_all[:, GDN_HEADS + h:GDN_HEADS + h + 1]
        gc = gc_all[:, h:h + 1]
        gc_last = gc_all[CHUNK - 1:CHUNK, h:h + 1]
        onehot = jnp.where(lane == h, 1.0, 0.0).astype(F32)
        gc_row = _dot_nt(onehot, gc_all, precision=HIGHEST)
        gamma = jnp.exp(jnp.where(tri_incl, gc - gc_row, NEG_BIG))

        k_beta = kh * beta
        v_beta = vh * beta
        kb16 = kh.astype(BF16)
        a_strict = jnp.where(tri_strict, _dot_nt(k_beta.astype(BF16), kb16) * gamma, 0.0)
        t_inv = _unit_lower_inverse(a_strict, eye, blk_mask)
        exp_gc = jnp.exp(gc)
        u = _dot(t_inv, v_beta, precision=HIGHEST)
        w = _dot(t_inv, k_beta * exp_gc, precision=HIGHEST)
        a_intra = jnp.where(tri_incl, _dot_nt(qh.astype(BF16), kb16) * gamma, 0.0)
        q_dec = qh * exp_gc
        k_tail = kh * jnp.exp(gc_last - gc)

        state = state_ref[h]
        s16 = state.astype(BF16)
        v_new = u - _dot(w.astype(BF16), s16)
        v16 = v_new.astype(BF16)
        o = _dot(q_dec.astype(BF16), s16) + _dot(a_intra.astype(BF16), v16)
        state_ref[h] = state * jnp.exp(gc_last) + _dot_tn(k_tail.astype(BF16), v16)

        ms = jnp.mean(o * o, axis=-1, keepdims=True)
        y = o * lax.rsqrt(ms + EPS) * nw_ref[...]
        o_ref[:, sl] = (y * z_ref[:, sl].astype(F32)).astype(o_ref.dtype)


def _gdn(qb, kb, vb, ab, zb, conv_w, alog_pad, dtb_pad, norm_w):
    b, s, _ = qb.shape
    blk = lambda bi, c: (bi, c, 0)
    const = lambda bi, c: (0, 0)
    return pl.pallas_call(
        _gdn_kernel,
        grid=(b, s // CHUNK),
        in_specs=[
            pl.BlockSpec((None, CHUNK, BRANCH), blk),
            pl.BlockSpec((None, CHUNK, BRANCH), blk),
            pl.BlockSpec((None, CHUNK, BRANCH), blk),
            pl.BlockSpec((None, CHUNK, LANES), blk),
            pl.BlockSpec((None, CHUNK, BRANCH), blk),
            pl.BlockSpec(conv_w.shape, const),
            pl.BlockSpec((1, LANES), const),
            pl.BlockSpec((1, LANES), const),
            pl.BlockSpec((1, GDN_V_DIM), const),
        ],
        out_specs=pl.BlockSpec((None, CHUNK, BRANCH), blk),
        out_shape=jax.ShapeDtypeStruct((b, s, BRANCH), BF16),
        scratch_shapes=[pltpu.VMEM((GDN_HEADS, GDN_K_DIM, GDN_V_DIM), F32),
                        pltpu.VMEM((8, 3 * BRANCH), F32)],
        compiler_params=pltpu.CompilerParams(
            dimension_semantics=("arbitrary", "arbitrary"), vmem_limit_bytes=VMEM_LIMIT),
        name="gdn",
    )(qb, kb, vb, ab, zb, conv_w, alog_pad, dtb_pad, norm_w)


def _out_proj_kernel(x_ref, ya_ref, yb_ref, gate_ref, wa_ref, wb_ref, wo_ref, o_ref):
    d = x_ref.shape[-1]
    up_a = _dot(ya_ref[...], wa_ref[...])
    up_b = _dot(yb_ref[...], wb_ref[...])
    merged = (gate_ref[:, :d].astype(F32) * up_a + gate_ref[:, d:].astype(F32) * up_b)
    o_ref[...] = x_ref[...] + _dot(merged.astype(BF16), wo_ref[...])


def _out_proj(x2d, ya, yb, gates, w_up_a, w_up_b, w_out, tm):
    n, d = x2d.shape
    const = lambda i: (0, 0)
    row = lambda i: (i, 0)
    return pl.pallas_call(
        _out_proj_kernel,
        grid=(n // tm,),
        in_specs=[
            pl.BlockSpec((tm, d), row),
            pl.BlockSpec((tm, BRANCH), row),
            pl.BlockSpec((tm, BRANCH), row),
            pl.BlockSpec((tm, 2 * d), row),
            pl.BlockSpec(w_up_a.shape, const),
            pl.BlockSpec(w_up_b.shape, const),
            pl.BlockSpec(w_out.shape, const),
        ],
        out_specs=pl.BlockSpec((tm, d), row),
        out_shape=jax.ShapeDtypeStruct((n, d), x2d.dtype),
        compiler_params=pltpu.CompilerParams(
            dimension_semantics=("arbitrary",), vmem_limit_bytes=VMEM_LIMIT),
        name="out_proj",
    )(x2d, ya, yb, gates, w_up_a, w_up_b, w_out)


def _lambda_init_for(layer):
    return 0.8 - 0.6 * math.exp(-0.3 * layer)


def kernel(x, norm_w, w_in, da_q_gain, da_k_gain, da_lambda, da_out_gain, gdn_conv_w, gdn_a_log,
           gdn_dt_bias, gdn_norm_w, w_up_da, w_up_gdn, w_out):
    b, s, d = x.shape
    depth = norm_w.shape[0]
    n_main = 8 * BRANCH
    row_tile = 512
    attn_q_tile = 512
    attn_k_tile = 256
    group = jnp.arange(BRANCH) // DA_HEAD_DIM
    gsum = (group[:, None] == group[None, :]).astype(BF16)
    groups_per_seg = BRANCH // DA_HEAD_DIM
    q_scale = DA_HEAD_DIM ** -0.5 * math.log2(math.e)

    x2d = x.reshape(b * s, d)
    for l in range(depth):
        w = w_in[l]
        w_main = jnp.concatenate([w[:, :2 * BRANCH], w[:, 3 * BRANCH:n_main],
                                  w[:, n_main + 2 * GDN_HEADS:]], axis=1).astype(BF16)
        w_vt = w[:, 2 * BRANCH:3 * BRANCH].T.astype(BF16)
        w_ab = jnp.pad(w[:, n_main:n_main + 2 * GDN_HEADS],
                       ((0, 0), (0, LANES - 2 * GDN_HEADS))).astype(BF16)
        q_gain = jnp.tile(da_q_gain[l] * q_scale, groups_per_seg)[None, :]
        k_gain = jnp.tile(da_k_gain[l], groups_per_seg)[None, :]
        qa, ka, vat, za, qb, kb, vb, zb, ab, gates = _in_proj(
            x2d, norm_w[l][None, :], w_main, w_vt, w_ab, q_gain, k_gain, gsum, row_tile)

        r3 = lambda t: t.reshape(b, s, t.shape[-1])
        ya = _diff_attn(r3(qa), r3(ka), vat, r3(za), da_lambda[l], da_out_gain[l][None, :],
                        _lambda_init_for(l), attn_q_tile, attn_k_tile)

        alog_pad = jnp.pad(gdn_a_log[l], (0, LANES - GDN_HEADS))[None, :]
        dtb_pad = jnp.pad(gdn_dt_bias[l], (0, LANES - GDN_HEADS))[None, :]
        yb = _gdn(r3(qb), r3(kb), r3(vb), r3(ab), r3(zb), gdn_conv_w[l], alog_pad, dtb_pad,
                  gdn_norm_w[l][None, :])

        x2d = _out_proj(x2d, ya.reshape(b * s, -1), yb.reshape(b * s, -1), gates,
                        w_up_da[l].astype(BF16), w_up_gdn[l].astype(BF16), w_out[l].astype(BF16),
                        row_tile)
    return x2d.reshape(b, s, d)
```

```python
import functools
import math

import jax
import jax.numpy as jnp
from jax import lax
from jax.experimental import pallas as pl
from jax.experimental.pallas import tpu as pltpu

F32 = jnp.float32
BF16 = jnp.bfloat16

CHUNK = 64
EPS = 1e-6
DA_HEADS = 4
DA_HEAD_DIM = 64
DA_V_DIM = 2 * DA_HEAD_DIM
GDN_HEADS = 4
GDN_K_DIM = 128
GDN_V_DIM = 128
CONV_WIDTH = 4
BRANCH = 512
LANES = 128
NEG_BIG = -1e30
ONES_ROWS = 16
VMEM_LIMIT = 56 * 1024 * 1024

HIGHEST = lax.Precision.HIGHEST


def _dot(a, b, precision=None):
    return jnp.dot(a, b, preferred_element_type=F32, precision=precision)


def _dot_nt(a, b, precision=None):
    return lax.dot_general(a, b, (((1,), (1,)), ((), ())),
                           preferred_element_type=F32, precision=precision)


def _dot_tn(a, b, precision=None):
    return lax.dot_general(a, b, (((0,), (0,)), ((), ())),
                           preferred_element_type=F32, precision=precision)


def _sigmoid(x):
    return 1.0 / (1.0 + jnp.exp(-x))


def _silu(x):
    return x * _sigmoid(x)


def _in_proj_kernel(x_ref, nw_ref, w_ref, wvt_ref, wab_ref, qg_ref, kg_ref, gsum_ref,
                    qa_ref, ka_ref, vat_ref, za_ref, qb_ref, kb_ref, vb_ref, zb_ref,
                    ab_ref, gate_ref):
    x = x_ref[...]
    ms = jnp.mean(x * x, axis=-1, keepdims=True)
    h = (x * lax.rsqrt(ms + EPS) * nw_ref[...]).astype(BF16)

    def proj(j, n=1):
        return _dot(h, w_ref[:, j * BRANCH:(j + n) * BRANCH])

    def qk_norm(t, gain):
        ssq = _dot((t * t).astype(BF16), gsum_ref[...])
        return (t * lax.rsqrt(ssq * (1.0 / DA_HEAD_DIM) + EPS) * gain).astype(BF16)

    qa_ref[...] = qk_norm(proj(0), qg_ref[...])
    ka_ref[...] = qk_norm(proj(1), kg_ref[...])
    vat_ref[...] = _dot_nt(wvt_ref[...], h).astype(BF16)
    za_ref[...] = _silu(proj(2)).astype(BF16)
    qb_ref[...] = proj(3).astype(BF16)
    kb_ref[...] = proj(4).astype(BF16)
    vb_ref[...] = proj(5).astype(BF16)
    zb_ref[...] = _silu(proj(6)).astype(BF16)
    gate_ref[...] = _sigmoid(proj(7, 4)).astype(BF16)
    ab_ref[...] = _dot(h, wab_ref[...])


def _in_proj(x2d, norm_w, w_main, w_vt, w_ab, q_gain, k_gain, gsum, tm):
    n, d = x2d.shape
    const = lambda i: (0, 0)
    row = lambda i: (i, 0)
    seg = jax.ShapeDtypeStruct((n, BRANCH), BF16)
    seg_spec = pl.BlockSpec((tm, BRANCH), row)
    out_shape = [seg, seg, jax.ShapeDtypeStruct((BRANCH, n), BF16)] + [seg] * 5 + [
        jax.ShapeDtypeStruct((n, LANES), F32), jax.ShapeDtypeStruct((n, 2 * d), BF16)]
    out_specs = ([seg_spec, seg_spec, pl.BlockSpec((BRANCH, tm), lambda i: (0, i))]
                 + [seg_spec] * 5
                 + [pl.BlockSpec((tm, LANES), row), pl.BlockSpec((tm, 2 * d), row)])
    return pl.pallas_call(
        _in_proj_kernel,
        grid=(n // tm,),
        in_specs=[
            pl.BlockSpec((tm, d), row),
            pl.BlockSpec((1, d), const),
            pl.BlockSpec(w_main.shape, const, pipeline_mode=pl.Buffered(1)),
            pl.BlockSpec(w_vt.shape, const, pipeline_mode=pl.Buffered(1)),
            pl.BlockSpec(w_ab.shape, const, pipeline_mode=pl.Buffered(1)),
            pl.BlockSpec((1, BRANCH), const),
            pl.BlockSpec((1, BRANCH), const),
            pl.BlockSpec((BRANCH, BRANCH), const, pipeline_mode=pl.Buffered(1)),
        ],
        out_specs=out_specs,
        out_shape=out_shape,
        compiler_params=pltpu.CompilerParams(
            dimension_semantics=("arbitrary",), vmem_limit_bytes=VMEM_LIMIT),
        name="in_proj",
    )(x2d, norm_w, w_main, w_vt, w_ab, q_gain, k_gain, gsum)


def _diff_attn_kernel(lam_ref, q_ref, k_ref, vt_ref, z_ref, og_ref, o_ref, s_scr, m_scr, acc_scr,
                      *, tq, tk, lambda_init):
    i = pl.program_id(2)
    q = q_ref[...]
    lane = lax.broadcasted_iota(jnp.int32, q.shape, 1)
    zero = jnp.zeros_like(q)
    qq = jnp.concatenate([jnp.where(lane < DA_HEAD_DIM, q, zero),
                          jnp.where(lane >= DA_HEAD_DIM, q, zero)], axis=0)
    sub_tiles = s_scr.shape[0]
    ones_rows = jnp.ones((ONES_ROWS, tk), BF16)
    m_scr[...] = jnp.full(m_scr.shape, NEG_BIG, F32)
    acc_scr[...] = jnp.zeros(acc_scr.shape, F32)

    def issue(start, slot):
        s = _dot_nt(k_ref[pl.ds(start, tk), :], qq)
        s_scr[slot] = s
        return jnp.max(s, axis=0, keepdims=True)

    def consume(start, slot, tile_max, masked):
        s = s_scr[slot]
        if masked:
            k_chunk = (lax.broadcasted_iota(jnp.int32, s.shape, 0) + slot * tk) // CHUNK
            q_chunk = (lax.broadcasted_iota(jnp.int32, s.shape, 1) % tq) // CHUNK
            valid = k_chunk <= q_chunk
            s = jnp.where(valid, s, NEG_BIG)
            tile_max = jnp.max(s, axis=0, keepdims=True)
        m = m_scr[...]
        m_new = jnp.maximum(m, tile_max)
        m_scr[...] = m_new
        alpha = jnp.exp2(m - m_new)
        p = jnp.exp2(s - m_new)
        if masked:
            p = jnp.where(valid, p, 0.0)
        pb = p.astype(BF16)
        vt = jnp.concatenate([vt_ref[:, pl.ds(start, tk)], ones_rows], axis=0)
        for g in range(2):
            cols = slice(g * tq, (g + 1) * tq)
            acc_scr[g] = alpha[:, cols] * acc_scr[g] + _dot(vt, pb[:, cols])

    def body(j, tile_max):
        for t in range(sub_tiles):
            start = pl.multiple_of(j * tq + t * tk, tk)
            next_max = issue(start + tk, (t + 1) % sub_tiles)
            consume(start, t, tile_max, masked=False)
            tile_max = next_max
        return tile_max

    tile_max = lax.fori_loop(0, i, body, issue(0, 0))
    for t in range(sub_tiles):
        start = pl.multiple_of(i * tq + t * tk, tk)
        if t + 1 < sub_tiles:
            issue(start + tk, t + 1)
        consume(start, t, None, masked=True)
    acc1 = acc_scr[0]
    acc2 = acc_scr[1]

    lam_p = lam_ref[...]
    lam = (jnp.exp(jnp.sum(lam_p[0:1] * lam_p[1:2], axis=-1, keepdims=True))
           - jnp.exp(jnp.sum(lam_p[2:3] * lam_p[3:4], axis=-1, keepdims=True))
           + lambda_init)
    ot = (acc1[:DA_V_DIM] / acc1[DA_V_DIM:DA_V_DIM + 1]
          - lam * (acc2[:DA_V_DIM] / acc2[DA_V_DIM:DA_V_DIM + 1]))
    ms = jnp.mean(ot * ot, axis=0, keepdims=True)
    y = (ot * lax.rsqrt(ms + EPS)).T * og_ref[...] * (1.0 - lambda_init)
    o_ref[...] = (y * z_ref[...].astype(F32)).astype(o_ref.dtype)


def _diff_attn(qa, ka, vat, za, da_lambda, out_gain, lambda_init, tq, tk):
    b, s, _ = qa.shape
    sub_tiles = tq // tk
    assert sub_tiles >= 2 and sub_tiles * tk == tq
    blk = lambda bi, h, i: (bi, i, h)
    const = lambda bi, h, i: (0, 0)
    return pl.pallas_call(
        functools.partial(_diff_attn_kernel, tq=tq, tk=tk, lambda_init=lambda_init),
        scratch_shapes=[pltpu.VMEM((sub_tiles, tk, 2 * tq), F32),
                        pltpu.VMEM((1, 2 * tq), F32),
                        pltpu.VMEM((2, DA_V_DIM + ONES_ROWS, tq), F32)],
        grid=(b, DA_HEADS, s // tq),
        in_specs=[
            pl.BlockSpec(da_lambda.shape, const),
            pl.BlockSpec((None, tq, DA_V_DIM), blk),
            pl.BlockSpec((None, s, DA_V_DIM), lambda bi, h, i: (bi, 0, h)),
            pl.BlockSpec((DA_V_DIM, s), lambda bi, h, i: (h, bi)),
            pl.BlockSpec((None, tq, DA_V_DIM), blk),
            pl.BlockSpec((1, DA_V_DIM), const),
        ],
        out_specs=pl.BlockSpec((None, tq, DA_V_DIM), blk),
        out_shape=jax.ShapeDtypeStruct((b, s, DA_HEADS * DA_V_DIM), BF16),
        compiler_params=pltpu.CompilerParams(
            dimension_semantics=("arbitrary", "arbitrary", "arbitrary"),
            vmem_limit_bytes=VMEM_LIMIT),
        name="diff_attn",
    )(da_lambda, qa, ka, vat, za, out_gain)


WIDE = GDN_HEADS * CHUNK
HALO = 16
PQ_ROWS = GDN_K_DIM + CHUNK
NX_ROWS = GDN_K_DIM + 8


def _split2(x):
    hi = x.astype(BF16)
    lo = (x - hi.astype(F32)).astype(BF16)
    return hi, lo


def _block_diag(y, mask):
    return jnp.where(mask, jnp.concatenate([y] * GDN_HEADS, axis=0), jnp.zeros((), y.dtype))


def _wide_mm(x, y, bd_mask):
    return _dot(x.astype(BF16), _block_diag(y.astype(BF16), bd_mask))


def _wide_unit_lower_inverse(a_list, eye, blk16, mm):
    d = [jnp.where(blk16, a, 0.0) for a in a_list]
    off = [a - di for a, di in zip(a_list, d)]
    p = [-di for di in d]
    t = [eye + pi for pi in p]
    for _ in range(3):
        p = [mm(pi, pi) for pi in p]
        t = [ti + mm(ti, pi) for ti, pi in zip(t, p)]
    e = [mm(ti, oi) for ti, oi in zip(t, off)]
    e2 = [mm(ei, ei) for ei in e]
    f = [eye - ei + e2i - mm(ei, e2i) for ei, e2i in zip(e, e2)]
    return [mm(fi, ti) for fi, ti in zip(f, t)]


def _head_bcast(col4, width):
    rows = col4.shape[0]
    return jnp.concatenate([jnp.broadcast_to(col4[:, h:h + 1], (rows, width))
                            for h in range(GDN_HEADS)], axis=1)


def _gdn_prep_kernel(q_ref, k_ref, v_ref, qh_ref, kh_ref, vh_ref, ab_ref, cw_ref, alog_ref, dtb_ref,
                     pq_ref, nx_ref, o0_ref):
    tr = q_ref.shape[0]
    chunks = tr // CHUNK

    cur = jnp.concatenate([q_ref[...], k_ref[...], v_ref[...]], axis=-1).astype(F32)
    halo = jnp.concatenate([qh_ref[...], kh_ref[...], vh_ref[...]], axis=-1).astype(F32)
    halo = jnp.where(pl.program_id(1) == 0, 0.0, halo)
    ext = jnp.concatenate([halo, cur], axis=0)
    conv = jnp.zeros_like(cur)
    for j in range(CONV_WIDTH):
        lo = HALO - (CONV_WIDTH - 1) + j
        conv = conv + ext[lo:lo + tr, :] * cw_ref[j:j + 1, :]
    qkv = _silu(conv)
    q = qkv[:, :BRANCH]
    k = qkv[:, BRANCH:2 * BRANCH]
    v = qkv[:, 2 * BRANCH:]

    r512 = lax.broadcasted_iota(jnp.int32, (BRANCH, BRANCH), 0) // GDN_K_DIM
    c512 = lax.broadcasted_iota(jnp.int32, (BRANCH, BRANCH), 1) // GDN_K_DIM
    head_ones = jnp.where(r512 == c512, 1.0, 0.0).astype(BF16)
    q = q * lax.rsqrt(_dot((q * q).astype(BF16), head_ones) + EPS) * (GDN_K_DIM ** -0.5)
    k = k * lax.rsqrt(_dot((k * k).astype(BF16), head_ones) + EPS)

    ab = ab_ref[...]
    xg = ab + dtb_ref[...]
    softplus = jnp.maximum(xg, 0.0) + jnp.log1p(jnp.exp(-jnp.abs(xg)))
    g_all = -jnp.exp(alog_ref[...]) * softplus
    beta_all = _sigmoid(ab)

    row_w = lax.broadcasted_iota(jnp.int32, (CHUNK, WIDE), 0)
    col_w = lax.broadcasted_iota(jnp.int32, (CHUNK, WIDE), 1) % CHUNK
    tri_incl = col_w <= row_w
    tri_strict = col_w < row_w
    eye_w = jnp.where(col_w == row_w, 1.0, 0.0).astype(F32)
    blk16 = (row_w // 16) == (col_w // 16)
    bd_mask = (lax.broadcasted_iota(jnp.int32, (WIDE, WIDE), 0) // CHUNK
               == lax.broadcasted_iota(jnp.int32, (WIDE, WIDE), 1) // CHUNK)
    kbd_mask = (lax.broadcasted_iota(jnp.int32, (WIDE, BRANCH), 0) // CHUNK
                == lax.broadcasted_iota(jnp.int32, (WIDE, BRANCH), 1) // GDN_K_DIM)
    r64 = lax.broadcasted_iota(jnp.int32, (CHUNK, CHUNK), 0)
    c64 = lax.broadcasted_iota(jnp.int32, (CHUNK, CHUNK), 1)
    lower_ones = jnp.where(c64 <= r64, 1.0, 0.0).astype(BF16)
    mm = functools.partial(_wide_mm, bd_mask=bd_mask)

    staged = []
    for c in range(chunks):
        rows = slice(c * CHUNK, (c + 1) * CHUNK)
        qc, kc, vc = q[rows], k[rows], v[rows]
        g_c = g_all[rows]
        g_hi, g_lo = _split2(g_c)
        gc2 = _dot(lower_ones, jnp.concatenate([g_hi, g_lo], axis=1))
        gc_all = gc2[:, :LANES] + gc2[:, LANES:]
        gcb = _head_bcast(gc_all, GDN_K_DIM)
        betab = _head_bcast(beta_all[rows, GDN_HEADS:], GDN_K_DIM)
        exp_gc = jnp.exp(gcb)
        gl_row = exp_gc[CHUNK - 1:CHUNK]
        k_beta = kc * betab
        v_beta = vc * betab
        kbe = k_beta * exp_gc
        q_dec = qc * exp_gc
        k_tail = kc * jnp.exp(gcb[CHUNK - 1:CHUNK] - gcb)

        r_w = jnp.where(tri_strict, _head_bcast(g_c, CHUNK), 0.0)
        r_hi, r_lo = _split2(r_w)
        d2 = _dot(lower_ones, jnp.concatenate([r_hi, r_lo], axis=1))
        gamma = jnp.exp(jnp.where(tri_incl, d2[:, :WIDE] + d2[:, WIDE:], NEG_BIG))

        k16 = kc.astype(BF16)
        kbd = jnp.where(kbd_mask, jnp.concatenate([k16] * GDN_HEADS, axis=0), jnp.zeros((), BF16))
        sc = _dot_nt(jnp.concatenate([k_beta.astype(BF16), qc.astype(BF16)], axis=0), kbd)
        a_strict = jnp.where(tri_strict, sc[:CHUNK] * gamma, 0.0)
        a_intra = jnp.where(tri_incl, sc[CHUNK:] * gamma, 0.0)
        staged.append((a_strict, a_intra, v_beta, kbe, q_dec, k_tail, gl_row))

    t_inv = _wide_unit_lower_inverse([st[0] for st in staged], eye_w, blk16, mm)

    for c in range(chunks):
        rows = slice(c * CHUNK, (c + 1) * CHUNK)
        _, a_intra, v_beta, kbe, q_dec, k_tail, gl_row = staged[c]
        t_off = t_inv[c] - eye_w
        for h in range(GDN_HEADS):
            hs = slice(h * GDN_K_DIM, (h + 1) * GDN_K_DIM)
            ws = slice(h * CHUNK, (h + 1) * CHUNK)
            rhs = jnp.concatenate([v_beta[:, hs], kbe[:, hs]], axis=1)
            uw = rhs + _dot(t_off[:, ws].astype(BF16), rhs.astype(BF16))
            uw16 = uw.astype(BF16)
            ai_uw = _dot(a_intra[:, ws].astype(BF16), uw16)
            kt_uw = _dot_tn(k_tail[:, hs].astype(BF16), uw16)
            pq_ref[c, :GDN_K_DIM, hs] = kt_uw[:, GDN_V_DIM:].astype(pq_ref.dtype)
            pq_ref[c, GDN_K_DIM:, hs] = (q_dec[:, hs] - ai_uw[:, GDN_V_DIM:]).astype(pq_ref.dtype)
            nx_ref[c, :GDN_K_DIM, hs] = kt_uw[:, :GDN_V_DIM]
            o0_ref[rows, hs] = ai_uw[:, :GDN_V_DIM]
        nx_ref[c, GDN_K_DIM:, :] = jnp.broadcast_to(gl_row, (NX_ROWS - GDN_K_DIM, BRANCH))


def _gdn_prep(qb, kb, vb, ab, conv_w, alog_pad, dtb_pad, tr):
    b, s, _ = qb.shape
    chunks = tr // CHUNK
    blk = lambda bi, i: (bi, i, 0)
    halo = lambda bi, i: (bi, jnp.maximum(i * (tr // HALO) - 1, 0), 0)
    const = lambda bi, i: (0, 0)
    per_chunk = lambda bi, i: (bi, i, 0, 0)
    return pl.pallas_call(
        _gdn_prep_kernel,
        grid=(b, s // tr),
        in_specs=[
            pl.BlockSpec((None, tr, BRANCH), blk),
            pl.BlockSpec((None, tr, BRANCH), blk),
            pl.BlockSpec((None, tr, BRANCH), blk),
            pl.BlockSpec((None, HALO, BRANCH), halo),
            pl.BlockSpec((None, HALO, BRANCH), halo),
            pl.BlockSpec((None, HALO, BRANCH), halo),
            pl.BlockSpec((None, tr, LANES), blk),
            pl.BlockSpec(conv_w.shape, const),
            pl.BlockSpec((1, LANES), const),
            pl.BlockSpec((1, LANES), const),
        ],
        out_specs=[
            pl.BlockSpec((None, chunks, PQ_ROWS, BRANCH), per_chunk),
            pl.BlockSpec((None, chunks, NX_ROWS, BRANCH), per_chunk),
            pl.BlockSpec((None, tr, BRANCH), blk),
        ],
        out_shape=[
            jax.ShapeDtypeStruct((b, s // CHUNK, PQ_ROWS, BRANCH), BF16),
            jax.ShapeDtypeStruct((b, s // CHUNK, NX_ROWS, BRANCH), F32),
            jax.ShapeDtypeStruct((b, s, BRANCH), F32),
        ],
        compiler_params=pltpu.CompilerParams(
            dimension_semantics=("arbitrary", "arbitrary"), vmem_limit_bytes=VMEM_LIMIT),
        name="gdn_prep",
    )(qb, kb, vb, qb, kb, vb, ab, conv_w, alog_pad, dtb_pad)


def _gdn_scan_kernel(pq_ref, nx_ref, o0_ref, z_ref, nw_ref, o_ref, state_ref):
    batch, chunks = pq_ref.shape[0], pq_ref.shape[1]

    @pl.when(pl.program_id(0) == 0)
    def _():
        state_ref[...] = jnp.zeros_like(state_ref)

    for c in range(chunks):
        rows = slice(c * CHUNK, (c + 1) * CHUNK)
        for bi in range(batch):
            for h in range(GDN_HEADS):
                hs = slice(h * GDN_K_DIM, (h + 1) * GDN_K_DIM)
                state = state_ref[bi, h]
                r = _dot(pq_ref[bi, c, :, hs], state.astype(BF16))
                gl = nx_ref[bi, c, GDN_K_DIM:GDN_K_DIM + 1, hs]
                state_ref[bi, h] = state * gl - r[:GDN_K_DIM] + nx_ref[bi, c, :GDN_K_DIM, hs]
                o = r[GDN_K_DIM:] + o0_ref[bi, rows, hs]
                ms = jnp.mean(o * o, axis=-1, keepdims=True)
                y = o * lax.rsqrt(ms + EPS) * nw_ref[...]
                o_ref[bi, rows, hs] = (y * z_ref[bi, rows, hs].astype(F32)).astype(o_ref.dtype)


def _gdn_scan(pq, nx, o0, zb, norm_w, chunks):
    b, n_chunks = pq.shape[0], pq.shape[1]
    tr = chunks * CHUNK
    per_chunk = lambda i: (0, i, 0, 0)
    blk = lambda i: (0, i, 0)
    return pl.pallas_call(
        _gdn_scan_kernel,
        grid=(n_chunks // chunks,),
        in_specs=[
            pl.BlockSpec((b, chunks, PQ_ROWS, BRANCH), per_chunk),
            pl.BlockSpec((b, chunks, NX_ROWS, BRANCH), per_chunk),
            pl.BlockSpec((b, tr, BRANCH), blk),
            pl.BlockSpec((b, tr, BRANCH), blk),
            pl.BlockSpec((1, GDN_V_DIM), lambda i: (0, 0)),
        ],
        out_specs=pl.BlockSpec((b, tr, BRANCH), blk),
        out_shape=jax.ShapeDtypeStruct((b, n_chunks * CHUNK, BRANCH), BF16),
        scratch_shapes=[pltpu.VMEM((b, GDN_HEADS, GDN_K_DIM, GDN_V_DIM), F32)],
        compiler_params=pltpu.CompilerParams(
            dimension_semantics=("arbitrary",), vmem_limit_bytes=VMEM_LIMIT),
        name="gdn_scan",
    )(pq, nx, o0, zb, norm_w)


def _out_proj_kernel(x_ref, ya_ref, yb_ref, gate_ref, wa_ref, wb_ref, wo_ref, o_ref):
    d = x_ref.shape[-1]
    up_a = _dot(ya_ref[...], wa_ref[...])
    up_b = _dot(yb_ref[...], wb_ref[...])
    merged = (gate_ref[:, :d].astype(F32) * up_a + gate_ref[:, d:].astype(F32) * up_b)
    o_ref[...] = x_ref[...] + _dot(merged.astype(BF16), wo_ref[...])


def _out_proj(x2d, ya, yb, gates, w_up_a, w_up_b, w_out, tm):
    n, d = x2d.shape
    const = lambda i: (0, 0)
    row = lambda i: (i, 0)
    return pl.pallas_call(
        _out_proj_kernel,
        grid=(n // tm,),
        in_specs=[
            pl.BlockSpec((tm, d), row),
            pl.BlockSpec((tm, BRANCH), row),
            pl.BlockSpec((tm, BRANCH), row),
            pl.BlockSpec((tm, 2 * d), row),
            pl.BlockSpec(w_up_a.shape, const),
            pl.BlockSpec(w_up_b.shape, const),
            pl.BlockSpec(w_out.shape, const),
        ],
        out_specs=pl.BlockSpec((tm, d), row),
        out_shape=jax.ShapeDtypeStruct((n, d), x2d.dtype),
        compiler_params=pltpu.CompilerParams(
            dimension_semantics=("arbitrary",), vmem_limit_bytes=VMEM_LIMIT),
        name="out_proj",
    )(x2d, ya, yb, gates, w_up_a, w_up_b, w_out)


def _lambda_init_for(layer):
    return 0.8 - 0.6 * math.exp(-0.3 * layer)


def kernel(x, norm_w, w_in, da_q_gain, da_k_gain, da_lambda, da_out_gain, gdn_conv_w, gdn_a_log,
           gdn_dt_bias, gdn_norm_w, w_up_da, w_up_gdn, w_out):
    b, s, d = x.shape
    depth = norm_w.shape[0]
    n_main = 8 * BRANCH
    row_tile = 512
    attn_q_tile = 512
    attn_k_tile = 256
    gdn_prep_tile = 256
    gdn_scan_chunks = 4
    group =jnp.arange(BRANCH) // DA_HEAD_DIM
    gsum = (group[:, None] == group[None, :]).astype(BF16)
    groups_per_seg = BRANCH // DA_HEAD_DIM
    q_scale = DA_HEAD_DIM ** -0.5 * math.log2(math.e)

    x2d = x.reshape(b * s, d)
    for l in range(depth):
        w = w_in[l]
        w_main = jnp.concatenate([w[:, :2 * BRANCH], w[:, 3 * BRANCH:n_main],
                                  w[:, n_main + 2 * GDN_HEADS:]], axis=1).astype(BF16)
        w_vt = w[:, 2 * BRANCH:3 * BRANCH].T.astype(BF16)
        w_ab = jnp.pad(w[:, n_main:n_main + 2 * GDN_HEADS],
                       ((0, 0), (0, LANES - 2 * GDN_HEADS))).astype(BF16)
        q_gain = jnp.tile(da_q_gain[l] * q_scale, groups_per_seg)[None, :]
        k_gain = jnp.tile(da_k_gain[l], groups_per_seg)[None, :]
        qa, ka, vat, za, qb, kb, vb, zb, ab, gates = _in_proj(
            x2d, norm_w[l][None, :], w_main, w_vt, w_ab, q_gain, k_gain, gsum, row_tile)

        r3 = lambda t: t.reshape(b, s, t.shape[-1])
        ya = _diff_attn(r3(qa), r3(ka), vat, r3(za), da_lambda[l], da_out_gain[l][None, :],
                        _lambda_init_for(l), attn_q_tile, attn_k_tile)

        alog_pad = jnp.pad(gdn_a_log[l], (0, LANES - GDN_HEADS))[None, :]
        dtb_pad = jnp.pad(gdn_dt_bias[l], (0, LANES - GDN_HEADS))[None, :]
        pq, nx, o0 = _gdn_prep(r3(qb), r3(kb), r3(vb), r3(ab), gdn_conv_w[l], alog_pad, dtb_pad,
                               gdn_prep_tile)
        yb = _gdn_scan(pq, nx, o0, r3(zb), gdn_norm_w[l][None, :], gdn_scan_chunks)

        x2d = _out_proj(x2d, ya.reshape(b * s, -1), yb.reshape(b * s, -1), gates,
                        w_up_da[l].astype(BF16), w_up_gdn[l].astype(BF16), w_out[l].astype(BF16),
                        row_tile)
    return x2d.reshape(b, s, d)
```

```python
import functools
import math

import jax
import jax.numpy as jnp
from jax import lax
from jax.experimental import pallas as pl
from jax.experimental.pallas import tpu as pltpu

F32 = jnp.float32
BF16 = jnp.bfloat16

CHUNK = 64
EPS = 1e-6
DA_HEADS = 4
DA_HEAD_DIM = 64
DA_V_DIM = 2 * DA_HEAD_DIM
GDN_HEADS = 4
GDN_K_DIM = 128
GDN_V_DIM = 128
CONV_WIDTH = 4
BRANCH = 512
LANES = 128
NEG_BIG = -1e30
ONES_ROWS = 16
UNROLL_BLOCKS = 2
VMEM_LIMIT = 56 * 1024 * 1024

HIGHEST = lax.Precision.HIGHEST


def _dot(a, b, precision=None):
    return jnp.dot(a, b, preferred_element_type=F32, precision=precision)


def _dot_nt(a, b, precision=None):
    return lax.dot_general(a, b, (((1,), (1,)), ((), ())),
                           preferred_element_type=F32, precision=precision)


def _dot_tn(a, b, precision=None):
    return lax.dot_general(a, b, (((0,), (0,)), ((), ())),
                           preferred_element_type=F32, precision=precision)


def _sigmoid(x):
    return 1.0 / (1.0 + jnp.exp(-x))


def _silu(x):
    return x * _sigmoid(x)


def _in_proj_kernel(x_ref, nw_ref, w_ref, wvt_ref, wab_ref, qg_ref, kg_ref, gsum_ref,
                    qa_ref, ka_ref, vat_ref, za_ref, qb_ref, kb_ref, vb_ref, zb_ref,
                    ab_ref, gate_ref):
    x = x_ref[...]
    ms = jnp.mean(x * x, axis=-1, keepdims=True)
    h = (x * lax.rsqrt(ms + EPS) * nw_ref[...]).astype(BF16)

    def proj(j, n=1):
        return _dot(h, w_ref[:, j * BRANCH:(j + n) * BRANCH])

    def qk_norm(t, gain):
        ssq = _dot((t * t).astype(BF16), gsum_ref[...])
        return (t * lax.rsqrt(ssq * (1.0 / DA_HEAD_DIM) + EPS) * gain).astype(BF16)

    qa_ref[...] = qk_norm(proj(0), qg_ref[...])
    ka_ref[...] = qk_norm(proj(1), kg_ref[...])
    vat_ref[...] = _dot_nt(wvt_ref[...], h).astype(BF16)
    za_ref[...] = _silu(proj(2)).astype(BF16)
    qb_ref[...] = proj(3).astype(BF16)
    kb_ref[...] = proj(4).astype(BF16)
    vb_ref[...] = proj(5).astype(BF16)
    zb_ref[...] = _silu(proj(6)).astype(BF16)
    gate_ref[...] = _sigmoid(proj(7, 4)).astype(BF16)
    ab_ref[...] = _dot(h, wab_ref[...])


def _in_proj(x2d, norm_w, w_main, w_vt, w_ab, q_gain, k_gain, gsum, tm):
    n, d = x2d.shape
    const = lambda i: (0, 0)
    row = lambda i: (i, 0)
    seg = jax.ShapeDtypeStruct((n, BRANCH), BF16)
    seg_spec = pl.BlockSpec((tm, BRANCH), row)
    out_shape = [seg, seg, jax.ShapeDtypeStruct((BRANCH, n), BF16)] + [seg] * 5 + [
        jax.ShapeDtypeStruct((n, LANES), F32), jax.ShapeDtypeStruct((n, 2 * d), BF16)]
    out_specs = ([seg_spec, seg_spec, pl.BlockSpec((BRANCH, tm), lambda i: (0, i))]
                 + [seg_spec] * 5
                 + [pl.BlockSpec((tm, LANES), row), pl.BlockSpec((tm, 2 * d), row)])
    return pl.pallas_call(
        _in_proj_kernel,
        grid=(n // tm,),
        in_specs=[
            pl.BlockSpec((tm, d), row),
            pl.BlockSpec((1, d), const),
            pl.BlockSpec(w_main.shape, const, pipeline_mode=pl.Buffered(1)),
            pl.BlockSpec(w_vt.shape, const, pipeline_mode=pl.Buffered(1)),
            pl.BlockSpec(w_ab.shape, const, pipeline_mode=pl.Buffered(1)),
            pl.BlockSpec((1, BRANCH), const),
            pl.BlockSpec((1, BRANCH), const),
            pl.BlockSpec((BRANCH, BRANCH), const, pipeline_mode=pl.Buffered(1)),
        ],
        out_specs=out_specs,
        out_shape=out_shape,
        compiler_params=pltpu.CompilerParams(
            dimension_semantics=("arbitrary",), vmem_limit_bytes=VMEM_LIMIT),
        name="in_proj",
    )(x2d, norm_w, w_main, w_vt, w_ab, q_gain, k_gain, gsum)


def _diff_attn_kernel(lam_ref, q_ref, k_ref, vt_ref, z_ref, og_ref, o_ref, s_scr, m_scr, acc_scr,
                      *, tq, tk, lambda_init):
    i = pl.program_id(2)
    q = q_ref[...]
    lane = lax.broadcasted_iota(jnp.int32, q.shape, 1)
    zero = jnp.zeros_like(q)
    qq = jnp.concatenate([jnp.where(lane < DA_HEAD_DIM, q, zero),
                          jnp.where(lane >= DA_HEAD_DIM, q, zero)], axis=0)
    sub_tiles = s_scr.shape[0]
    ones_rows = jnp.ones((ONES_ROWS, tk), BF16)
    m_scr[...] = jnp.full(m_scr.shape, NEG_BIG, F32)
    acc_scr[...] = jnp.zeros(acc_scr.shape, F32)

    def issue(start, slot, q0=0):
        wq = tq - q0
        qsel = qq if q0 == 0 else jnp.concatenate([qq[q0:tq], qq[tq + q0:]], axis=0)
        s = _dot_nt(k_ref[pl.ds(start, tk), :], qsel)
        s_scr[slot, :, :2 * wq] = s
        return jnp.max(s, axis=0, keepdims=True)

    def consume(start, slot, tile_max, masked, q0=0):
        wq = tq - q0
        s = s_scr[slot, :, :2 * wq]
        if masked:
            k_chunk = (lax.broadcasted_iota(jnp.int32, s.shape, 0) + slot * tk) // CHUNK
            q_chunk = (lax.broadcasted_iota(jnp.int32, s.shape, 1) % wq + q0) // CHUNK
            valid = k_chunk <= q_chunk
            s = jnp.where(valid, s, NEG_BIG)
            tile_max = jnp.max(s, axis=0, keepdims=True)
        m_cols = [slice(g * tq + q0, (g + 1) * tq) for g in range(2)]
        m = jnp.concatenate([m_scr[:, mc] for mc in m_cols], axis=1)
        m_new = jnp.maximum(m, tile_max)
        alpha = jnp.exp2(m - m_new)
        p = jnp.exp2(s - m_new)
        if masked:
            p = jnp.where(valid, p, 0.0)
        pb = p.astype(BF16)
        vt = jnp.concatenate([vt_ref[:, pl.ds(start, tk)], ones_rows], axis=0)
        for g in range(2):
            cols = slice(g * wq, (g + 1) * wq)
            m_scr[:, m_cols[g]] = m_new[:, cols]
            acc_scr[g, :, q0:] = alpha[:, cols] * acc_scr[g, :, q0:] + _dot(vt, pb[:, cols])

    def make_body(blocks):
        def body(jj, tile_max):
            for u in range(blocks):
                for t in range(sub_tiles):
                    start = pl.multiple_of((jj * blocks + u) * tq + t * tk, tk)
                    next_max = issue(start + tk, (t + 1) % sub_tiles)
                    consume(start, t, tile_max, masked=False)
                    tile_max = next_max
            return tile_max
        return body

    paired = i // UNROLL_BLOCKS
    tile_max = lax.fori_loop(0, paired, make_body(UNROLL_BLOCKS), issue(0, 0))
    tile_max = lax.fori_loop(paired * UNROLL_BLOCKS, i, make_body(1), tile_max)
    for t in range(sub_tiles):
        start = pl.multiple_of(i * tq + t * tk, tk)
        if t + 1 < sub_tiles:
            issue(start + tk, t + 1, q0=(t + 1) * tk)
        consume(start, t, None, masked=True, q0=t * tk)
    acc1 = acc_scr[0]
    acc2 = acc_scr[1]

    lam_p = lam_ref[...]
    lam = (jnp.exp(jnp.sum(lam_p[0:1] * lam_p[1:2], axis=-1, keepdims=True))
           - jnp.exp(jnp.sum(lam_p[2:3] * lam_p[3:4], axis=-1, keepdims=True))
           + lambda_init)
    ot = (acc1[:DA_V_DIM] / acc1[DA_V_DIM:DA_V_DIM + 1]
          - lam * (acc2[:DA_V_DIM] / acc2[DA_V_DIM:DA_V_DIM + 1]))
    ms = jnp.mean(ot * ot, axis=0, keepdims=True)
    y = (ot * lax.rsqrt(ms + EPS)).T * og_ref[...] * (1.0 - lambda_init)
    o_ref[...] = (y * z_ref[...].astype(F32)).astype(o_ref.dtype)


def _diff_attn(qa, ka, vat, za, da_lambda, out_gain, lambda_init, tq, tk):
    b, s, _ = qa.shape
    sub_tiles = tq // tk
    assert sub_tiles >= 2 and sub_tiles * tk == tq
    blk = lambda bi, h, i: (bi, i, h)
    const = lambda bi, h, i: (0, 0)
    return pl.pallas_call(
        functools.partial(_diff_attn_kernel, tq=tq, tk=tk, lambda_init=lambda_init),
        scratch_shapes=[pltpu.VMEM((sub_tiles, tk, 2 * tq), F32),
                        pltpu.VMEM((1, 2 * tq), F32),
                        pltpu.VMEM((2, DA_V_DIM + ONES_ROWS, tq), F32)],
        grid=(b, DA_HEADS, s // tq),
        in_specs=[
            pl.BlockSpec(da_lambda.shape, const),
            pl.BlockSpec((None, tq, DA_V_DIM), blk),
            pl.BlockSpec((None, s, DA_V_DIM), lambda bi, h, i: (bi, 0, h)),
            pl.BlockSpec((DA_V_DIM, s), lambda bi, h, i: (h, bi)),
            pl.BlockSpec((None, tq, DA_V_DIM), blk),
            pl.BlockSpec((1, DA_V_DIM), const),
        ],
        out_specs=pl.BlockSpec((None, tq, DA_V_DIM), blk),
        out_shape=jax.ShapeDtypeStruct((b, s, DA_HEADS * DA_V_DIM), BF16),
        compiler_params=pltpu.CompilerParams(
            dimension_semantics=("arbitrary", "arbitrary", "arbitrary"),
            vmem_limit_bytes=VMEM_LIMIT),
        name="diff_attn",
    )(da_lambda, qa, ka, vat, za, out_gain)


WIDE = GDN_HEADS * CHUNK
HALO = 16
PQ_ROWS = GDN_K_DIM + CHUNK
NX_ROWS = GDN_K_DIM + 8


def _split2(x):
    hi = x.astype(BF16)
    lo = (x - hi.astype(F32)).astype(BF16)
    return hi, lo


def _block_diag(y, mask01):
    return jnp.concatenate([y] * GDN_HEADS, axis=0) * mask01


def _wide_mm(x, y, bd_mask):
    return _dot(x.astype(BF16), _block_diag(y.astype(BF16), bd_mask))


def _wide_unit_lower_inverse(a_list, eye, blk16, mm):
    d = [jnp.where(blk16, a, 0.0) for a in a_list]
    off = [a - di for a, di in zip(a_list, d)]
    p = [-di for di in d]
    t = [eye + pi for pi in p]
    for _ in range(3):
        p = [mm(pi, pi) for pi in p]
        yield
        t = [ti + mm(ti, pi) for ti, pi in zip(t, p)]
        yield
    e = [mm(ti, oi) for ti, oi in zip(t, off)]
    yield
    e2 = [mm(ei, ei) for ei in e]
    yield
    f = [eye - ei + e2i - mm(ei, e2i) for ei, e2i in zip(e, e2)]
    yield
    return [mm(fi, ti) for fi, ti in zip(f, t)]


def _run_interleaved(*gens):
    results = [None] * len(gens)
    alive = [g is not None for g in gens]
    while any(alive):
        for n, gen in enumerate(gens):
            if alive[n]:
                try:
                    next(gen)
                except StopIteration as stop:
                    results[n] = stop.value
                    alive[n] = False
    return results


def _head_bcast(col4, width):
    rows = col4.shape[0]
    return jnp.concatenate([jnp.broadcast_to(col4[:, h:h + 1], (rows, width))
                            for h in range(GDN_HEADS)], axis=1)


def _gdn_prep_kernel(q_ref, k_ref, v_ref, qh_ref, kh_ref, vh_ref, ab_ref, cw_ref, alog_ref, dtb_ref,
                     shift_ref, hsel_ref, hones_ref, pq_ref, nx_ref, o0_ref):
    tr = q_ref.shape[0]
    grp = shift_ref.shape[1]
    chunks = grp // CHUNK
    head_ones = hones_ref[...]

    row_w = lax.broadcasted_iota(jnp.int32, (CHUNK, WIDE), 0)
    col_w = lax.broadcasted_iota(jnp.int32, (CHUNK, WIDE), 1) % CHUNK
    tri_incl = col_w <= row_w
    tri_strict = col_w < row_w
    eye_w = jnp.where(col_w == row_w, 1.0, 0.0).astype(F32)
    blk16 = (row_w // 16) == (col_w // 16)
    bd_mask = jnp.where(lax.broadcasted_iota(jnp.int32, (WIDE, WIDE), 0) // CHUNK
                        == lax.broadcasted_iota(jnp.int32, (WIDE, WIDE), 1) // CHUNK,
                        1.0, 0.0).astype(BF16)
    kbd_mask = jnp.where(lax.broadcasted_iota(jnp.int32, (WIDE, BRANCH), 0) // CHUNK
                         == lax.broadcasted_iota(jnp.int32, (WIDE, BRANCH), 1) // GDN_K_DIM,
                         1.0, 0.0).astype(BF16)
    r64 = lax.broadcasted_iota(jnp.int32, (CHUNK, CHUNK), 0)
    c64 = lax.broadcasted_iota(jnp.int32, (CHUNK, CHUNK), 1)
    lower_ones = jnp.where(c64 <= r64, 1.0, 0.0).astype(BF16)
    mm = functools.partial(_wide_mm, bd_mask=bd_mask)

    def conv_silu(x_ref, x_halo_ref, seg, gr, first_group):
        cols = slice(seg * BRANCH, (seg + 1) * BRANCH)
        cur = x_ref[gr, :]
        if first_group:
            halo = x_halo_ref[...]
            halo = jnp.where(pl.program_id(1) == 0, jnp.zeros((), halo.dtype), halo)
        else:
            halo = x_ref[gr.start - HALO:gr.start, :]
        conv = cur.astype(F32) * cw_ref[CONV_WIDTH - 1:CONV_WIDTH, cols]
        for s in range(1, CONV_WIDTH):
            shifted = _dot(shift_ref[s - 1], cur)
            head = shifted[:HALO] + _dot(hsel_ref[s - 1], halo)
            shifted = jnp.concatenate([head, shifted[HALO:]], axis=0)
            conv = conv + shifted * cw_ref[CONV_WIDTH - 1 - s:CONV_WIDTH - s, cols]
        return _silu(conv)

    def front(g):
        gr = slice(g * grp, (g + 1) * grp)
        q = conv_silu(q_ref, qh_ref, 0, gr, g == 0)
        q = q * lax.rsqrt(_dot((q * q).astype(BF16), head_ones) + EPS) * (GDN_K_DIM ** -0.5)
        yield
        k = conv_silu(k_ref, kh_ref, 1, gr, g == 0)
        k = k * lax.rsqrt(_dot((k * k).astype(BF16), head_ones) + EPS)
        yield
        v = conv_silu(v_ref, vh_ref, 2, gr, g == 0)
        yield

        ab = ab_ref[gr, :]
        xg = ab + dtb_ref[...]
        softplus = jnp.maximum(xg, 0.0) + jnp.log1p(jnp.exp(-jnp.abs(xg)))
        g_all = -jnp.exp(alog_ref[...]) * softplus
        beta_all = _sigmoid(ab)

        staged = []
        for c in range(chunks):
            rows = slice(c * CHUNK, (c + 1) * CHUNK)
            qc, kc, vc = q[rows], k[rows], v[rows]
            g_c = g_all[rows]
            g_hi, g_lo = _split2(g_c)
            gc2 = _dot(lower_ones, jnp.concatenate([g_hi, g_lo], axis=1))
            gc_all = gc2[:, :LANES] + gc2[:, LANES:]
            gcb = _head_bcast(gc_all, GDN_K_DIM)
            betab = _head_bcast(beta_all[rows, GDN_HEADS:], GDN_K_DIM)
            exp_gc = jnp.exp(gcb)
            gl_row = exp_gc[CHUNK - 1:CHUNK]
            k_beta = kc * betab
            v_beta = vc * betab
            kbe = k_beta * exp_gc
            q_dec = qc * exp_gc
            k_tail = kc * jnp.exp(gcb[CHUNK - 1:CHUNK] - gcb)

            r_w = jnp.where(tri_strict, _head_bcast(g_c, CHUNK), 0.0)
            r_hi, r_lo = _split2(r_w)
            d2 = _dot(lower_ones, jnp.concatenate([r_hi, r_lo], axis=1))
            gamma = jnp.exp(jnp.where(tri_incl, d2[:, :WIDE] + d2[:, WIDE:], NEG_BIG))

            k16 = kc.astype(BF16)
            kbd = jnp.concatenate([k16] * GDN_HEADS, axis=0) * kbd_mask
            sc = _dot_nt(jnp.concatenate([k_beta.astype(BF16), qc.astype(BF16)], axis=0), kbd)
            a_strict = jnp.where(tri_strict, sc[:CHUNK] * gamma, 0.0)
            a_intra = jnp.where(tri_incl, sc[CHUNK:] * gamma, 0.0)
            staged.append((a_strict, a_intra, v_beta, kbe, q_dec, k_tail, gl_row))
            yield
        return staged

    def back(g, staged):
        t_inv = yield from _wide_unit_lower_inverse([st[0] for st in staged], eye_w, blk16, mm)
        pairs = [(c, h) for c in range(chunks) for h in range(GDN_HEADS)]
        hsl = lambda h: slice(h * GDN_K_DIM, (h + 1) * GDN_K_DIM)
        wsl = lambda h: slice(h * CHUNK, (h + 1) * CHUNK)
        uw16 = {}
        for c, h in pairs:
            _, _, v_beta, kbe, _, _, _ = staged[c]
            t_off = t_inv[c] - eye_w
            rhs = jnp.concatenate([v_beta[:, hsl(h)], kbe[:, hsl(h)]], axis=1)
            uw = rhs + _dot(t_off[:, wsl(h)].astype(BF16), rhs.astype(BF16))
            uw16[c, h] = uw.astype(BF16)
        yield
        for c, h in pairs:
            oc = g * chunks + c
            a_intra, q_dec = staged[c][1], staged[c][4]
            ai_uw = _dot(a_intra[:, wsl(h)].astype(BF16), uw16[c, h])
            pq_ref[oc, GDN_K_DIM:, hsl(h)] = (q_dec[:, hsl(h)] - ai_uw[:, GDN_V_DIM:]).astype(pq_ref.dtype)
            o0_ref[oc * CHUNK:(oc + 1) * CHUNK, hsl(h)] = ai_uw[:, :GDN_V_DIM]
        yield
        for c, h in pairs:
            oc = g * chunks + c
            k_tail = staged[c][5]
            kt_uw = _dot_tn(k_tail[:, hsl(h)].astype(BF16), uw16[c, h])
            pq_ref[oc, :GDN_K_DIM, hsl(h)] = kt_uw[:, GDN_V_DIM:].astype(pq_ref.dtype)
            nx_ref[oc, :GDN_K_DIM, hsl(h)] = kt_uw[:, :GDN_V_DIM]
        for c in range(chunks):
            oc = g * chunks + c
            nx_ref[oc, GDN_K_DIM:, :] = jnp.broadcast_to(staged[c][6], (NX_ROWS - GDN_K_DIM, BRANCH))
        yield

    groups = tr // grp
    staged, = _run_interleaved(front(0))
    for g in range(groups):
        nxt = front(g + 1) if g + 1 < groups else None
        _, staged = _run_interleaved(back(g, staged), nxt)


def _gdn_prep(qb, kb, vb, ab, conv_w, alog_pad, dtb_pad, tr, grp):
    b, s, _ = qb.shape
    assert tr % grp == 0 and grp % CHUNK == 0
    chunks = tr // CHUNK
    blk = lambda bi, i: (bi, i, 0)
    halo = lambda bi, i: (bi, jnp.maximum(i * (tr // HALO) - 1, 0), 0)
    const = lambda bi, i: (0, 0)
    const3 = lambda bi, i: (0, 0, 0)
    per_chunk = lambda bi, i: (bi, i, 0, 0)
    t_idx = jnp.arange(grp)
    h_idx = jnp.arange(HALO)
    shift = jnp.stack([(t_idx[:, None] - sft == t_idx[None, :]) for sft in range(1, CONV_WIDTH)]).astype(BF16)
    hsel = jnp.stack([(h_idx[:, None] - sft + HALO == h_idx[None, :])
                      for sft in range(1, CONV_WIDTH)]).astype(BF16)
    head = jnp.arange(BRANCH) // GDN_K_DIM
    head_ones = (head[:, None] == head[None, :]).astype(BF16)
    return pl.pallas_call(
        _gdn_prep_kernel,
        grid=(b, s // tr),
        in_specs=[
            pl.BlockSpec((None, tr, BRANCH), blk),
            pl.BlockSpec((None, tr, BRANCH), blk),
            pl.BlockSpec((None, tr, BRANCH), blk),
            pl.BlockSpec((None, HALO, BRANCH), halo),
            pl.BlockSpec((None, HALO, BRANCH), halo),
            pl.BlockSpec((None, HALO, BRANCH), halo),
            pl.BlockSpec((None, tr, LANES), blk),
            pl.BlockSpec(conv_w.shape, const),
            pl.BlockSpec((1, LANES), const),
            pl.BlockSpec((1, LANES), const),
            pl.BlockSpec(shift.shape, const3),
            pl.BlockSpec(hsel.shape, const3),
            pl.BlockSpec(head_ones.shape, const),
        ],
        out_specs=[
            pl.BlockSpec((None, chunks, PQ_ROWS, BRANCH), per_chunk),
            pl.BlockSpec((None, chunks, NX_ROWS, BRANCH), per_chunk),
            pl.BlockSpec((None, tr, BRANCH), blk),
        ],
        out_shape=[
            jax.ShapeDtypeStruct((b, s // CHUNK, PQ_ROWS, BRANCH), BF16),
            jax.ShapeDtypeStruct((b, s // CHUNK, NX_ROWS, BRANCH), F32),
            jax.ShapeDtypeStruct((b, s, BRANCH), F32),
        ],
        compiler_params=pltpu.CompilerParams(
            dimension_semantics=("arbitrary", "arbitrary"), vmem_limit_bytes=VMEM_LIMIT),
        name="gdn_prep",
    )(qb, kb, vb, qb, kb, vb, ab, conv_w, alog_pad, dtb_pad, shift, hsel, head_ones)


def _gdn_scan_kernel(pq_ref, nx_ref, o0_ref, z_ref, nw_ref, o_ref, state_ref):
    batch, chunks = pq_ref.shape[0], pq_ref.shape[1]

    @pl.when(pl.program_id(0) == 0)
    def _():
        state_ref[...] = jnp.zeros_like(state_ref)

    for c in range(chunks):
        rows = slice(c * CHUNK, (c + 1) * CHUNK)
        for bi in range(batch):
            for h in range(GDN_HEADS):
                hs = slice(h * GDN_K_DIM, (h + 1) * GDN_K_DIM)
                state = state_ref[bi, h]
                r = _dot(pq_ref[bi, c, :, hs], state.astype(BF16))
                gl = nx_ref[bi, c, GDN_K_DIM:GDN_K_DIM + 1, hs]
                state_ref[bi, h] = state * gl - r[:GDN_K_DIM] + nx_ref[bi, c, :GDN_K_DIM, hs]
                o = r[GDN_K_DIM:] + o0_ref[bi, rows, hs]
                ms = jnp.mean(o * o, axis=-1, keepdims=True)
                y = o * lax.rsqrt(ms + EPS) * nw_ref[...]
                o_ref[bi, rows, hs] = (y * z_ref[bi, rows, hs].astype(F32)).astype(o_ref.dtype)


def _gdn_scan(pq, nx, o0, zb, norm_w, chunks):
    b, n_chunks = pq.shape[0], pq.shape[1]
    tr = chunks * CHUNK
    per_chunk = lambda i: (0, i, 0, 0)
    blk = lambda i: (0, i, 0)
    return pl.pallas_call(
        _gdn_scan_kernel,
        grid=(n_chunks // chunks,),
        in_specs=[
            pl.BlockSpec((b, chunks, PQ_ROWS, BRANCH), per_chunk),
            pl.BlockSpec((b, chunks, NX_ROWS, BRANCH), per_chunk),
            pl.BlockSpec((b, tr, BRANCH), blk),
            pl.BlockSpec((b, tr, BRANCH), blk),
            pl.BlockSpec((1, GDN_V_DIM), lambda i: (0, 0)),
        ],
        out_specs=pl.BlockSpec((b, tr, BRANCH), blk),
        out_shape=jax.ShapeDtypeStruct((b, n_chunks * CHUNK, BRANCH), BF16),
        scratch_shapes=[pltpu.VMEM((b, GDN_HEADS, GDN_K_DIM, GDN_V_DIM), F32)],
        compiler_params=pltpu.CompilerParams(
            dimension_semantics=("arbitrary",), vmem_limit_bytes=VMEM_LIMIT),
        name="gdn_scan",
    )(pq, nx, o0, zb, norm_w)


def _out_proj_kernel(x_ref, ya_ref, yb_ref, gate_ref, wa_ref, wb_ref, wo_ref, o_ref):
    d = x_ref.shape[-1]
    up_a = _dot(ya_ref[...], wa_ref[...])
    up_b = _dot(yb_ref[...], wb_ref[...])
    merged = (gate_ref[:, :d].astype(F32) * up_a + gate_ref[:, d:].astype(F32) * up_b)
    o_ref[...] = x_ref[...] + _dot(merged.astype(BF16), wo_ref[...])


def _out_proj(x2d, ya, yb, gates, w_up_a, w_up_b, w_out, tm):
    n, d = x2d.shape
    const = lambda i: (0, 0)
    row = lambda i: (i, 0)
    return pl.pallas_call(
        _out_proj_kernel,
        grid=(n // tm,),
        in_specs=[
            pl.BlockSpec((tm, d), row),
            pl.BlockSpec((tm, BRANCH), row),
            pl.BlockSpec((tm, BRANCH), row),
            pl.BlockSpec((tm, 2 * d), row),
            pl.BlockSpec(w_up_a.shape, const),
            pl.BlockSpec(w_up_b.shape, const),
            pl.BlockSpec(w_out.shape, const),
        ],
        out_specs=pl.BlockSpec((tm, d), row),
        out_shape=jax.ShapeDtypeStruct((n, d), x2d.dtype),
        compiler_params=pltpu.CompilerParams(
            dimension_semantics=("arbitrary",), vmem_limit_bytes=VMEM_LIMIT),
        name="out_proj",
    )(x2d, ya, yb, gates, w_up_a, w_up_b, w_out)


def _lambda_init_for(layer):
    return 0.8 - 0.6 * math.exp(-0.3 * layer)


def kernel(x, norm_w, w_in, da_q_gain, da_k_gain, da_lambda, da_out_gain, gdn_conv_w, gdn_a_log,
           gdn_dt_bias, gdn_norm_w, w_up_da, w_up_gdn, w_out):
    b, s, d = x.shape
    depth = norm_w.shape[0]
    n_main = 8 * BRANCH
    row_tile = 512
    attn_q_tile = 512
    attn_k_tile = 256
    gdn_prep_tile = 512
    gdn_prep_group = 256
    gdn_scan_chunks = 4
    group =jnp.arange(BRANCH) // DA_HEAD_DIM
    gsum = (group[:, None] == group[None, :]).astype(BF16)
    groups_per_seg = BRANCH // DA_HEAD_DIM
    q_scale = DA_HEAD_DIM ** -0.5 * math.log2(math.e)

    x2d = x.reshape(b * s, d)
    for l in range(depth):
        w = w_in[l]
        w_main = jnp.concatenate([w[:, :2 * BRANCH], w[:, 3 * BRANCH:n_main],
                                  w[:, n_main + 2 * GDN_HEADS:]], axis=1).astype(BF16)
        w_vt = w[:, 2 * BRANCH:3 * BRANCH].T.astype(BF16)
        w_ab = jnp.pad(w[:, n_main:n_main + 2 * GDN_HEADS],
                       ((0, 0), (0, LANES - 2 * GDN_HEADS))).astype(BF16)
        q_gain = jnp.tile(da_q_gain[l] * q_scale, groups_per_seg)[None, :]
        k_gain = jnp.tile(da_k_gain[l], groups_per_seg)[None, :]
        qa, ka, vat, za, qb, kb, vb, zb, ab, gates = _in_proj(
            x2d, norm_w[l][None, :], w_main, w_vt, w_ab, q_gain, k_gain, gsum, row_tile)

        r3 = lambda t: t.reshape(b, s, t.shape[-1])
        ya = _diff_attn(r3(qa), r3(ka), vat, r3(za), da_lambda[l], da_out_gain[l][None, :],
                        _lambda_init_for(l), attn_q_tile, attn_k_tile)

        alog_pad = jnp.pad(gdn_a_log[l], (0, LANES - GDN_HEADS))[None, :]
        dtb_pad = jnp.pad(gdn_dt_bias[l], (0, LANES - GDN_HEADS))[None, :]
        pq, nx, o0 = _gdn_prep(r3(qb), r3(kb), r3(vb), r3(ab), gdn_conv_w[l], alog_pad, dtb_pad,
                               gdn_prep_tile, gdn_prep_group)
        yb = _gdn_scan(pq, nx, o0, r3(zb), gdn_norm_w[l][None, :], gdn_scan_chunks)

        x2d = _out_proj(x2d, ya.reshape(b * s, -1), yb.reshape(b * s, -1), gates,
                        w_up_da[l].astype(BF16), w_up_gdn[l].astype(BF16), w_out[l].astype(BF16),
                        row_tile)
    return x2d.reshape(b, s, d)
```

```python
import functools
import math

import jax
import jax.numpy as jnp
from jax import lax
from jax.experimental import pallas as pl
from jax.experimental.pallas import tpu as pltpu

F32 = jnp.float32
BF16 = jnp.bfloat16

CHUNK = 64
EPS = 1e-6
DA_HEADS = 4
DA_HEAD_DIM = 64
DA_V_DIM = 2 * DA_HEAD_DIM
GDN_HEADS = 4
GDN_K_DIM = 128
GDN_V_DIM = 128
CONV_WIDTH = 4
BRANCH = 512
LANES = 128
NEG_BIG = -1e30
ONES_ROWS = 16
UNROLL_BLOCKS = 2
VMEM_LIMIT = 56 * 1024 * 1024

HIGHEST = lax.Precision.HIGHEST


def _dot(a, b, precision=None):
    return jnp.dot(a, b, preferred_element_type=F32, precision=precision)


def _dot_nt(a, b, precision=None):
    return lax.dot_general(a, b, (((1,), (1,)), ((), ())),
                           preferred_element_type=F32, precision=precision)


def _dot_tn(a, b, precision=None):
    return lax.dot_general(a, b, (((0,), (0,)), ((), ())),
                           preferred_element_type=F32, precision=precision)


def _sigmoid(x):
    return 1.0 / (1.0 + jnp.exp(-x))


def _silu(x):
    return x * _sigmoid(x)


def _in_proj_kernel(x_ref, nw_ref, wqk_ref, wmid_ref, wgate_ref, wvt_ref, wab_ref, qg_ref, kg_ref, gsum_ref,
                    qa_ref, ka_ref, vat_ref, za_ref, qb_ref, kb_ref, vb_ref, zb_ref,
                    ab_ref, gate_ref):
    x = x_ref[...]
    ms = jnp.mean(x * x, axis=-1, keepdims=True)
    h = (x * lax.rsqrt(ms + EPS) * nw_ref[...]).astype(BF16)

    def proj(w_ref, j):
        return _dot(h, w_ref[:, j * BRANCH:(j + 1) * BRANCH])

    def qk_norm(t, gain):
        ssq = _dot((t * t).astype(BF16), gsum_ref[...])
        return (t * lax.rsqrt(ssq * (1.0 / DA_HEAD_DIM) + EPS) * gain).astype(BF16)

    qa_ref[...] = qk_norm(proj(wqk_ref, 0), qg_ref[...])
    ka_ref[...] = qk_norm(proj(wqk_ref, 1), kg_ref[...])
    vat_ref[...] = _dot_nt(wvt_ref[...], h).astype(BF16)
    za_ref[...] = _silu(proj(wmid_ref, 0)).astype(BF16)
    qb_ref[...] = proj(wmid_ref, 1).astype(BF16)
    kb_ref[...] = proj(wmid_ref, 2).astype(BF16)
    vb_ref[...] = proj(wmid_ref, 3).astype(BF16)
    zb_ref[...] = _silu(proj(wmid_ref, 4)).astype(BF16)
    gate_ref[...] = _sigmoid(_dot(h, wgate_ref[...])).astype(BF16)
    ab_ref[...] = _dot(h, wab_ref[...])


def _in_proj(x2d, norm_w, w_qk, w_mid, w_gate, w_vt, w_ab, q_gain, k_gain, gsum, tm):
    n, d = x2d.shape
    const = lambda i: (0, 0)
    row = lambda i: (i, 0)
    seg = jax.ShapeDtypeStruct((n, BRANCH), BF16)
    seg_spec = pl.BlockSpec((tm, BRANCH), row)
    out_shape = [seg, seg, jax.ShapeDtypeStruct((BRANCH, n), BF16)] + [seg] * 5 + [
        jax.ShapeDtypeStruct((n, LANES), F32), jax.ShapeDtypeStruct((n, 2 * d), BF16)]
    out_specs = ([seg_spec, seg_spec, pl.BlockSpec((BRANCH, tm), lambda i: (0, i))]
                 + [seg_spec] * 5
                 + [pl.BlockSpec((tm, LANES), row), pl.BlockSpec((tm, 2 * d), row)])
    return pl.pallas_call(
        _in_proj_kernel,
        grid=(n // tm,),
        in_specs=[
            pl.BlockSpec((tm, d), row),
            pl.BlockSpec((1, d), const),
            pl.BlockSpec(w_qk.shape, const, pipeline_mode=pl.Buffered(1)),
            pl.BlockSpec(w_mid.shape, const, pipeline_mode=pl.Buffered(1)),
            pl.BlockSpec(w_gate.shape, const, pipeline_mode=pl.Buffered(1)),
            pl.BlockSpec(w_vt.shape, const, pipeline_mode=pl.Buffered(1)),
            pl.BlockSpec(w_ab.shape, const, pipeline_mode=pl.Buffered(1)),
            pl.BlockSpec((1, BRANCH), const),
            pl.BlockSpec((1, BRANCH), const),
            pl.BlockSpec((BRANCH, BRANCH), const, pipeline_mode=pl.Buffered(1)),
        ],
        out_specs=out_specs,
        out_shape=out_shape,
        compiler_params=pltpu.CompilerParams(
            dimension_semantics=("arbitrary",), vmem_limit_bytes=VMEM_LIMIT),
        name="in_proj",
    )(x2d, norm_w, w_qk, w_mid, w_gate, w_vt, w_ab, q_gain, k_gain, gsum)


def _diff_attn_kernel(lam_ref, q_ref, k_ref, vt_ref, z_ref, og_ref, o_ref, s_scr, m_scr, acc_scr,
                      *, tq, tk, lambda_init):
    i = pl.program_id(2)
    q = q_ref[...]
    lane = lax.broadcasted_iota(jnp.int32, q.shape, 1)
    zero = jnp.zeros_like(q)
    qq = jnp.concatenate([jnp.where(lane < DA_HEAD_DIM, q, zero),
                          jnp.where(lane >= DA_HEAD_DIM, q, zero)], axis=0)
    sub_tiles = s_scr.shape[0]
    ones_rows = jnp.ones((ONES_ROWS, tk), BF16)
    m_scr[...] = jnp.full(m_scr.shape, NEG_BIG, F32)
    acc_scr[...] = jnp.zeros(acc_scr.shape, F32)

    def issue(start, slot, q0=0):
        wq = tq - q0
        qsel = qq if q0 == 0 else jnp.concatenate([qq[q0:tq], qq[tq + q0:]], axis=0)
        s = _dot_nt(k_ref[pl.ds(start, tk), :], qsel)
        s_scr[slot, :, :2 * wq] = s
        return jnp.max(s, axis=0, keepdims=True)

    def consume(start, slot, tile_max, masked, q0=0):
        wq = tq - q0
        s = s_scr[slot, :, :2 * wq]
        if masked:
            k_chunk = (lax.broadcasted_iota(jnp.int32, s.shape, 0) + slot * tk) // CHUNK
            q_chunk = (lax.broadcasted_iota(jnp.int32, s.shape, 1) % wq + q0) // CHUNK
            valid = k_chunk <= q_chunk
            s = jnp.where(valid, s, NEG_BIG)
            tile_max = jnp.max(s, axis=0, keepdims=True)
        m_cols = [slice(g * tq + q0, (g + 1) * tq) for g in range(2)]
        m = jnp.concatenate([m_scr[:, mc] for mc in m_cols], axis=1)
        m_new = jnp.maximum(m, tile_max)
        alpha = jnp.exp2(m - m_new)
        p = jnp.exp2(s - m_new)
        if masked:
            p = jnp.where(valid, p, 0.0)
        pb = p.astype(BF16)
        vt = jnp.concatenate([vt_ref[:, pl.ds(start, tk)], ones_rows], axis=0)
        for g in range(2):
            cols = slice(g * wq, (g + 1) * wq)
            m_scr[:, m_cols[g]] = m_new[:, cols]
            acc_scr[g, :, q0:] = alpha[:, cols] * acc_scr[g, :, q0:] + _dot(vt, pb[:, cols])

    def make_body(blocks):
        def body(jj, tile_max):
            for u in range(blocks):
                for t in range(sub_tiles):
                    start = pl.multiple_of((jj * blocks + u) * tq + t * tk, tk)
                    next_max = issue(start + tk, (t + 1) % sub_tiles)
                    consume(start, t, tile_max, masked=False)
                    tile_max = next_max
            return tile_max
        return body

    paired = i // UNROLL_BLOCKS
    tile_max = lax.fori_loop(0, paired, make_body(UNROLL_BLOCKS), issue(0, 0))
    tile_max = lax.fori_loop(paired * UNROLL_BLOCKS, i, make_body(1), tile_max)
    for t in range(sub_tiles):
        start = pl.multiple_of(i * tq + t * tk, tk)
        if t + 1 < sub_tiles:
            issue(start + tk, t + 1, q0=(t + 1) * tk)
        consume(start, t, None, masked=True, q0=t * tk)
    acc1 = acc_scr[0]
    acc2 = acc_scr[1]

    lam_p = lam_ref[...]
    lam = (jnp.exp(jnp.sum(lam_p[0:1] * lam_p[1:2], axis=-1, keepdims=True))
           - jnp.exp(jnp.sum(lam_p[2:3] * lam_p[3:4], axis=-1, keepdims=True))
           + lambda_init)
    ot = (acc1[:DA_V_DIM] / acc1[DA_V_DIM:DA_V_DIM + 1]
          - lam * (acc2[:DA_V_DIM] / acc2[DA_V_DIM:DA_V_DIM + 1]))
    ms = jnp.mean(ot * ot, axis=0, keepdims=True)
    y = (ot * lax.rsqrt(ms + EPS)).T * og_ref[...] * (1.0 - lambda_init)
    o_ref[...] = (y * z_ref[...].astype(F32)).astype(o_ref.dtype)


def _diff_attn(qa, ka, vat, za, da_lambda, out_gain, lambda_init, tq, tk):
    b, s, _ = qa.shape
    sub_tiles = tq // tk
    assert sub_tiles >= 2 and sub_tiles * tk == tq
    blk = lambda bi, h, i: (bi, i, h)
    const = lambda bi, h, i: (0, 0)
    return pl.pallas_call(
        functools.partial(_diff_attn_kernel, tq=tq, tk=tk, lambda_init=lambda_init),
        scratch_shapes=[pltpu.VMEM((sub_tiles, tk, 2 * tq), F32),
                        pltpu.VMEM((1, 2 * tq), F32),
                        pltpu.VMEM((2, DA_V_DIM + ONES_ROWS, tq), F32)],
        grid=(b, DA_HEADS, s // tq),
        in_specs=[
            pl.BlockSpec(da_lambda.shape, const),
            pl.BlockSpec((None, tq, DA_V_DIM), blk),
            pl.BlockSpec((None, s, DA_V_DIM), lambda bi, h, i: (bi, 0, h)),
            pl.BlockSpec((DA_V_DIM, s), lambda bi, h, i: (h, bi)),
            pl.BlockSpec((None, tq, DA_V_DIM), blk),
            pl.BlockSpec((1, DA_V_DIM), const),
        ],
        out_specs=pl.BlockSpec((None, tq, DA_V_DIM), blk),
        out_shape=jax.ShapeDtypeStruct((b, s, DA_HEADS * DA_V_DIM), BF16),
        compiler_params=pltpu.CompilerParams(
            dimension_semantics=("arbitrary", "arbitrary", "arbitrary"),
            vmem_limit_bytes=VMEM_LIMIT),
        name="diff_attn",
    )(da_lambda, qa, ka, vat, za, out_gain)


WIDE = GDN_HEADS * CHUNK
HALO = 16
PQ_ROWS = GDN_K_DIM + CHUNK
PQN_ROWS = PQ_ROWS + GDN_K_DIM
GL_ROWS = 8


def _split2(x):
    hi = x.astype(BF16)
    lo = (x - hi.astype(F32)).astype(BF16)
    return hi, lo


def _block_diag(y, mask01):
    return jnp.concatenate([y] * GDN_HEADS, axis=0) * mask01


def _wide_mm(x, y, bd_mask):
    return _dot(x.astype(BF16), _block_diag(y.astype(BF16), bd_mask))


def _wide_unit_lower_inverse(a_list, eye, blk16, mm):
    d = [jnp.where(blk16, a, 0.0) for a in a_list]
    off = [a - di for a, di in zip(a_list, d)]
    p = [-di for di in d]
    t = [eye + pi for pi in p]
    for _ in range(3):
        p = [mm(pi, pi) for pi in p]
        yield
        t = [ti + mm(ti, pi) for ti, pi in zip(t, p)]
        yield
    e = [mm(ti, oi) for ti, oi in zip(t, off)]
    yield
    e2 = [mm(ei, ei) for ei in e]
    yield
    f = [eye - ei + e2i - mm(ei, e2i) for ei, e2i in zip(e, e2)]
    yield
    return [mm(fi, ti) for fi, ti in zip(f, t)]


def _run_interleaved(*gens):
    results = [None] * len(gens)
    alive = [g is not None for g in gens]
    while any(alive):
        for n, gen in enumerate(gens):
            if alive[n]:
                try:
                    next(gen)
                except StopIteration as stop:
                    results[n] = stop.value
                    alive[n] = False
    return results


def _head_l2norm(x):
    parts = []
    for h in range(GDN_HEADS):
        xh = x[:, h * GDN_K_DIM:(h + 1) * GDN_K_DIM]
        parts.append(xh * lax.rsqrt(jnp.sum(xh * xh, axis=-1, keepdims=True) + EPS))
    return jnp.concatenate(parts, axis=1)


def _head_bcast(col4, width):
    rows = col4.shape[0]
    return jnp.concatenate([jnp.broadcast_to(col4[:, h:h + 1], (rows, width))
                            for h in range(GDN_HEADS)], axis=1)


def _gdn_prep_kernel(q_ref, k_ref, v_ref, qh_ref, kh_ref, vh_ref, ab_ref, cw_ref, alog_ref, dtb_ref,
                     shift_ref, hsel_ref, pq_ref, gl_ref, o0_ref):
    tr = q_ref.shape[0]
    grp = shift_ref.shape[1]
    chunks = grp // CHUNK

    row_w = lax.broadcasted_iota(jnp.int32, (CHUNK, WIDE), 0)
    col_w = lax.broadcasted_iota(jnp.int32, (CHUNK, WIDE), 1) % CHUNK
    tri_incl = col_w <= row_w
    tri_strict = col_w < row_w
    eye_w = jnp.where(col_w == row_w, 1.0, 0.0).astype(F32)
    blk16 = (row_w // 16) == (col_w // 16)
    bd_mask = jnp.where(lax.broadcasted_iota(jnp.int32, (WIDE, WIDE), 0) // CHUNK
                        == lax.broadcasted_iota(jnp.int32, (WIDE, WIDE), 1) // CHUNK,
                        1.0, 0.0).astype(BF16)
    kbd_mask = jnp.where(lax.broadcasted_iota(jnp.int32, (WIDE, BRANCH), 0) // CHUNK
                         == lax.broadcasted_iota(jnp.int32, (WIDE, BRANCH), 1) // GDN_K_DIM,
                         1.0, 0.0).astype(BF16)
    r64 = lax.broadcasted_iota(jnp.int32, (CHUNK, CHUNK), 0)
    c64 = lax.broadcasted_iota(jnp.int32, (CHUNK, CHUNK), 1)
    lower_ones = jnp.where(c64 <= r64, 1.0, 0.0).astype(BF16)
    mm = functools.partial(_wide_mm, bd_mask=bd_mask)

    def conv_silu(x_ref, x_halo_ref, seg, gr, first_group):
        cols = slice(seg * BRANCH, (seg + 1) * BRANCH)
        cur = x_ref[gr, :]
        if first_group:
            halo = x_halo_ref[...]
            halo = jnp.where(pl.program_id(1) == 0, jnp.zeros((), halo.dtype), halo)
        else:
            halo = x_ref[gr.start - HALO:gr.start, :]
        conv = cur.astype(F32) * cw_ref[CONV_WIDTH - 1:CONV_WIDTH, cols]
        for s in range(1, CONV_WIDTH):
            shifted = _dot(shift_ref[s - 1], cur)
            head = shifted[:HALO] + _dot(hsel_ref[s - 1], halo)
            shifted = jnp.concatenate([head, shifted[HALO:]], axis=0)
            conv = conv + shifted * cw_ref[CONV_WIDTH - 1 - s:CONV_WIDTH - s, cols]
        return _silu(conv)

    def front(g):
        gr = slice(g * grp, (g + 1) * grp)
        q = _head_l2norm(conv_silu(q_ref, qh_ref, 0, gr, g == 0)) * (GDN_K_DIM ** -0.5)
        yield
        k = _head_l2norm(conv_silu(k_ref, kh_ref, 1, gr, g == 0))
        yield
        v = conv_silu(v_ref, vh_ref, 2, gr, g == 0)
        yield

        ab = ab_ref[gr, :]
        xg = ab + dtb_ref[...]
        softplus = jnp.maximum(xg, 0.0) + jnp.log1p(jnp.exp(-jnp.abs(xg)))
        g_all = -jnp.exp(alog_ref[...]) * softplus
        beta_all = _sigmoid(ab)

        staged = []
        for c in range(chunks):
            rows = slice(c * CHUNK, (c + 1) * CHUNK)
            qc, kc, vc = q[rows], k[rows], v[rows]
            g_c = g_all[rows]
            g_hi, g_lo = _split2(g_c)
            gc2 = _dot(lower_ones, jnp.concatenate([g_hi, g_lo], axis=1))
            gc_all = gc2[:, :LANES] + gc2[:, LANES:]
            gcb = _head_bcast(gc_all, GDN_K_DIM)
            betab = _head_bcast(beta_all[rows, GDN_HEADS:], GDN_K_DIM)
            exp_gc = jnp.exp(gcb)
            gl_row = exp_gc[CHUNK - 1:CHUNK]
            k_beta = kc * betab
            v_beta = vc * betab
            kbe = k_beta * exp_gc
            q_dec = qc * exp_gc
            k_tail = kc * jnp.exp(gcb[CHUNK - 1:CHUNK] - gcb)

            r_w = jnp.where(tri_strict, _head_bcast(g_c, CHUNK), 0.0)
            r_hi, r_lo = _split2(r_w)
            d2 = _dot(lower_ones, jnp.concatenate([r_hi, r_lo], axis=1))
            gamma = jnp.exp(jnp.where(tri_incl, d2[:, :WIDE] + d2[:, WIDE:], NEG_BIG))

            k16 = kc.astype(BF16)
            kbd = jnp.concatenate([k16] * GDN_HEADS, axis=0) * kbd_mask
            sc = _dot_nt(jnp.concatenate([k_beta.astype(BF16), qc.astype(BF16)], axis=0), kbd)
            a_strict = jnp.where(tri_strict, sc[:CHUNK] * gamma, 0.0)
            a_intra = jnp.where(tri_incl, sc[CHUNK:] * gamma, 0.0)
            staged.append((a_strict, a_intra, v_beta, kbe, q_dec, k_tail, gl_row))
            yield
        return staged

    def back(g, staged):
        t_inv = yield from _wide_unit_lower_inverse([st[0] for st in staged], eye_w, blk16, mm)
        pairs = [(c, h) for c in range(chunks) for h in range(GDN_HEADS)]
        hsl = lambda h: slice(h * GDN_K_DIM, (h + 1) * GDN_K_DIM)
        wsl = lambda h: slice(h * CHUNK, (h + 1) * CHUNK)
        uw16 = {}
        for c, h in pairs:
            _, _, v_beta, kbe, _, _, _ = staged[c]
            t_off = t_inv[c] - eye_w
            rhs = jnp.concatenate([v_beta[:, hsl(h)], kbe[:, hsl(h)]], axis=1)
            uw = rhs + _dot(t_off[:, wsl(h)].astype(BF16), rhs.astype(BF16))
            uw16[c, h] = uw.astype(BF16)
        yield
        for c, h in pairs:
            oc = g * chunks + c
            a_intra, q_dec = staged[c][1], staged[c][4]
            ai_uw = _dot(a_intra[:, wsl(h)].astype(BF16), uw16[c, h])
            pq_ref[oc, GDN_K_DIM:PQ_ROWS, hsl(h)] = (q_dec[:, hsl(h)]
                                                      - ai_uw[:, GDN_V_DIM:]).astype(pq_ref.dtype)
            o0_ref[oc * CHUNK:(oc + 1) * CHUNK, hsl(h)] = ai_uw[:, :GDN_V_DIM].astype(o0_ref.dtype)
        yield
        for c, h in pairs:
            oc = g * chunks + c
            k_tail = staged[c][5]
            kt_uw = _dot_tn(k_tail[:, hsl(h)].astype(BF16), uw16[c, h])
            pq_ref[oc, :GDN_K_DIM, hsl(h)] = kt_uw[:, GDN_V_DIM:].astype(pq_ref.dtype)
            pq_ref[oc, PQ_ROWS:, hsl(h)] = kt_uw[:, :GDN_V_DIM].astype(pq_ref.dtype)
        for c in range(chunks):
            oc = g * chunks + c
            gl_ref[oc] = jnp.broadcast_to(staged[c][6], (GL_ROWS, BRANCH))
        yield

    groups = tr // grp
    staged, = _run_interleaved(front(0))
    for g in range(groups):
        nxt = front(g + 1) if g + 1 < groups else None
        _, staged = _run_interleaved(back(g, staged), nxt)


def _gdn_prep(qb, kb, vb, ab, conv_w, alog_pad, dtb_pad, tr, grp):
    b, s, _ = qb.shape
    assert tr % grp == 0 and grp % CHUNK == 0
    chunks = tr // CHUNK
    blk = lambda bi, i: (bi, i, 0)
    halo = lambda bi, i: (bi, jnp.maximum(i * (tr // HALO) - 1, 0), 0)
    const = lambda bi, i: (0, 0)
    const3 = lambda bi, i: (0, 0, 0)
    per_chunk = lambda bi, i: (bi, i, 0, 0)
    t_idx = jnp.arange(grp)
    h_idx = jnp.arange(HALO)
    shift = jnp.stack([(t_idx[:, None] - sft == t_idx[None, :]) for sft in range(1, CONV_WIDTH)]).astype(BF16)
    hsel = jnp.stack([(h_idx[:, None] - sft + HALO == h_idx[None, :])
                      for sft in range(1, CONV_WIDTH)]).astype(BF16)
    return pl.pallas_call(
        _gdn_prep_kernel,
        grid=(b, s // tr),
        in_specs=[
            pl.BlockSpec((None, tr, BRANCH), blk),
            pl.BlockSpec((None, tr, BRANCH), blk),
            pl.BlockSpec((None, tr, BRANCH), blk),
            pl.BlockSpec((None, HALO, BRANCH), halo),
            pl.BlockSpec((None, HALO, BRANCH), halo),
            pl.BlockSpec((None, HALO, BRANCH), halo),
            pl.BlockSpec((None, tr, LANES), blk),
            pl.BlockSpec(conv_w.shape, const),
            pl.BlockSpec((1, LANES), const),
            pl.BlockSpec((1, LANES), const),
            pl.BlockSpec(shift.shape, const3),
            pl.BlockSpec(hsel.shape, const3),
        ],
        out_specs=[
            pl.BlockSpec((None, chunks, PQN_ROWS, BRANCH), per_chunk),
            pl.BlockSpec((None, chunks, GL_ROWS, BRANCH), per_chunk),
            pl.BlockSpec((None, tr, BRANCH), blk),
        ],
        out_shape=[
            jax.ShapeDtypeStruct((b, s // CHUNK, PQN_ROWS, BRANCH), BF16),
            jax.ShapeDtypeStruct((b, s // CHUNK, GL_ROWS, BRANCH), F32),
            jax.ShapeDtypeStruct((b, s, BRANCH), BF16),
        ],
        compiler_params=pltpu.CompilerParams(
            dimension_semantics=("arbitrary", "arbitrary"), vmem_limit_bytes=VMEM_LIMIT),
        name="gdn_prep",
    )(qb, kb, vb, qb, kb, vb, ab, conv_w, alog_pad, dtb_pad, shift, hsel)


def _gdn_scan_kernel(pq_ref, gl_ref, o0_ref, z_ref, nw_ref, o_ref, state_ref):
    batch, chunks = pq_ref.shape[0], pq_ref.shape[1]

    @pl.when(pl.program_id(0) == 0)
    def _():
        state_ref[...] = jnp.zeros_like(state_ref)

    for c in range(chunks):
        rows = slice(c * CHUNK, (c + 1) * CHUNK)
        for bi in range(batch):
            for h in range(GDN_HEADS):
                hs = slice(h * GDN_K_DIM, (h + 1) * GDN_K_DIM)
                state = state_ref[bi, h]
                r = _dot(pq_ref[bi, c, :PQ_ROWS, hs], state.astype(BF16))
                gl = gl_ref[bi, c, 0:1, hs]
                n_c = pq_ref[bi, c, PQ_ROWS:, hs].astype(F32)
                state_ref[bi, h] = state * gl - r[:GDN_K_DIM] + n_c
                o = r[GDN_K_DIM:] + o0_ref[bi, rows, hs].astype(F32)
                ms = jnp.mean(o * o, axis=-1, keepdims=True)
                y = o * lax.rsqrt(ms + EPS) * nw_ref[...]
                o_ref[bi, rows, hs] = (y * z_ref[bi, rows, hs].astype(F32)).astype(o_ref.dtype)


def _gdn_scan(pq, gl, o0, zb, norm_w, chunks):
    b, n_chunks = pq.shape[0], pq.shape[1]
    tr = chunks * CHUNK
    per_chunk = lambda i: (0, i, 0, 0)
    blk = lambda i: (0, i, 0)
    return pl.pallas_call(
        _gdn_scan_kernel,
        grid=(n_chunks // chunks,),
        in_specs=[
            pl.BlockSpec((b, chunks, PQN_ROWS, BRANCH), per_chunk),
            pl.BlockSpec((b, chunks, GL_ROWS, BRANCH), per_chunk),
            pl.BlockSpec((b, tr, BRANCH), blk),
            pl.BlockSpec((b, tr, BRANCH), blk),
            pl.BlockSpec((1, GDN_V_DIM), lambda i: (0, 0)),
        ],
        out_specs=pl.BlockSpec((b, tr, BRANCH), blk),
        out_shape=jax.ShapeDtypeStruct((b, n_chunks * CHUNK, BRANCH), BF16),
        scratch_shapes=[pltpu.VMEM((b, GDN_HEADS, GDN_K_DIM, GDN_V_DIM), F32)],
        compiler_params=pltpu.CompilerParams(
            dimension_semantics=("arbitrary",), vmem_limit_bytes=VMEM_LIMIT),
        name="gdn_scan",
    )(pq, gl, o0, zb, norm_w)


def _out_proj_kernel(x_ref, ya_ref, yb_ref, gate_ref, wa_ref, wb_ref, wo_ref, o_ref):
    d = x_ref.shape[-1]
    up_a = _dot(ya_ref[...], wa_ref[...])
    up_b = _dot(yb_ref[...], wb_ref[...])
    merged = (gate_ref[:, :d].astype(F32) * up_a + gate_ref[:, d:].astype(F32) * up_b)
    o_ref[...] = x_ref[...] + _dot(merged.astype(BF16), wo_ref[...])


def _out_proj(x2d, ya, yb, gates, w_up_a, w_up_b, w_out, tm):
    n, d = x2d.shape
    const = lambda i: (0, 0)
    row = lambda i: (i, 0)
    return pl.pallas_call(
        _out_proj_kernel,
        grid=(n // tm,),
        in_specs=[
            pl.BlockSpec((tm, d), row),
            pl.BlockSpec((tm, BRANCH), row),
            pl.BlockSpec((tm, BRANCH), row),
            pl.BlockSpec((tm, 2 * d), row),
            pl.BlockSpec(w_up_a.shape, const),
            pl.BlockSpec(w_up_b.shape, const),
            pl.BlockSpec(w_out.shape, const),
        ],
        out_specs=pl.BlockSpec((tm, d), row),
        out_shape=jax.ShapeDtypeStruct((n, d), x2d.dtype),
        compiler_params=pltpu.CompilerParams(
            dimension_semantics=("arbitrary",), vmem_limit_bytes=VMEM_LIMIT),
        name="out_proj",
    )(x2d, ya, yb, gates, w_up_a, w_up_b, w_out)


def _lambda_init_for(layer):
    return 0.8 - 0.6 * math.exp(-0.3 * layer)


def kernel(x, norm_w, w_in, da_q_gain, da_k_gain, da_lambda, da_out_gain, gdn_conv_w, gdn_a_log,
           gdn_dt_bias, gdn_norm_w, w_up_da, w_up_gdn, w_out):
    b, s, d = x.shape
    depth = norm_w.shape[0]
    n_main = 8 * BRANCH
    row_tile = 512
    attn_q_tile = 512
    attn_k_tile = 256
    gdn_prep_tile = 512
    gdn_prep_group = 256
    gdn_scan_chunks = 4
    group =jnp.arange(BRANCH) // DA_HEAD_DIM
    gsum = (group[:, None] == group[None, :]).astype(BF16)
    groups_per_seg = BRANCH // DA_HEAD_DIM
    q_scale = DA_HEAD_DIM ** -0.5 * math.log2(math.e)

    x2d = x.reshape(b * s, d)
    for l in range(depth):
        w_qk = w_in[l, :, :2 * BRANCH].astype(BF16)
        w_vt = w_in[l, :, 2 * BRANCH:3 * BRANCH].T.astype(BF16)
        w_mid = w_in[l, :, 3 * BRANCH:n_main].astype(BF16)
        w_ab = jnp.pad(w_in[l, :, n_main:n_main + 2 * GDN_HEADS],
                       ((0, 0), (0, LANES - 2 * GDN_HEADS))).astype(BF16)
        w_gate = w_in[l, :, n_main + 2 * GDN_HEADS:].astype(BF16)
        q_gain = jnp.tile(da_q_gain[l] * q_scale, groups_per_seg)[None, :]
        k_gain = jnp.tile(da_k_gain[l], groups_per_seg)[None, :]
        qa, ka, vat, za, qb, kb, vb, zb, ab, gates = _in_proj(
            x2d, norm_w[l][None, :], w_qk, w_mid, w_gate, w_vt, w_ab, q_gain, k_gain, gsum, row_tile)

        r3 = lambda t: t.reshape(b, s, t.shape[-1])
        ya = _diff_attn(r3(qa), r3(ka), vat, r3(za), da_lambda[l], da_out_gain[l][None, :],
                        _lambda_init_for(l), attn_q_tile, attn_k_tile)

        alog_pad = jnp.pad(gdn_a_log[l], (0, LANES - GDN_HEADS))[None, :]
        dtb_pad = jnp.pad(gdn_dt_bias[l], (0, LANES - GDN_HEADS))[None, :]
        pqn, gl, o0 = _gdn_prep(r3(qb), r3(kb), r3(vb), r3(ab), gdn_conv_w[l], alog_pad, dtb_pad,
                               gdn_prep_tile, gdn_prep_group)
        yb = _gdn_scan(pqn, gl, o0, r3(zb), gdn_norm_w[l][None, :], gdn_scan_chunks)

        x2d = _out_proj(x2d, ya.reshape(b * s, -1), yb.reshape(b * s, -1), gates,
                        w_up_da[l].astype(BF16), w_up_gdn[l].astype(BF16), w_out[l].astype(BF16),
                        row_tile)
    return x2d.reshape(b, s, d)
```

```python
import functools
import math

import jax
import jax.numpy as jnp
from jax import lax
from jax.experimental import pallas as pl
from jax.experimental.pallas import tpu as pltpu

F32 = jnp.float32
BF16 = jnp.bfloat16

CHUNK = 64
EPS = 1e-6
DA_HEADS = 4
DA_HEAD_DIM = 64
DA_V_DIM = 2 * DA_HEAD_DIM
GDN_HEADS = 4
GDN_K_DIM = 128
GDN_V_DIM = 128
CONV_WIDTH = 4
BRANCH = 512
LANES = 128
NEG_BIG = -1e30
ONES_ROWS = 16
UNROLL_BLOCKS = 2
VMEM_LIMIT = 56 * 1024 * 1024

HIGHEST = lax.Precision.HIGHEST


def _dot(a, b, precision=None):
    return jnp.dot(a, b, preferred_element_type=F32, precision=precision)


def _dot_nt(a, b, precision=None):
    return lax.dot_general(a, b, (((1,), (1,)), ((), ())),
                           preferred_element_type=F32, precision=precision)


def _dot_tn(a, b, precision=None):
    return lax.dot_general(a, b, (((0,), (0,)), ((), ())),
                           preferred_element_type=F32, precision=precision)


def _sigmoid(x):
    return 1.0 / (1.0 + jnp.exp(-x))


def _silu(x):
    return x * _sigmoid(x)


def _weight_prep_kernel(w_ref, wb_ref, wvt_ref, *, v_block):
    w = w_ref[...]
    wb_ref[...] = w.astype(BF16)

    @pl.when(pl.program_id(0) == v_block)
    def _():
        wvt_ref[...] = w.T.astype(BF16)


def _weight_prep(w, n_cols):
    d = w.shape[0]
    return pl.pallas_call(
        functools.partial(_weight_prep_kernel, v_block=2),
        grid=(n_cols // BRANCH,),
        in_specs=[pl.BlockSpec((d, BRANCH), lambda j: (0, j))],
        out_specs=[pl.BlockSpec((d, BRANCH), lambda j: (0, j)),
                   pl.BlockSpec((BRANCH, d), lambda j: (0, 0))],
        out_shape=[jax.ShapeDtypeStruct((d, n_cols), BF16), jax.ShapeDtypeStruct((BRANCH, d), BF16)],
        compiler_params=pltpu.CompilerParams(
            dimension_semantics=("arbitrary",), vmem_limit_bytes=VMEM_LIMIT),
        name="weight_prep",
    )(w)


def _in_proj_kernel(x_ref, nw_ref, wb_ref, wlast_ref, wvt_ref, qg_ref, kg_ref, gsum_ref,
                    qa_ref, ka_ref, vat_ref, za_ref, qb_ref, kb_ref, vb_ref, zb_ref,
                    ab_ref, gate_ref):
    x = x_ref[...]
    ms = jnp.mean(x * x, axis=-1, keepdims=True)
    h = (x * lax.rsqrt(ms + EPS) * nw_ref[...]).astype(BF16)

    def proj(j):
        return _dot(h, wb_ref[:, j * BRANCH:(j + 1) * BRANCH])

    def qk_norm(t, gain):
        ssq = _dot((t * t).astype(BF16), gsum_ref[...])
        return (t * lax.rsqrt(ssq * (1.0 / DA_HEAD_DIM) + EPS) * gain).astype(BF16)

    qa_ref[...] = qk_norm(proj(0), qg_ref[...])
    ka_ref[...] = qk_norm(proj(1), kg_ref[...])
    vat_ref[...] = _dot_nt(wvt_ref[...], h).astype(BF16)
    za_ref[...] = _silu(proj(3)).astype(BF16)
    qb_ref[...] = proj(4).astype(BF16)
    kb_ref[...] = proj(5).astype(BF16)
    vb_ref[...] = proj(6).astype(BF16)
    zb_ref[...] = _silu(proj(7)).astype(BF16)
    tail = jnp.concatenate([_dot(h, wb_ref[:, 8 * BRANCH:]), _dot(h, wlast_ref[...])], axis=1)
    ab_ref[...] = tail[:, :LANES]
    n_ab = 2 * GDN_HEADS
    gate_ref[...] = _sigmoid(tail[:, n_ab:n_ab + gate_ref.shape[1]]).astype(BF16)


def _in_proj(x2d, norm_w, w_b, w_last, w_vt, q_gain, k_gain, gsum, tm):
    n, d = x2d.shape
    const = lambda i: (0, 0)
    row = lambda i: (i, 0)
    seg = jax.ShapeDtypeStruct((n, BRANCH), BF16)
    seg_spec = pl.BlockSpec((tm, BRANCH), row)
    out_shape = [seg, seg, jax.ShapeDtypeStruct((BRANCH, n), BF16)] + [seg] * 5 + [
        jax.ShapeDtypeStruct((n, LANES), F32), jax.ShapeDtypeStruct((n, 2 * d), BF16)]
    out_specs = ([seg_spec, seg_spec, pl.BlockSpec((BRANCH, tm), lambda i: (0, i))]
                 + [seg_spec] * 5
                 + [pl.BlockSpec((tm, LANES), row), pl.BlockSpec((tm, 2 * d), row)])
    return pl.pallas_call(
        _in_proj_kernel,
        grid=(n // tm,),
        in_specs=[
            pl.BlockSpec((tm, d), row),
            pl.BlockSpec((1, d), const),
            pl.BlockSpec(w_b.shape, const, pipeline_mode=pl.Buffered(1)),
            pl.BlockSpec(w_last.shape, const, pipeline_mode=pl.Buffered(1)),
            pl.BlockSpec(w_vt.shape, const, pipeline_mode=pl.Buffered(1)),
            pl.BlockSpec((1, BRANCH), const),
            pl.BlockSpec((1, BRANCH), const),
            pl.BlockSpec((BRANCH, BRANCH), const, pipeline_mode=pl.Buffered(1)),
        ],
        out_specs=out_specs,
        out_shape=out_shape,
        compiler_params=pltpu.CompilerParams(
            dimension_semantics=("arbitrary",), vmem_limit_bytes=VMEM_LIMIT),
        name="in_proj",
    )(x2d, norm_w, w_b, w_last, w_vt, q_gain, k_gain, gsum)


def _diff_attn_kernel(lam_ref, q_ref, k_ref, vt_ref, z_ref, og_ref, o_ref, s_scr, m_scr, acc_scr,
                      *, tq, tk, lambda_init):
    i = pl.program_id(2)
    q = q_ref[...]
    lane = lax.broadcasted_iota(jnp.int32, q.shape, 1)
    zero = jnp.zeros_like(q)
    qq = jnp.concatenate([jnp.where(lane < DA_HEAD_DIM, q, zero),
                          jnp.where(lane >= DA_HEAD_DIM, q, zero)], axis=0)
    sub_tiles = s_scr.shape[0]
    ones_rows = jnp.ones((ONES_ROWS, tk), BF16)
    m_scr[...] = jnp.full(m_scr.shape, NEG_BIG, F32)
    acc_scr[...] = jnp.zeros(acc_scr.shape, F32)

    def issue(start, slot, q0=0):
        wq = tq - q0
        qsel = qq if q0 == 0 else jnp.concatenate([qq[q0:tq], qq[tq + q0:]], axis=0)
        s = _dot_nt(k_ref[pl.ds(start, tk), :], qsel)
        s_scr[slot, :, :2 * wq] = s
        return jnp.max(s, axis=0, keepdims=True)

    def consume(start, slot, tile_max, masked, q0=0):
        wq = tq - q0
        s = s_scr[slot, :, :2 * wq]
        if masked:
            k_chunk = (lax.broadcasted_iota(jnp.int32, s.shape, 0) + slot * tk) // CHUNK
            q_chunk = (lax.broadcasted_iota(jnp.int32, s.shape, 1) % wq + q0) // CHUNK
            valid = k_chunk <= q_chunk
            s = jnp.where(valid, s, NEG_BIG)
            tile_max = jnp.max(s, axis=0, keepdims=True)
        m_cols = [slice(g * tq + q0, (g + 1) * tq) for g in range(2)]
        m = jnp.concatenate([m_scr[:, mc] for mc in m_cols], axis=1)
        m_new = jnp.maximum(m, tile_max)
        alpha = jnp.exp2(m - m_new)
        p = jnp.exp2(s - m_new)
        if masked:
            p = jnp.where(valid, p, 0.0)
        pb = p.astype(BF16)
        vt = jnp.concatenate([vt_ref[:, pl.ds(start, tk)], ones_rows], axis=0)
        for g in range(2):
            cols = slice(g * wq, (g + 1) * wq)
            m_scr[:, m_cols[g]] = m_new[:, cols]
            acc_scr[g, :, q0:] = alpha[:, cols] * acc_scr[g, :, q0:] + _dot(vt, pb[:, cols])

    def make_body(blocks):
        def body(jj, tile_max):
            for u in range(blocks):
                for t in range(sub_tiles):
                    start = pl.multiple_of((jj * blocks + u) * tq + t * tk, tk)
                    next_max = issue(start + tk, (t + 1) % sub_tiles)
                    consume(start, t, tile_max, masked=False)
                    tile_max = next_max
            return tile_max
        return body

    paired = i // UNROLL_BLOCKS
    tile_max = lax.fori_loop(0, paired, make_body(UNROLL_BLOCKS), issue(0, 0))
    tile_max = lax.fori_loop(paired * UNROLL_BLOCKS, i, make_body(1), tile_max)
    for t in range(sub_tiles):
        start = pl.multiple_of(i * tq + t * tk, tk)
        if t + 1 < sub_tiles:
            issue(start + tk, t + 1, q0=(t + 1) * tk)
        consume(start, t, None, masked=True, q0=t * tk)
    acc1 = acc_scr[0]
    acc2 = acc_scr[1]

    lam_p = lam_ref[...]
    lam = (jnp.exp(jnp.sum(lam_p[0:1] * lam_p[1:2], axis=-1, keepdims=True))
           - jnp.exp(jnp.sum(lam_p[2:3] * lam_p[3:4], axis=-1, keepdims=True))
           + lambda_init)
    ot = (acc1[:DA_V_DIM] / acc1[DA_V_DIM:DA_V_DIM + 1]
          - lam * (acc2[:DA_V_DIM] / acc2[DA_V_DIM:DA_V_DIM + 1]))
    ms = jnp.mean(ot * ot, axis=0, keepdims=True)
    y = (ot * lax.rsqrt(ms + EPS)).T * og_ref[...] * (1.0 - lambda_init)
    o_ref[...] = (y * z_ref[...].astype(F32)).astype(o_ref.dtype)


def _diff_attn(qa, ka, vat, za, da_lambda, out_gain, lambda_init, tq, tk):
    b, s, _ = qa.shape
    sub_tiles = tq // tk
    assert sub_tiles >= 2 and sub_tiles * tk == tq
    blk = lambda bi, h, i: (bi, i, h)
    const = lambda bi, h, i: (0, 0)
    return pl.pallas_call(
        functools.partial(_diff_attn_kernel, tq=tq, tk=tk, lambda_init=lambda_init),
        scratch_shapes=[pltpu.VMEM((sub_tiles, tk, 2 * tq), F32),
                        pltpu.VMEM((1, 2 * tq), F32),
                        pltpu.VMEM((2, DA_V_DIM + ONES_ROWS, tq), F32)],
        grid=(b, DA_HEADS, s // tq),
        in_specs=[
            pl.BlockSpec(da_lambda.shape, const),
            pl.BlockSpec((None, tq, DA_V_DIM), blk),
            pl.BlockSpec((None, s, DA_V_DIM), lambda bi, h, i: (bi, 0, h)),
            pl.BlockSpec((DA_V_DIM, s), lambda bi, h, i: (h, bi)),
            pl.BlockSpec((None, tq, DA_V_DIM), blk),
            pl.BlockSpec((1, DA_V_DIM), const),
        ],
        out_specs=pl.BlockSpec((None, tq, DA_V_DIM), blk),
        out_shape=jax.ShapeDtypeStruct((b, s, DA_HEADS * DA_V_DIM), BF16),
        compiler_params=pltpu.CompilerParams(
            dimension_semantics=("arbitrary", "arbitrary", "arbitrary"),
            vmem_limit_bytes=VMEM_LIMIT),
        name="diff_attn",
    )(da_lambda, qa, ka, vat, za, out_gain)


WIDE = GDN_HEADS * CHUNK
HALO = 16
PQ_ROWS = GDN_K_DIM + CHUNK
PQN_ROWS = PQ_ROWS + GDN_K_DIM
GL_ROWS = 8


def _split2(x):
    hi = x.astype(BF16)
    lo = (x - hi.astype(F32)).astype(BF16)
    return hi, lo


def _block_diag(y, mask01):
    return jnp.concatenate([y] * GDN_HEADS, axis=0) * mask01


def _wide_mm(x, y, bd_mask):
    return _dot(x.astype(BF16), _block_diag(y.astype(BF16), bd_mask))


def _wide_unit_lower_inverse(a_list, eye, blk16, mm):
    d = [jnp.where(blk16, a, 0.0) for a in a_list]
    off = [a - di for a, di in zip(a_list, d)]
    p = [-di for di in d]
    t = [eye + pi for pi in p]
    for _ in range(3):
        p = [mm(pi, pi) for pi in p]
        yield
        t = [ti + mm(ti, pi) for ti, pi in zip(t, p)]
        yield
    e = [mm(ti, oi) for ti, oi in zip(t, off)]
    yield
    e2 = [mm(ei, ei) for ei in e]
    yield
    f = [eye - ei + e2i - mm(ei, e2i) for ei, e2i in zip(e, e2)]
    yield
    return [mm(fi, ti) for fi, ti in zip(f, t)]


def _run_interleaved(*gens):
    results = [None] * len(gens)
    alive = [g is not None for g in gens]
    while any(alive):
        for n, gen in enumerate(gens):
            if alive[n]:
                try:
                    next(gen)
                except StopIteration as stop:
                    results[n] = stop.value
                    alive[n] = False
    return results


def _head_l2norm(x):
    parts = []
    for h in range(GDN_HEADS):
        xh = x[:, h * GDN_K_DIM:(h + 1) * GDN_K_DIM]
        parts.append(xh * lax.rsqrt(jnp.sum(xh * xh, axis=-1, keepdims=True) + EPS))
    return jnp.concatenate(parts, axis=1)


def _head_bcast(col4, width):
    rows = col4.shape[0]
    return jnp.concatenate([jnp.broadcast_to(col4[:, h:h + 1], (rows, width))
                            for h in range(GDN_HEADS)], axis=1)


def _gdn_prep_kernel(q_ref, k_ref, v_ref, qh_ref, kh_ref, vh_ref, ab_ref, cw_ref, alog_ref, dtb_ref,
                     shift_ref, hsel_ref, pq_ref, gl_ref, o0_ref):
    tr = q_ref.shape[0]
    grp = shift_ref.shape[1]
    chunks = grp // CHUNK

    row_w = lax.broadcasted_iota(jnp.int32, (CHUNK, WIDE), 0)
    col_w = lax.broadcasted_iota(jnp.int32, (CHUNK, WIDE), 1) % CHUNK
    tri_incl = col_w <= row_w
    tri_strict = col_w < row_w
    eye_w = jnp.where(col_w == row_w, 1.0, 0.0).astype(F32)
    blk16 = (row_w // 16) == (col_w // 16)
    bd_mask = jnp.where(lax.broadcasted_iota(jnp.int32, (WIDE, WIDE), 0) // CHUNK
                        == lax.broadcasted_iota(jnp.int32, (WIDE, WIDE), 1) // CHUNK,
                        1.0, 0.0).astype(BF16)
    kbd_mask = jnp.where(lax.broadcasted_iota(jnp.int32, (WIDE, BRANCH), 0) // CHUNK
                         == lax.broadcasted_iota(jnp.int32, (WIDE, BRANCH), 1) // GDN_K_DIM,
                         1.0, 0.0).astype(BF16)
    r64 = lax.broadcasted_iota(jnp.int32, (CHUNK, CHUNK), 0)
    c64 = lax.broadcasted_iota(jnp.int32, (CHUNK, CHUNK), 1)
    lower_ones = jnp.where(c64 <= r64, 1.0, 0.0).astype(BF16)
    mm = functools.partial(_wide_mm, bd_mask=bd_mask)

    def conv_silu(x_ref, x_halo_ref, seg, gr, first_group):
        cols = slice(seg * BRANCH, (seg + 1) * BRANCH)
        cur = x_ref[gr, :]
        if first_group:
            halo = x_halo_ref[...]
            halo = jnp.where(pl.program_id(1) == 0, jnp.zeros((), halo.dtype), halo)
        else:
            halo = x_ref[gr.start - HALO:gr.start, :]
        conv = cur.astype(F32) * cw_ref[CONV_WIDTH - 1:CONV_WIDTH, cols]
        for s in range(1, CONV_WIDTH):
            shifted = _dot(shift_ref[s - 1], cur)
            head = shifted[:HALO] + _dot(hsel_ref[s - 1], halo)
            shifted = jnp.concatenate([head, shifted[HALO:]], axis=0)
            conv = conv + shifted * cw_ref[CONV_WIDTH - 1 - s:CONV_WIDTH - s, cols]
        return _silu(conv)

    def front(g):
        gr = slice(g * grp, (g + 1) * grp)
        q = _head_l2norm(conv_silu(q_ref, qh_ref, 0, gr, g == 0)) * (GDN_K_DIM ** -0.5)
        yield
        k = _head_l2norm(conv_silu(k_ref, kh_ref, 1, gr, g == 0))
        yield
        v = conv_silu(v_ref, vh_ref, 2, gr, g == 0)
        yield

        ab = ab_ref[gr, :]
        xg = ab + dtb_ref[...]
        softplus = jnp.maximum(xg, 0.0) + jnp.log1p(jnp.exp(-jnp.abs(xg)))
        g_all = -jnp.exp(alog_ref[...]) * softplus
        beta_all = _sigmoid(ab)

        staged = []
        for c in range(chunks):
            rows = slice(c * CHUNK, (c + 1) * CHUNK)
            qc, kc, vc = q[rows], k[rows], v[rows]
            g_c = g_all[rows]
            g_hi, g_lo = _split2(g_c)
            gc2 = _dot(lower_ones, jnp.concatenate([g_hi, g_lo], axis=1))
            gc_all = gc2[:, :LANES] + gc2[:, LANES:]
            gcb = _head_bcast(gc_all, GDN_K_DIM)
            betab = _head_bcast(beta_all[rows, GDN_HEADS:], GDN_K_DIM)
            exp_gc = jnp.exp(gcb)
            gl_row = exp_gc[CHUNK - 1:CHUNK]
            k_beta = kc * betab
            v_beta = vc * betab
            kbe = k_beta * exp_gc
            q_dec = qc * exp_gc
            k_tail = kc * jnp.exp(gcb[CHUNK - 1:CHUNK] - gcb)

            r_w = jnp.where(tri_strict, _head_bcast(g_c, CHUNK), 0.0)
            r_hi, r_lo = _split2(r_w)
            d2 = _dot(lower_ones, jnp.concatenate([r_hi, r_lo], axis=1))
            gamma = jnp.exp(jnp.where(tri_incl, d2[:, :WIDE] + d2[:, WIDE:], NEG_BIG))

            k16 = kc.astype(BF16)
            kbd = jnp.concatenate([k16] * GDN_HEADS, axis=0) * kbd_mask
            sc = _dot_nt(jnp.concatenate([k_beta.astype(BF16), qc.astype(BF16)], axis=0), kbd)
            a_strict = jnp.where(tri_strict, sc[:CHUNK] * gamma, 0.0)
            a_intra = jnp.where(tri_incl, sc[CHUNK:] * gamma, 0.0)
            staged.append((a_strict, a_intra, v_beta, kbe, q_dec, k_tail, gl_row))
            yield
        return staged

    def back(g, staged):
        t_inv = yield from _wide_unit_lower_inverse([st[0] for st in staged], eye_w, blk16, mm)
        pairs = [(c, h) for c in range(chunks) for h in range(GDN_HEADS)]
        hsl = lambda h: slice(h * GDN_K_DIM, (h + 1) * GDN_K_DIM)
        wsl = lambda h: slice(h * CHUNK, (h + 1) * CHUNK)
        uw16 = {}
        for c, h in pairs:
            _, _, v_beta, kbe, _, _, _ = staged[c]
            t_off = t_inv[c] - eye_w
            rhs = jnp.concatenate([v_beta[:, hsl(h)], kbe[:, hsl(h)]], axis=1)
            uw = rhs + _dot(t_off[:, wsl(h)].astype(BF16), rhs.astype(BF16))
            uw16[c, h] = uw.astype(BF16)
        yield
        for c, h in pairs:
            oc = g * chunks + c
            a_intra, q_dec = staged[c][1], staged[c][4]
            ai_uw = _dot(a_intra[:, wsl(h)].astype(BF16), uw16[c, h])
            pq_ref[oc, GDN_K_DIM:PQ_ROWS, hsl(h)] = (q_dec[:, hsl(h)]
                                                      - ai_uw[:, GDN_V_DIM:]).astype(pq_ref.dtype)
            o0_ref[oc * CHUNK:(oc + 1) * CHUNK, hsl(h)] = ai_uw[:, :GDN_V_DIM].astype(o0_ref.dtype)
        yield
        for c, h in pairs:
            oc = g * chunks + c
            k_tail = staged[c][5]
            kt_uw = _dot_tn(k_tail[:, hsl(h)].astype(BF16), uw16[c, h])
            pq_ref[oc, :GDN_K_DIM, hsl(h)] = kt_uw[:, GDN_V_DIM:].astype(pq_ref.dtype)
            pq_ref[oc, PQ_ROWS:, hsl(h)] = kt_uw[:, :GDN_V_DIM].astype(pq_ref.dtype)
        for c in range(chunks):
            oc = g * chunks + c
            gl_ref[oc] = jnp.broadcast_to(staged[c][6], (GL_ROWS, BRANCH))
        yield

    groups = tr // grp
    staged, = _run_interleaved(front(0))
    for g in range(groups):
        nxt = front(g + 1) if g + 1 < groups else None
        _, staged = _run_interleaved(back(g, staged), nxt)


def _gdn_prep(qb, kb, vb, ab, conv_w, alog_pad, dtb_pad, tr, grp):
    b, s, _ = qb.shape
    assert tr % grp == 0 and grp % CHUNK == 0
    chunks = tr // CHUNK
    blk = lambda bi, i: (bi, i, 0)
    halo = lambda bi, i: (bi, jnp.maximum(i * (tr // HALO) - 1, 0), 0)
    const = lambda bi, i: (0, 0)
    const3 = lambda bi, i: (0, 0, 0)
    per_chunk = lambda bi, i: (bi, i, 0, 0)
    t_idx = jnp.arange(grp)
    h_idx = jnp.arange(HALO)
    shift = jnp.stack([(t_idx[:, None] - sft == t_idx[None, :]) for sft in range(1, CONV_WIDTH)]).astype(BF16)
    hsel = jnp.stack([(h_idx[:, None] - sft + HALO == h_idx[None, :])
                      for sft in range(1, CONV_WIDTH)]).astype(BF16)
    return pl.pallas_call(
        _gdn_prep_kernel,
        grid=(b, s // tr),
        in_specs=[
            pl.BlockSpec((None, tr, BRANCH), blk),
            pl.BlockSpec((None, tr, BRANCH), blk),
            pl.BlockSpec((None, tr, BRANCH), blk),
            pl.BlockSpec((None, HALO, BRANCH), halo),
            pl.BlockSpec((None, HALO, BRANCH), halo),
            pl.BlockSpec((None, HALO, BRANCH), halo),
            pl.BlockSpec((None, tr, LANES), blk),
            pl.BlockSpec(conv_w.shape, const),
            pl.BlockSpec((1, LANES), const),
            pl.BlockSpec((1, LANES), const),
            pl.BlockSpec(shift.shape, const3),
            pl.BlockSpec(hsel.shape, const3),
        ],
        out_specs=[
            pl.BlockSpec((None, chunks, PQN_ROWS, BRANCH), per_chunk),
            pl.BlockSpec((None, chunks, GL_ROWS, BRANCH), per_chunk),
            pl.BlockSpec((None, tr, BRANCH), blk),
        ],
        out_shape=[
            jax.ShapeDtypeStruct((b, s // CHUNK, PQN_ROWS, BRANCH), BF16),
            jax.ShapeDtypeStruct((b, s // CHUNK, GL_ROWS, BRANCH), F32),
            jax.ShapeDtypeStruct((b, s, BRANCH), BF16),
        ],
        compiler_params=pltpu.CompilerParams(
            dimension_semantics=("arbitrary", "arbitrary"), vmem_limit_bytes=VMEM_LIMIT),
        name="gdn_prep",
    )(qb, kb, vb, qb, kb, vb, ab, conv_w, alog_pad, dtb_pad, shift, hsel)


def _gdn_scan_kernel(pq_ref, gl_ref, o0_ref, z_ref, nw_ref, o_ref, state_ref):
    batch, chunks = pq_ref.shape[0], pq_ref.shape[1]

    @pl.when(pl.program_id(0) == 0)
    def _():
        state_ref[...] = jnp.zeros_like(state_ref)

    for c in range(chunks):
        rows = slice(c * CHUNK, (c + 1) * CHUNK)
        for bi in range(batch):
            for h in range(GDN_HEADS):
                hs = slice(h * GDN_K_DIM, (h + 1) * GDN_K_DIM)
                state = state_ref[bi, h]
                r = _dot(pq_ref[bi, c, :PQ_ROWS, hs], state.astype(BF16))
                gl = gl_ref[bi, c, 0:1, hs]
                n_c = pq_ref[bi, c, PQ_ROWS:, hs].astype(F32)
                state_ref[bi, h] = state * gl - r[:GDN_K_DIM] + n_c
                o = r[GDN_K_DIM:] + o0_ref[bi, rows, hs].astype(F32)
                ms = jnp.mean(o * o, axis=-1, keepdims=True)
                y = o * lax.rsqrt(ms + EPS) * nw_ref[...]
                o_ref[bi, rows, hs] = (y * z_ref[bi, rows, hs].astype(F32)).astype(o_ref.dtype)


def _gdn_scan(pq, gl, o0, zb, norm_w, chunks):
    b, n_chunks = pq.shape[0], pq.shape[1]
    tr = chunks * CHUNK
    per_chunk = lambda i: (0, i, 0, 0)
    blk = lambda i: (0, i, 0)
    return pl.pallas_call(
        _gdn_scan_kernel,
        grid=(n_chunks // chunks,),
        in_specs=[
            pl.BlockSpec((b, chunks, PQN_ROWS, BRANCH), per_chunk),
            pl.BlockSpec((b, chunks, GL_ROWS, BRANCH), per_chunk),
            pl.BlockSpec((b, tr, BRANCH), blk),
            pl.BlockSpec((b, tr, BRANCH), blk),
            pl.BlockSpec((1, GDN_V_DIM), lambda i: (0, 0)),
        ],
        out_specs=pl.BlockSpec((b, tr, BRANCH), blk),
        out_shape=jax.ShapeDtypeStruct((b, n_chunks * CHUNK, BRANCH), BF16),
        scratch_shapes=[pltpu.VMEM((b, GDN_HEADS, GDN_K_DIM, GDN_V_DIM), F32)],
        compiler_params=pltpu.CompilerParams(
            dimension_semantics=("arbitrary",), vmem_limit_bytes=VMEM_LIMIT),
        name="gdn_scan",
    )(pq, gl, o0, zb, norm_w)


def _out_proj_kernel(x_ref, ya_ref, yb_ref, gate_ref, wa_ref, wb_ref, wo_ref, o_ref):
    d = x_ref.shape[-1]
    up_a = _dot(ya_ref[...], wa_ref[...])
    up_b = _dot(yb_ref[...], wb_ref[...])
    merged = (gate_ref[:, :d].astype(F32) * up_a + gate_ref[:, d:].astype(F32) * up_b)
    o_ref[...] = x_ref[...] + _dot(merged.astype(BF16), wo_ref[...])


def _out_proj(x2d, ya, yb, gates, w_up_a, w_up_b, w_out, tm):
    n, d = x2d.shape
    const = lambda i: (0, 0)
    row = lambda i: (i, 0)
    return pl.pallas_call(
        _out_proj_kernel,
        grid=(n // tm,),
        in_specs=[
            pl.BlockSpec((tm, d), row),
            pl.BlockSpec((tm, BRANCH), row),
            pl.BlockSpec((tm, BRANCH), row),
            pl.BlockSpec((tm, 2 * d), row),
            pl.BlockSpec(w_up_a.shape, const),
            pl.BlockSpec(w_up_b.shape, const),
            pl.BlockSpec(w_out.shape, const),
        ],
        out_specs=pl.BlockSpec((tm, d), row),
        out_shape=jax.ShapeDtypeStruct((n, d), x2d.dtype),
        compiler_params=pltpu.CompilerParams(
            dimension_semantics=("arbitrary",), vmem_limit_bytes=VMEM_LIMIT),
        name="out_proj",
    )(x2d, ya, yb, gates, w_up_a, w_up_b, w_out)


def _lambda_init_for(layer):
    return 0.8 - 0.6 * math.exp(-0.3 * layer)


def kernel(x, norm_w, w_in, da_q_gain, da_k_gain, da_lambda, da_out_gain, gdn_conv_w, gdn_a_log,
           gdn_dt_bias, gdn_norm_w, w_up_da, w_up_gdn, w_out):
    b, s, d = x.shape
    depth = norm_w.shape[0]
    n_main = 8 * BRANCH
    row_tile = 512
    attn_q_tile = 1024
    attn_k_tile = 256
    gdn_prep_tile = 512
    gdn_prep_group = 256
    gdn_scan_chunks = 4
    group =jnp.arange(BRANCH) // DA_HEAD_DIM
    gsum = (group[:, None] == group[None, :]).astype(BF16)
    groups_per_seg = BRANCH // DA_HEAD_DIM
    q_scale = DA_HEAD_DIM ** -0.5 * math.log2(math.e)

    x2d = x.reshape(b * s, d)
    for l in range(depth):
        n_blocked = w_in.shape[2] // BRANCH * BRANCH
        w_b, w_vt = _weight_prep(w_in[l], n_blocked)
        w_last = jnp.pad(w_in[l, :, n_blocked:],
                         ((0, 0), (0, -(w_in.shape[2] - n_blocked) % LANES))).astype(BF16)
        q_gain = jnp.tile(da_q_gain[l] * q_scale, groups_per_seg)[None, :]
        k_gain = jnp.tile(da_k_gain[l], groups_per_seg)[None, :]
        qa, ka, vat, za, qb, kb, vb, zb, ab, gates = _in_proj(
            x2d, norm_w[l][None, :], w_b, w_last, w_vt, q_gain, k_gain, gsum, row_tile)

        r3 = lambda t: t.reshape(b, s, t.shape[-1])
        ya = _diff_attn(r3(qa), r3(ka), vat, r3(za), da_lambda[l], da_out_gain[l][None, :],
                        _lambda_init_for(l), attn_q_tile, attn_k_tile)

        alog_pad = jnp.pad(gdn_a_log[l], (0, LANES - GDN_HEADS))[None, :]
        dtb_pad = jnp.pad(gdn_dt_bias[l], (0, LANES - GDN_HEADS))[None, :]
        pqn, gl, o0 = _gdn_prep(r3(qb), r3(kb), r3(vb), r3(ab), gdn_conv_w[l], alog_pad, dtb_pad,
                               gdn_prep_tile, gdn_prep_group)
        yb = _gdn_scan(pqn, gl, o0, r3(zb), gdn_norm_w[l][None, :], gdn_scan_chunks)

        x2d = _out_proj(x2d, ya.reshape(b * s, -1), yb.reshape(b * s, -1), gates,
                        w_up_da[l].astype(BF16), w_up_gdn[l].astype(BF16), w_out[l].astype(BF16),
                        row_tile)
    return x2d.reshape(b, s, d)
```

```python
import functools
import math

import jax
import jax.numpy as jnp
from jax import lax
from jax.experimental import pallas as pl
from jax.experimental.pallas import tpu as pltpu

F32 = jnp.float32
BF16 = jnp.bfloat16

CHUNK = 64
EPS = 1e-6
DA_HEADS = 4
DA_HEAD_DIM = 64
DA_V_DIM = 2 * DA_HEAD_DIM
GDN_HEADS = 4
GDN_K_DIM = 128
GDN_V_DIM = 128
CONV_WIDTH = 4
BRANCH = 512
LANES = 128
NEG_BIG = -1e30
ONES_ROWS = 16
UNROLL_BLOCKS = 2
VMEM_LIMIT = 56 * 1024 * 1024


def _dot(a, b, precision=None):
    return jnp.dot(a, b, preferred_element_type=F32, precision=precision)


def _dot_nt(a, b, precision=None):
    return lax.dot_general(a, b, (((1,), (1,)), ((), ())),
                           preferred_element_type=F32, precision=precision)


def _dot_tn(a, b, precision=None):
    return lax.dot_general(a, b, (((0,), (0,)), ((), ())),
                           preferred_element_type=F32, precision=precision)


def _sigmoid(x):
    return 1.0 / (1.0 + jnp.exp(-x))


def _silu(x):
    return x * _sigmoid(x)


def _weight_prep_kernel(w_ref, wb_ref):
    wb_ref[...] = w_ref[...].astype(BF16)


def _weight_prep(wt, n_rows):
    d = wt.shape[1]
    blk = pl.BlockSpec((BRANCH, d), lambda j: (j, 0))
    return pl.pallas_call(
        _weight_prep_kernel,
        grid=(n_rows // BRANCH,),
        in_specs=[blk],
        out_specs=blk,
        out_shape=jax.ShapeDtypeStruct((n_rows, d), BF16),
        compiler_params=pltpu.CompilerParams(
            dimension_semantics=("arbitrary",), vmem_limit_bytes=VMEM_LIMIT),
        name="weight_prep",
    )(wt)


def _in_proj_kernel(x_ref, nw_ref, wt_ref, qg_ref, kg_ref, gsum_ref,
                    qa_ref, ka_ref, vat_ref, za_ref, qb_ref, kb_ref, vb_ref, zb_ref, ab_ref):
    x = x_ref[...]
    ms = jnp.mean(x * x, axis=-1, keepdims=True)
    h = (x * lax.rsqrt(ms + EPS) * nw_ref[...]).astype(BF16)

    def proj(j):
        return _dot_nt(h, wt_ref[j * BRANCH:(j + 1) * BRANCH, :])

    def qk_norm(t, gain):
        ssq = _dot((t * t).astype(BF16), gsum_ref[...])
        return (t * lax.rsqrt(ssq * (1.0 / DA_HEAD_DIM) + EPS) * gain).astype(BF16)

    qa_ref[...] = qk_norm(proj(0), qg_ref[...])
    ka_ref[...] = qk_norm(proj(1), kg_ref[...])
    vat_ref[...] = _dot_nt(wt_ref[2 * BRANCH:3 * BRANCH, :], h).astype(BF16)
    za_ref[...] = _silu(proj(3)).astype(BF16)
    qb_ref[...] = proj(4).astype(BF16)
    kb_ref[...] = proj(5).astype(BF16)
    vb_ref[...] = proj(6).astype(BF16)
    zb_ref[...] = _silu(proj(7)).astype(BF16)
    ab_ref[...] = _dot_nt(h, wt_ref[8 * BRANCH:8 * BRANCH + LANES, :])


def _in_proj(x2d, norm_w, w_t, q_gain, k_gain, gsum, tm):
    n, d = x2d.shape
    const = lambda i: (0, 0)
    row = lambda i: (i, 0)
    seg = jax.ShapeDtypeStruct((n, BRANCH), BF16)
    seg_spec = pl.BlockSpec((tm, BRANCH), row)
    out_shape = [seg, seg, jax.ShapeDtypeStruct((BRANCH, n), BF16)] + [seg] * 5 + [
        jax.ShapeDtypeStruct((n, LANES), F32)]
    out_specs = ([seg_spec, seg_spec, pl.BlockSpec((BRANCH, tm), lambda i: (0, i))]
                 + [seg_spec] * 5
                 + [pl.BlockSpec((tm, LANES), row)])
    return pl.pallas_call(
        _in_proj_kernel,
        grid=(n // tm,),
        in_specs=[
            pl.BlockSpec((tm, d), row),
            pl.BlockSpec((1, d), const),
            pl.BlockSpec(w_t.shape, const, pipeline_mode=pl.Buffered(1)),
            pl.BlockSpec((1, BRANCH), const),
            pl.BlockSpec((1, BRANCH), const),
            pl.BlockSpec((BRANCH, BRANCH), const, pipeline_mode=pl.Buffered(1)),
        ],
        out_specs=out_specs,
        out_shape=out_shape,
        compiler_params=pltpu.CompilerParams(
            dimension_semantics=("arbitrary",), vmem_limit_bytes=VMEM_LIMIT),
        name="in_proj",
    )(x2d, norm_w, w_t, q_gain, k_gain, gsum)


def _diff_attn_kernel(lam_ref, q_ref, k_ref, vt_ref, z_ref, og_ref, o_ref, s_scr, m_scr, acc_scr,
                      *, tq, tk, lambda_init):
    i = pl.program_id(2)
    q = q_ref[...]
    lane = lax.broadcasted_iota(jnp.int32, q.shape, 1)
    zero = jnp.zeros_like(q)
    qq = jnp.concatenate([jnp.where(lane < DA_HEAD_DIM, q, zero),
                          jnp.where(lane >= DA_HEAD_DIM, q, zero)], axis=0)
    sub_tiles = s_scr.shape[0]
    ones_rows = jnp.ones((ONES_ROWS, tk), BF16)
    m_scr[...] = jnp.full(m_scr.shape, NEG_BIG, F32)
    acc_scr[...] = jnp.zeros(acc_scr.shape, F32)

    def issue(start, slot, q0=0):
        wq = tq - q0
        qsel = qq if q0 == 0 else jnp.concatenate([qq[q0:tq], qq[tq + q0:]], axis=0)
        s = _dot_nt(k_ref[pl.ds(start, tk), :], qsel)
        s_scr[slot, :, :2 * wq] = s
        return jnp.max(s, axis=0, keepdims=True)

    def consume(start, slot, tile_max, masked, q0=0):
        wq = tq - q0
        s = s_scr[slot, :, :2 * wq]
        if masked:
            k_chunk = (lax.broadcasted_iota(jnp.int32, s.shape, 0) + slot * tk) // CHUNK
            q_chunk = (lax.broadcasted_iota(jnp.int32, s.shape, 1) % wq + q0) // CHUNK
            valid = k_chunk <= q_chunk
            s = jnp.where(valid, s, NEG_BIG)
            tile_max = jnp.max(s, axis=0, keepdims=True)
        m_cols = [slice(g * tq + q0, (g + 1) * tq) for g in range(2)]
        m = jnp.concatenate([m_scr[:, mc] for mc in m_cols], axis=1)
        m_new = jnp.maximum(m, tile_max)
        alpha = jnp.exp2(m - m_new)
        p = jnp.exp2(s - m_new)
        if masked:
            p = jnp.where(valid, p, 0.0)
        pb = p.astype(BF16)
        vt = jnp.concatenate([vt_ref[:, pl.ds(start, tk)], ones_rows], axis=0)
        for g in range(2):
            cols = slice(g * wq, (g + 1) * wq)
            m_scr[:, m_cols[g]] = m_new[:, cols]
            acc_scr[g, :, q0:] = alpha[:, cols] * acc_scr[g, :, q0:] + _dot(vt, pb[:, cols])

    def make_body(blocks):
        def body(jj, tile_max):
            for u in range(blocks):
                for t in range(sub_tiles):
                    start = pl.multiple_of((jj * blocks + u) * tq + t * tk, tk)
                    next_max = issue(start + tk, (t + 1) % sub_tiles)
                    consume(start, t, tile_max, masked=False)
                    tile_max = next_max
            return tile_max
        return body

    paired = i // UNROLL_BLOCKS
    tile_max = lax.fori_loop(0, paired, make_body(UNROLL_BLOCKS), issue(0, 0))
    tile_max = lax.fori_loop(paired * UNROLL_BLOCKS, i, make_body(1), tile_max)
    for t in range(sub_tiles):
        start = pl.multiple_of(i * tq + t * tk, tk)
        if t + 1 < sub_tiles:
            issue(start + tk, t + 1, q0=(t + 1) * tk)
        consume(start, t, None, masked=True, q0=t * tk)
    acc1 = acc_scr[0]
    acc2 = acc_scr[1]

    lam_p = lam_ref[...]
    lam = (jnp.exp(jnp.sum(lam_p[0:1] * lam_p[1:2], axis=-1, keepdims=True))
           - jnp.exp(jnp.sum(lam_p[2:3] * lam_p[3:4], axis=-1, keepdims=True))
           + lambda_init)
    ot = (acc1[:DA_V_DIM] / acc1[DA_V_DIM:DA_V_DIM + 1]
          - lam * (acc2[:DA_V_DIM] / acc2[DA_V_DIM:DA_V_DIM + 1]))
    ms = jnp.mean(ot * ot, axis=0, keepdims=True)
    y = (ot * lax.rsqrt(ms + EPS)).T * og_ref[...] * (1.0 - lambda_init)
    o_ref[...] = (y * z_ref[...].astype(F32)).astype(o_ref.dtype)


def _diff_attn(qa, ka, vat, za, da_lambda, out_gain, lambda_init, tq, tk):
    b, s, _ = qa.shape
    sub_tiles = tq // tk
    assert sub_tiles >= 2 and sub_tiles * tk == tq
    blk = lambda bi, h, i: (bi, i, h)
    const = lambda bi, h, i: (0, 0)
    return pl.pallas_call(
        functools.partial(_diff_attn_kernel, tq=tq, tk=tk, lambda_init=lambda_init),
        scratch_shapes=[pltpu.VMEM((sub_tiles, tk, 2 * tq), F32),
                        pltpu.VMEM((1, 2 * tq), F32),
                        pltpu.VMEM((2, DA_V_DIM + ONES_ROWS, tq), F32)],
        grid=(b, DA_HEADS, s // tq),
        in_specs=[
            pl.BlockSpec(da_lambda.shape, const),
            pl.BlockSpec((None, tq, DA_V_DIM), blk),
            pl.BlockSpec((None, s, DA_V_DIM), lambda bi, h, i: (bi, 0, h)),
            pl.BlockSpec((DA_V_DIM, s), lambda bi, h, i: (h, bi)),
            pl.BlockSpec((None, tq, DA_V_DIM), blk),
            pl.BlockSpec((1, DA_V_DIM), const),
        ],
        out_specs=pl.BlockSpec((None, tq, DA_V_DIM), blk),
        out_shape=jax.ShapeDtypeStruct((b, s, DA_HEADS * DA_V_DIM), BF16),
        compiler_params=pltpu.CompilerParams(
            dimension_semantics=("arbitrary", "arbitrary", "arbitrary"),
            vmem_limit_bytes=VMEM_LIMIT),
        name="diff_attn",
    )(da_lambda, qa, ka, vat, za, out_gain)


WIDE = GDN_HEADS * CHUNK
HALO = 16
PQ_ROWS = GDN_K_DIM + CHUNK
PQN_ROWS = PQ_ROWS + GDN_K_DIM
GL_ROWS = 8


def _split2(x):
    hi = x.astype(BF16)
    lo = (x - hi.astype(F32)).astype(BF16)
    return hi, lo


def _block_diag(y, mask01):
    return jnp.concatenate([y] * GDN_HEADS, axis=0) * mask01


def _wide_mm(x, y, bd_mask):
    return _dot(x.astype(BF16), _block_diag(y.astype(BF16), bd_mask))


def _wide_unit_lower_inverse(a_list, eye, blk16, mm):
    d = [jnp.where(blk16, a, 0.0) for a in a_list]
    off = [a - di for a, di in zip(a_list, d)]
    p = [-di for di in d]
    t = [eye + pi for pi in p]
    for _ in range(3):
        p = [mm(pi, pi) for pi in p]
        yield
        t = [ti + mm(ti, pi) for ti, pi in zip(t, p)]
        yield
    e = [mm(ti, oi) for ti, oi in zip(t, off)]
    yield
    e2 = [mm(ei, ei) for ei in e]
    yield
    f = [eye - ei + e2i - mm(ei, e2i) for ei, e2i in zip(e, e2)]
    yield
    return [mm(fi, ti) for fi, ti in zip(f, t)]


def _run_interleaved(*gens):
    results = [None] * len(gens)
    alive = [g is not None for g in gens]
    while any(alive):
        for n, gen in enumerate(gens):
            if alive[n]:
                try:
                    next(gen)
                except StopIteration as stop:
                    results[n] = stop.value
                    alive[n] = False
    return results


def _head_l2norm(x):
    parts = []
    for h in range(GDN_HEADS):
        xh = x[:, h * GDN_K_DIM:(h + 1) * GDN_K_DIM]
        parts.append(xh * lax.rsqrt(jnp.sum(xh * xh, axis=-1, keepdims=True) + EPS))
    return jnp.concatenate(parts, axis=1)


def _head_bcast(col4, width):
    rows = col4.shape[0]
    return jnp.concatenate([jnp.broadcast_to(col4[:, h:h + 1], (rows, width))
                            for h in range(GDN_HEADS)], axis=1)


def _gdn_prep_kernel(q_ref, k_ref, v_ref, qh_ref, kh_ref, vh_ref, ab_ref, cw_ref, alog_ref, dtb_ref,
                     shift_ref, hsel_ref, pq_ref, gl_ref, o0_ref):
    tr = q_ref.shape[0]
    grp = shift_ref.shape[1]
    chunks = grp // CHUNK

    row_w = lax.broadcasted_iota(jnp.int32, (CHUNK, WIDE), 0)
    col_w = lax.broadcasted_iota(jnp.int32, (CHUNK, WIDE), 1) % CHUNK
    tri_incl = col_w <= row_w
    tri_strict = col_w < row_w
    eye_w = jnp.where(col_w == row_w, 1.0, 0.0).astype(F32)
    blk16 = (row_w // 16) == (col_w // 16)
    bd_mask = jnp.where(lax.broadcasted_iota(jnp.int32, (WIDE, WIDE), 0) // CHUNK
                        == lax.broadcasted_iota(jnp.int32, (WIDE, WIDE), 1) // CHUNK,
                        1.0, 0.0).astype(BF16)
    kbd_mask = jnp.where(lax.broadcasted_iota(jnp.int32, (WIDE, BRANCH), 0) // CHUNK
                         == lax.broadcasted_iota(jnp.int32, (WIDE, BRANCH), 1) // GDN_K_DIM,
                         1.0, 0.0).astype(BF16)
    r64 = lax.broadcasted_iota(jnp.int32, (CHUNK, CHUNK), 0)
    c64 = lax.broadcasted_iota(jnp.int32, (CHUNK, CHUNK), 1)
    lower_ones = jnp.where(c64 <= r64, 1.0, 0.0).astype(BF16)
    mm = functools.partial(_wide_mm, bd_mask=bd_mask)

    def conv_silu(x_ref, x_halo_ref, seg, gr, first_group):
        cols = slice(seg * BRANCH, (seg + 1) * BRANCH)
        cur = x_ref[gr, :]
        if first_group:
            halo = x_halo_ref[...]
            halo = jnp.where(pl.program_id(1) == 0, jnp.zeros((), halo.dtype), halo)
        else:
            halo = x_ref[gr.start - HALO:gr.start, :]
        conv = cur.astype(F32) * cw_ref[CONV_WIDTH - 1:CONV_WIDTH, cols]
        for s in range(1, CONV_WIDTH):
            shifted = _dot(shift_ref[s - 1], cur)
            head = shifted[:HALO] + _dot(hsel_ref[s - 1], halo)
            shifted = jnp.concatenate([head, shifted[HALO:]], axis=0)
            conv = conv + shifted * cw_ref[CONV_WIDTH - 1 - s:CONV_WIDTH - s, cols]
        return _silu(conv)

    def front(g):
        gr = slice(g * grp, (g + 1) * grp)
        q = _head_l2norm(conv_silu(q_ref, qh_ref, 0, gr, g == 0)) * (GDN_K_DIM ** -0.5)
        yield
        k = _head_l2norm(conv_silu(k_ref, kh_ref, 1, gr, g == 0))
        yield
        v = conv_silu(v_ref, vh_ref, 2, gr, g == 0)
        yield

        ab = ab_ref[gr, :]
        xg = ab + dtb_ref[...]
        softplus = jnp.maximum(xg, 0.0) + jnp.log1p(jnp.exp(-jnp.abs(xg)))
        g_all = -jnp.exp(alog_ref[...]) * softplus
        beta_all = _sigmoid(ab)

        staged = []
        for c in range(chunks):
            rows = slice(c * CHUNK, (c + 1) * CHUNK)
            qc, kc, vc = q[rows], k[rows], v[rows]
            g_c = g_all[rows]
            g_hi, g_lo = _split2(g_c)
            gc2 = _dot(lower_ones, jnp.concatenate([g_hi, g_lo], axis=1))
            gc_all = gc2[:, :LANES] + gc2[:, LANES:]
            gcb = _head_bcast(gc_all, GDN_K_DIM)
            betab = _head_bcast(beta_all[rows, GDN_HEADS:], GDN_K_DIM)
            exp_gc = jnp.exp(gcb)
            gl_row = exp_gc[CHUNK - 1:CHUNK]
            k_beta = kc * betab
            v_beta = vc * betab
            kbe = k_beta * exp_gc
            q_dec = qc * exp_gc
            k_tail = kc * jnp.exp(gcb[CHUNK - 1:CHUNK] - gcb)

            r_w = jnp.where(tri_strict, _head_bcast(g_c, CHUNK), 0.0)
            r_hi, r_lo = _split2(r_w)
            d2 = _dot(lower_ones, jnp.concatenate([r_hi, r_lo], axis=1))
            gamma = jnp.exp(jnp.where(tri_incl, d2[:, :WIDE] + d2[:, WIDE:], NEG_BIG))

            k16 = kc.astype(BF16)
            kbd = jnp.concatenate([k16] * GDN_HEADS, axis=0) * kbd_mask
            sc = _dot_nt(jnp.concatenate([k_beta.astype(BF16), qc.astype(BF16)], axis=0), kbd)
            a_strict = jnp.where(tri_strict, sc[:CHUNK] * gamma, 0.0)
            a_intra = jnp.where(tri_incl, sc[CHUNK:] * gamma, 0.0)
            staged.append((a_strict, a_intra, v_beta, kbe, q_dec, k_tail, gl_row))
            yield
        return staged

    def back(g, staged):
        t_inv = yield from _wide_unit_lower_inverse([st[0] for st in staged], eye_w, blk16, mm)
        pairs = [(c, h) for c in range(chunks) for h in range(GDN_HEADS)]
        hsl = lambda h: slice(h * GDN_K_DIM, (h + 1) * GDN_K_DIM)
        wsl = lambda h: slice(h * CHUNK, (h + 1) * CHUNK)
        uw16 = {}
        for c, h in pairs:
            _, _, v_beta, kbe, _, _, _ = staged[c]
            t_off = t_inv[c] - eye_w
            rhs = jnp.concatenate([v_beta[:, hsl(h)], kbe[:, hsl(h)]], axis=1)
            uw = rhs + _dot(t_off[:, wsl(h)].astype(BF16), rhs.astype(BF16))
            uw16[c, h] = uw.astype(BF16)
        yield
        for c, h in pairs:
            oc = g * chunks + c
            a_intra, q_dec = staged[c][1], staged[c][4]
            ai_uw = _dot(a_intra[:, wsl(h)].astype(BF16), uw16[c, h])
            pq_ref[oc, GDN_K_DIM:PQ_ROWS, hsl(h)] = (q_dec[:, hsl(h)]
                                                      - ai_uw[:, GDN_V_DIM:]).astype(pq_ref.dtype)
            o0_ref[oc * CHUNK:(oc + 1) * CHUNK, hsl(h)] = ai_uw[:, :GDN_V_DIM].astype(o0_ref.dtype)
        yield
        for c, h in pairs:
            oc = g * chunks + c
            k_tail = staged[c][5]
            kt_uw = _dot_tn(k_tail[:, hsl(h)].astype(BF16), uw16[c, h])
            pq_ref[oc, :GDN_K_DIM, hsl(h)] = kt_uw[:, GDN_V_DIM:].astype(pq_ref.dtype)
            pq_ref[oc, PQ_ROWS:, hsl(h)] = kt_uw[:, :GDN_V_DIM].astype(pq_ref.dtype)
        for c in range(chunks):
            oc = g * chunks + c
            gl_ref[oc] = jnp.broadcast_to(staged[c][6], (GL_ROWS, BRANCH))
        yield

    groups = tr // grp
    staged, = _run_interleaved(front(0))
    for g in range(groups):
        nxt = front(g + 1) if g + 1 < groups else None
        _, staged = _run_interleaved(back(g, staged), nxt)


def _gdn_prep(qb, kb, vb, ab, conv_w, alog_pad, dtb_pad, tr, grp):
    b, s, _ = qb.shape
    assert tr % grp == 0 and grp % CHUNK == 0
    chunks = tr // CHUNK
    blk = lambda bi, i: (bi, i, 0)
    halo = lambda bi, i: (bi, jnp.maximum(i * (tr // HALO) - 1, 0), 0)
    const = lambda bi, i: (0, 0)
    const3 = lambda bi, i: (0, 0, 0)
    per_chunk = lambda bi, i: (bi, i, 0, 0)
    t_idx = jnp.arange(grp)
    h_idx = jnp.arange(HALO)
    shift = jnp.stack([(t_idx[:, None] - sft == t_idx[None, :]) for sft in range(1, CONV_WIDTH)]).astype(BF16)
    hsel = jnp.stack([(h_idx[:, None] - sft + HALO == h_idx[None, :])
                      for sft in range(1, CONV_WIDTH)]).astype(BF16)
    return pl.pallas_call(
        _gdn_prep_kernel,
        grid=(b, s // tr),
        in_specs=[
            pl.BlockSpec((None, tr, BRANCH), blk),
            pl.BlockSpec((None, tr, BRANCH), blk),
            pl.BlockSpec((None, tr, BRANCH), blk),
            pl.BlockSpec((None, HALO, BRANCH), halo),
            pl.BlockSpec((None, HALO, BRANCH), halo),
            pl.BlockSpec((None, HALO, BRANCH), halo),
            pl.BlockSpec((None, tr, LANES), blk),
            pl.BlockSpec(conv_w.shape, const),
            pl.BlockSpec((1, LANES), const),
            pl.BlockSpec((1, LANES), const),
            pl.BlockSpec(shift.shape, const3),
            pl.BlockSpec(hsel.shape, const3),
        ],
        out_specs=[
            pl.BlockSpec((None, chunks, PQN_ROWS, BRANCH), per_chunk),
            pl.BlockSpec((None, chunks, GL_ROWS, BRANCH), per_chunk),
            pl.BlockSpec((None, tr, BRANCH), blk),
        ],
        out_shape=[
            jax.ShapeDtypeStruct((b, s // CHUNK, PQN_ROWS, BRANCH), BF16),
            jax.ShapeDtypeStruct((b, s // CHUNK, GL_ROWS, BRANCH), F32),
            jax.ShapeDtypeStruct((b, s, BRANCH), BF16),
        ],
        compiler_params=pltpu.CompilerParams(
            dimension_semantics=("arbitrary", "arbitrary"), vmem_limit_bytes=VMEM_LIMIT),
        name="gdn_prep",
    )(qb, kb, vb, qb, kb, vb, ab, conv_w, alog_pad, dtb_pad, shift, hsel)


def _gdn_scan_kernel(pq_ref, gl_ref, o0_ref, z_ref, nw_ref, o_ref, state_ref):
    batch, chunks = pq_ref.shape[0], pq_ref.shape[1]

    @pl.when(pl.program_id(0) == 0)
    def _():
        state_ref[...] = jnp.zeros_like(state_ref)

    for c in range(chunks):
        rows = slice(c * CHUNK, (c + 1) * CHUNK)
        for bi in range(batch):
            for h in range(GDN_HEADS):
                hs = slice(h * GDN_K_DIM, (h + 1) * GDN_K_DIM)
                state = state_ref[bi, h]
                r = _dot(pq_ref[bi, c, :PQ_ROWS, hs], state.astype(BF16))
                gl = gl_ref[bi, c, 0:1, hs]
                n_c = pq_ref[bi, c, PQ_ROWS:, hs].astype(F32)
                state_ref[bi, h] = state * gl - r[:GDN_K_DIM] + n_c
                o = r[GDN_K_DIM:] + o0_ref[bi, rows, hs].astype(F32)
                ms = jnp.mean(o * o, axis=-1, keepdims=True)
                y = o * lax.rsqrt(ms + EPS) * nw_ref[...]
                o_ref[bi, rows, hs] = (y * z_ref[bi, rows, hs].astype(F32)).astype(o_ref.dtype)


def _gdn_scan(pq, gl, o0, zb, norm_w, chunks):
    b, n_chunks = pq.shape[0], pq.shape[1]
    tr = chunks * CHUNK
    per_chunk = lambda i: (0, i, 0, 0)
    blk = lambda i: (0, i, 0)
    return pl.pallas_call(
        _gdn_scan_kernel,
        grid=(n_chunks // chunks,),
        in_specs=[
            pl.BlockSpec((b, chunks, PQN_ROWS, BRANCH), per_chunk),
            pl.BlockSpec((b, chunks, GL_ROWS, BRANCH), per_chunk),
            pl.BlockSpec((b, tr, BRANCH), blk),
            pl.BlockSpec((b, tr, BRANCH), blk),
            pl.BlockSpec((1, GDN_V_DIM), lambda i: (0, 0)),
        ],
        out_specs=pl.BlockSpec((b, tr, BRANCH), blk),
        out_shape=jax.ShapeDtypeStruct((b, n_chunks * CHUNK, BRANCH), BF16),
        scratch_shapes=[pltpu.VMEM((b, GDN_HEADS, GDN_K_DIM, GDN_V_DIM), F32)],
        compiler_params=pltpu.CompilerParams(
            dimension_semantics=("arbitrary",), vmem_limit_bytes=VMEM_LIMIT),
        name="gdn_scan",
    )(pq, gl, o0, zb, norm_w)


def _out_proj_kernel(x_ref, nw_ref, ya_ref, yb_ref, wg_ref, wlast_ref, wa_ref, wb_ref, wo_ref, o_ref):
    d = x_ref.shape[-1]
    x = x_ref[...]
    ms = jnp.mean(x * x, axis=-1, keepdims=True)
    h = (x * lax.rsqrt(ms + EPS) * nw_ref[...]).astype(BF16)
    tail = jnp.concatenate([_dot_nt(h, wg_ref[...]), _dot_nt(h, wlast_ref[...])], axis=1)
    n_ab = 2 * GDN_HEADS
    gates = _sigmoid(tail[:, n_ab:n_ab + 2 * d])
    up_a = _dot(ya_ref[...], wa_ref[...])
    up_b = _dot(yb_ref[...], wb_ref[...])
    merged = gates[:, :d] * up_a + gates[:, d:] * up_b
    o_ref[...] = x + _dot(merged.astype(BF16), wo_ref[...])


def _out_proj(x2d, norm_w, ya, yb, w_t, w_last, w_up_a, w_up_b, w_out, tm):
    n, d = x2d.shape
    const = lambda i: (0, 0)
    row = lambda i: (i, 0)
    gate_rows = 2 * d
    assert (8 * BRANCH) % gate_rows == 0 and w_t.shape[0] == 8 * BRANCH + gate_rows
    return pl.pallas_call(
        _out_proj_kernel,
        grid=(n // tm,),
        in_specs=[
            pl.BlockSpec((tm, d), row),
            pl.BlockSpec((1, d), const),
            pl.BlockSpec((tm, BRANCH), row),
            pl.BlockSpec((tm, BRANCH), row),
            pl.BlockSpec((gate_rows, d), lambda i: (8 * BRANCH // gate_rows, 0)),
            pl.BlockSpec(w_last.shape, const),
            pl.BlockSpec(w_up_a.shape, const),
            pl.BlockSpec(w_up_b.shape, const),
            pl.BlockSpec(w_out.shape, const),
        ],
        out_specs=pl.BlockSpec((tm, d), row),
        out_shape=jax.ShapeDtypeStruct((n, d), x2d.dtype),
        compiler_params=pltpu.CompilerParams(
            dimension_semantics=("arbitrary",), vmem_limit_bytes=VMEM_LIMIT),
        name="out_proj",
    )(x2d, norm_w, ya, yb, w_t, w_last, w_up_a, w_up_b, w_out)


def _lambda_init_for(layer):
    return 0.8 - 0.6 * math.exp(-0.3 * layer)


def kernel(x, norm_w, w_in, da_q_gain, da_k_gain, da_lambda, da_out_gain, gdn_conv_w, gdn_a_log,
           gdn_dt_bias, gdn_norm_w, w_up_da, w_up_gdn, w_out):
    b, s, d = x.shape
    depth = norm_w.shape[0]
    row_tile = 512
    attn_q_tile = 1024
    attn_k_tile = 256
    gdn_prep_tile = 512
    gdn_prep_group = 256
    gdn_scan_chunks = 4
    group = jnp.arange(BRANCH) // DA_HEAD_DIM
    gsum = (group[:, None] == group[None, :]).astype(BF16)
    groups_per_seg = BRANCH // DA_HEAD_DIM
    q_scale = DA_HEAD_DIM ** -0.5 * math.log2(math.e)

    x2d = x.reshape(b * s, d)
    for l in range(depth):
        w_feat = w_in[l].T
        n_blocked = w_feat.shape[0] // BRANCH * BRANCH
        w_t = _weight_prep(w_feat, n_blocked)
        w_last = jnp.pad(w_feat[n_blocked:], ((0, -(w_feat.shape[0] - n_blocked) % LANES), (0, 0))).astype(BF16)
        q_gain = jnp.tile(da_q_gain[l] * q_scale, groups_per_seg)[None, :]
        k_gain = jnp.tile(da_k_gain[l], groups_per_seg)[None, :]
        qa, ka, vat, za, qb, kb, vb, zb, ab = _in_proj(
            x2d, norm_w[l][None, :], w_t, q_gain, k_gain, gsum, row_tile)

        r3 = lambda t: t.reshape(b, s, t.shape[-1])
        ya = _diff_attn(r3(qa), r3(ka), vat, r3(za), da_lambda[l], da_out_gain[l][None, :],
                        _lambda_init_for(l), attn_q_tile, attn_k_tile)

        alog_pad = jnp.pad(gdn_a_log[l], (0, LANES - GDN_HEADS))[None, :]
        dtb_pad = jnp.pad(gdn_dt_bias[l], (0, LANES - GDN_HEADS))[None, :]
        pqn, gl, o0 = _gdn_prep(r3(qb), r3(kb), r3(vb), r3(ab), gdn_conv_w[l], alog_pad, dtb_pad,
                               gdn_prep_tile, gdn_prep_group)
        yb = _gdn_scan(pqn, gl, o0, r3(zb), gdn_norm_w[l][None, :], gdn_scan_chunks)

        x2d = _out_proj(x2d, norm_w[l][None, :], ya.reshape(b * s, -1), yb.reshape(b * s, -1), w_t, w_last,
                        w_up_da[l].astype(BF16), w_up_gdn[l].astype(BF16), w_out[l].astype(BF16),
                        row_tile)
    return x2d.reshape(b, s, d)
```

```python
import functools
import math

import jax
import jax.numpy as jnp
from jax import lax
from jax.experimental import pallas as pl
from jax.experimental.pallas import tpu as pltpu

F32 = jnp.float32
BF16 = jnp.bfloat16

CHUNK = 64
EPS = 1e-6
DA_HEADS = 4
DA_HEAD_DIM = 64
DA_V_DIM = 2 * DA_HEAD_DIM
GDN_HEADS = 4
GDN_K_DIM = 128
GDN_V_DIM = 128
CONV_WIDTH = 4
BRANCH = 512
LANES = 128
NEG_BIG = -1e30
ONES_ROWS = 16
UNROLL_BLOCKS = 2
VMEM_LIMIT = 56 * 1024 * 1024


def _dot(a, b, precision=None):
    return jnp.dot(a, b, preferred_element_type=F32, precision=precision)


def _dot_nt(a, b, precision=None):
    return lax.dot_general(a, b, (((1,), (1,)), ((), ())),
                           preferred_element_type=F32, precision=precision)


def _dot_tn(a, b, precision=None):
    return lax.dot_general(a, b, (((0,), (0,)), ((), ())),
                           preferred_element_type=F32, precision=precision)


def _sigmoid(x):
    return 1.0 / (1.0 + jnp.exp(-x))


def _silu(x):
    return x * _sigmoid(x)


def _weight_prep_kernel(w_ref, wb_ref):
    wb_ref[...] = w_ref[...].astype(BF16)


def _weight_prep(wt, n_rows):
    d = wt.shape[1]
    blk = pl.BlockSpec((BRANCH, d), lambda j: (j, 0))
    return pl.pallas_call(
        _weight_prep_kernel,
        grid=(n_rows // BRANCH,),
        in_specs=[blk],
        out_specs=blk,
        out_shape=jax.ShapeDtypeStruct((n_rows, d), BF16),
        compiler_params=pltpu.CompilerParams(
            dimension_semantics=("arbitrary",), vmem_limit_bytes=VMEM_LIMIT),
        name="weight_prep",
    )(wt)


def _in_proj_kernel(x_ref, nw_ref, wt_ref, wlast_ref, qg_ref, kg_ref, gsum_ref,
                    qa_ref, ka_ref, vat_ref, za_ref, qb_ref, kb_ref, vb_ref, zb_ref, ab_ref, gate_ref):
    x = x_ref[...]
    ms = jnp.mean(x * x, axis=-1, keepdims=True)
    h = (x * lax.rsqrt(ms + EPS) * nw_ref[...]).astype(BF16)

    def proj(j):
        return _dot_nt(h, wt_ref[j * BRANCH:(j + 1) * BRANCH, :])

    def qk_norm(t, gain):
        ssq = _dot((t * t).astype(BF16), gsum_ref[...])
        return (t * lax.rsqrt(ssq * (1.0 / DA_HEAD_DIM) + EPS) * gain).astype(BF16)

    qa_ref[...] = qk_norm(proj(0), qg_ref[...])
    ka_ref[...] = qk_norm(proj(1), kg_ref[...])
    vat_ref[...] = _dot_nt(wt_ref[2 * BRANCH:3 * BRANCH, :], h).astype(BF16)
    za_ref[...] = _silu(proj(3)).astype(BF16)
    qb_ref[...] = proj(4).astype(BF16)
    kb_ref[...] = proj(5).astype(BF16)
    vb_ref[...] = proj(6).astype(BF16)
    zb_ref[...] = _silu(proj(7)).astype(BF16)
    tail = jnp.concatenate([_dot_nt(h, wt_ref[8 * BRANCH:, :]), _dot_nt(h, wlast_ref[...])], axis=1)
    ab_ref[...] = tail[:, :LANES]
    n_ab = 2 * GDN_HEADS
    gate_ref[...] = _sigmoid(tail[:, n_ab:n_ab + gate_ref.shape[1]]).astype(BF16)


def _in_proj(x2d, norm_w, w_t, w_last, q_gain, k_gain, gsum, tm):
    n, d = x2d.shape
    const = lambda i: (0, 0)
    row = lambda i: (i, 0)
    seg = jax.ShapeDtypeStruct((n, BRANCH), BF16)
    seg_spec = pl.BlockSpec((tm, BRANCH), row)
    out_shape = [seg, seg, jax.ShapeDtypeStruct((BRANCH, n), BF16)] + [seg] * 5 + [
        jax.ShapeDtypeStruct((n, LANES), F32), jax.ShapeDtypeStruct((n, 2 * d), BF16)]
    out_specs = ([seg_spec, seg_spec, pl.BlockSpec((BRANCH, tm), lambda i: (0, i))]
                 + [seg_spec] * 5
                 + [pl.BlockSpec((tm, LANES), row), pl.BlockSpec((tm, 2 * d), row)])
    return pl.pallas_call(
        _in_proj_kernel,
        grid=(n // tm,),
        in_specs=[
            pl.BlockSpec((tm, d), row),
            pl.BlockSpec((1, d), const),
            pl.BlockSpec(w_t.shape, const, pipeline_mode=pl.Buffered(1)),
            pl.BlockSpec(w_last.shape, const, pipeline_mode=pl.Buffered(1)),
            pl.BlockSpec((1, BRANCH), const),
            pl.BlockSpec((1, BRANCH), const),
            pl.BlockSpec((BRANCH, BRANCH), const, pipeline_mode=pl.Buffered(1)),
        ],
        out_specs=out_specs,
        out_shape=out_shape,
        compiler_params=pltpu.CompilerParams(
            dimension_semantics=("arbitrary",), vmem_limit_bytes=VMEM_LIMIT),
        name="in_proj",
    )(x2d, norm_w, w_t, w_last, q_gain, k_gain, gsum)


def _diff_attn_kernel(lam_ref, bias_ref, q_ref, k_ref, vt_ref, z_ref, og_ref, o_ref, s_scr, m_scr, acc_scr,
                      *, tq, tk, lambda_init):
    i = pl.program_id(2)
    q = q_ref[...]
    lane = lax.broadcasted_iota(jnp.int32, q.shape, 1)
    zero = jnp.zeros_like(q)
    qq = jnp.concatenate([jnp.where(lane < DA_HEAD_DIM, q, zero),
                          jnp.where(lane >= DA_HEAD_DIM, q, zero)], axis=0)
    sub_tiles = s_scr.shape[0]
    ones_rows = jnp.ones((ONES_ROWS, tk), BF16)
    m_scr[...] = jnp.full(m_scr.shape, NEG_BIG, F32)
    acc_scr[...] = jnp.zeros(acc_scr.shape, F32)

    def issue(start, slot, q0=0):
        wq = tq - q0
        qsel = qq if q0 == 0 else jnp.concatenate([qq[q0:tq], qq[tq + q0:]], axis=0)
        s = _dot_nt(k_ref[pl.ds(start, tk), :], qsel)
        s_scr[slot, :, :2 * wq] = s
        return jnp.max(s, axis=0, keepdims=True)

    def consume(start, slot, tile_max, masked, q0=0):
        wq = tq - q0
        s = s_scr[slot, :, :2 * wq]
        if masked:
            bias = bias_ref[...]
            pieces = []
            for g in range(2):
                pieces.append(s[:, g * wq:g * wq + tk] + bias)
                if wq > tk:
                    pieces.append(s[:, g * wq + tk:(g + 1) * wq])
            s = jnp.concatenate(pieces, axis=1)
            tile_max = jnp.max(s, axis=0, keepdims=True)
        m_cols = [slice(g * tq + q0, (g + 1) * tq) for g in range(2)]
        m = jnp.concatenate([m_scr[:, mc] for mc in m_cols], axis=1)
        m_new = jnp.maximum(m, tile_max)
        alpha = jnp.exp2(m - m_new)
        p = jnp.exp2(s - m_new)
        pb = p.astype(BF16)
        vt = jnp.concatenate([vt_ref[:, pl.ds(start, tk)], ones_rows], axis=0)
        for g in range(2):
            cols = slice(g * wq, (g + 1) * wq)
            m_scr[:, m_cols[g]] = m_new[:, cols]
            acc_scr[g, :, q0:] = alpha[:, cols] * acc_scr[g, :, q0:] + _dot(vt, pb[:, cols])

    def make_body(blocks):
        def body(jj, tile_max):
            for u in range(blocks):
                for t in range(sub_tiles):
                    start = pl.multiple_of((jj * blocks + u) * tq + t * tk, tk)
                    next_max = issue(start + tk, (t + 1) % sub_tiles)
                    consume(start, t, tile_max, masked=False)
                    tile_max = next_max
            return tile_max
        return body

    paired = i // UNROLL_BLOCKS
    tile_max = lax.fori_loop(0, paired, make_body(UNROLL_BLOCKS), issue(0, 0))
    tile_max = lax.fori_loop(paired * UNROLL_BLOCKS, i, make_body(1), tile_max)
    for t in range(sub_tiles):
        start = pl.multiple_of(i * tq + t * tk, tk)
        if t + 1 < sub_tiles:
            issue(start + tk, t + 1, q0=(t + 1) * tk)
        consume(start, t, None, masked=True, q0=t * tk)
    acc1 = acc_scr[0]
    acc2 = acc_scr[1]

    lam_p = lam_ref[...]
    lam = (jnp.exp(jnp.sum(lam_p[0:1] * lam_p[1:2], axis=-1, keepdims=True))
           - jnp.exp(jnp.sum(lam_p[2:3] * lam_p[3:4], axis=-1, keepdims=True))
           + lambda_init)
    ot = (acc1[:DA_V_DIM] / acc1[DA_V_DIM:DA_V_DIM + 1]
          - lam * (acc2[:DA_V_DIM] / acc2[DA_V_DIM:DA_V_DIM + 1]))
    ms = jnp.mean(ot * ot, axis=0, keepdims=True)
    y = (ot * lax.rsqrt(ms + EPS)).T * og_ref[...] * (1.0 - lambda_init)
    o_ref[...] = (y * z_ref[...].astype(F32)).astype(o_ref.dtype)


def _diff_attn(qa, ka, vat, za, da_lambda, out_gain, lambda_init, tq, tk):
    b, s, _ = qa.shape
    sub_tiles = tq // tk
    assert sub_tiles >= 2 and sub_tiles * tk == tq
    blk = lambda bi, h, i: (bi, i, h)
    const = lambda bi, h, i: (0, 0)
    chunk_of = jnp.arange(tk) // CHUNK
    bias = jnp.where(chunk_of[:, None] <= chunk_of[None, :], 0.0, NEG_BIG).astype(F32)
    return pl.pallas_call(
        functools.partial(_diff_attn_kernel, tq=tq, tk=tk, lambda_init=lambda_init),
        scratch_shapes=[pltpu.VMEM((sub_tiles, tk, 2 * tq), F32),
                        pltpu.VMEM((1, 2 * tq), F32),
                        pltpu.VMEM((2, DA_V_DIM + ONES_ROWS, tq), F32)],
        grid=(b, DA_HEADS, s // tq),
        in_specs=[
            pl.BlockSpec(da_lambda.shape, const),
            pl.BlockSpec(bias.shape, const),
            pl.BlockSpec((None, tq, DA_V_DIM), blk),
            pl.BlockSpec((None, s, DA_V_DIM), lambda bi, h, i: (bi, 0, h)),
            pl.BlockSpec((DA_V_DIM, s), lambda bi, h, i: (h, bi)),
            pl.BlockSpec((None, tq, DA_V_DIM), blk),
            pl.BlockSpec((1, DA_V_DIM), const),
        ],
        out_specs=pl.BlockSpec((None, tq, DA_V_DIM), blk),
        out_shape=jax.ShapeDtypeStruct((b, s, DA_HEADS * DA_V_DIM), BF16),
        compiler_params=pltpu.CompilerParams(
            dimension_semantics=("arbitrary", "arbitrary", "arbitrary"),
            vmem_limit_bytes=VMEM_LIMIT),
        name="diff_attn",
    )(da_lambda, bias, qa, ka, vat, za, out_gain)


WIDE = GDN_HEADS * CHUNK
HALO = 16
PQ_ROWS = GDN_K_DIM + CHUNK
PQN_ROWS = PQ_ROWS + GDN_K_DIM
GL_ROWS = 8


def _split2(x):
    hi = x.astype(BF16)
    lo = (x - hi.astype(F32)).astype(BF16)
    return hi, lo


def _block_diag(y, mask01):
    return jnp.concatenate([y] * GDN_HEADS, axis=0) * mask01


def _wide_mm(x, y, bd_mask):
    return _dot(x.astype(BF16), _block_diag(y.astype(BF16), bd_mask))


def _wide_unit_lower_inverse(a_list, eye, blk16, mm):
    d = [jnp.where(blk16, a, 0.0) for a in a_list]
    off = [a - di for a, di in zip(a_list, d)]
    p = [-di for di in d]
    t = [eye + pi for pi in p]
    for _ in range(3):
        p = [mm(pi, pi) for pi in p]
        yield
        t = [ti + mm(ti, pi) for ti, pi in zip(t, p)]
        yield
    e = [mm(ti, oi) for ti, oi in zip(t, off)]
    yield
    e2 = [mm(ei, ei) for ei in e]
    yield
    f = [eye - ei + e2i - mm(ei, e2i) for ei, e2i in zip(e, e2)]
    yield
    return [mm(fi, ti) for fi, ti in zip(f, t)]


def _run_interleaved(*gens):
    results = [None] * len(gens)
    alive = [g is not None for g in gens]
    while any(alive):
        for n, gen in enumerate(gens):
            if alive[n]:
                try:
                    next(gen)
                except StopIteration as stop:
                    results[n] = stop.value
                    alive[n] = False
    return results


def _head_l2norm(x):
    parts = []
    for h in range(GDN_HEADS):
        xh = x[:, h * GDN_K_DIM:(h + 1) * GDN_K_DIM]
        parts.append(xh * lax.rsqrt(jnp.sum(xh * xh, axis=-1, keepdims=True) + EPS))
    return jnp.concatenate(parts, axis=1)


def _head_bcast(col4, width):
    rows = col4.shape[0]
    return jnp.concatenate([jnp.broadcast_to(col4[:, h:h + 1], (rows, width))
                            for h in range(GDN_HEADS)], axis=1)


def _gdn_prep_kernel(q_ref, k_ref, v_ref, qh_ref, kh_ref, vh_ref, ab_ref, cw_ref, alog_ref, dtb_ref,
                     shift_ref, hsel_ref, pq_ref, gl_ref, o0_ref):
    tr = q_ref.shape[0]
    grp = shift_ref.shape[1]
    chunks = grp // CHUNK

    row_w = lax.broadcasted_iota(jnp.int32, (CHUNK, WIDE), 0)
    col_w = lax.broadcasted_iota(jnp.int32, (CHUNK, WIDE), 1) % CHUNK
    tri_incl = col_w <= row_w
    tri_strict = col_w < row_w
    eye_w = jnp.where(col_w == row_w, 1.0, 0.0).astype(F32)
    blk16 = (row_w // 16) == (col_w // 16)
    bd_mask = jnp.where(lax.broadcasted_iota(jnp.int32, (WIDE, WIDE), 0) // CHUNK
                        == lax.broadcasted_iota(jnp.int32, (WIDE, WIDE), 1) // CHUNK,
                        1.0, 0.0).astype(BF16)
    kbd_mask = jnp.where(lax.broadcasted_iota(jnp.int32, (WIDE, BRANCH), 0) // CHUNK
                         == lax.broadcasted_iota(jnp.int32, (WIDE, BRANCH), 1) // GDN_K_DIM,
                         1.0, 0.0).astype(BF16)
    r64 = lax.broadcasted_iota(jnp.int32, (CHUNK, CHUNK), 0)
    c64 = lax.broadcasted_iota(jnp.int32, (CHUNK, CHUNK), 1)
    lower_ones = jnp.where(c64 <= r64, 1.0, 0.0).astype(BF16)
    mm = functools.partial(_wide_mm, bd_mask=bd_mask)

    def conv_silu(x_ref, x_halo_ref, seg, gr, first_group):
        cols = slice(seg * BRANCH, (seg + 1) * BRANCH)
        cur = x_ref[gr, :]
        if first_group:
            halo = x_halo_ref[...]
            halo = jnp.where(pl.program_id(1) == 0, jnp.zeros((), halo.dtype), halo)
        else:
            halo = x_ref[gr.start - HALO:gr.start, :]
        conv = cur.astype(F32) * cw_ref[CONV_WIDTH - 1:CONV_WIDTH, cols]
        for s in range(1, CONV_WIDTH):
            shifted = _dot(shift_ref[s - 1], cur)
            head = shifted[:HALO] + _dot(hsel_ref[s - 1], halo)
            shifted = jnp.concatenate([head, shifted[HALO:]], axis=0)
            conv = conv + shifted * cw_ref[CONV_WIDTH - 1 - s:CONV_WIDTH - s, cols]
        return _silu(conv)

    def front(g):
        gr = slice(g * grp, (g + 1) * grp)
        q = _head_l2norm(conv_silu(q_ref, qh_ref, 0, gr, g == 0)) * (GDN_K_DIM ** -0.5)
        yield
        k = _head_l2norm(conv_silu(k_ref, kh_ref, 1, gr, g == 0))
        yield
        v = conv_silu(v_ref, vh_ref, 2, gr, g == 0)
        yield

        ab = ab_ref[gr, :]
        xg = ab + dtb_ref[...]
        softplus = jnp.maximum(xg, 0.0) + jnp.log1p(jnp.exp(-jnp.abs(xg)))
        g_all = -jnp.exp(alog_ref[...]) * softplus
        beta_all = _sigmoid(ab)

        staged = []
        for c in range(chunks):
            rows = slice(c * CHUNK, (c + 1) * CHUNK)
            qc, kc, vc = q[rows], k[rows], v[rows]
            g_c = g_all[rows]
            g_hi, g_lo = _split2(g_c)
            gc2 = _dot(lower_ones, jnp.concatenate([g_hi, g_lo], axis=1))
            gc_all = gc2[:, :LANES] + gc2[:, LANES:]
            gcb = _head_bcast(gc_all, GDN_K_DIM)
            betab = _head_bcast(beta_all[rows, GDN_HEADS:], GDN_K_DIM)
            exp_gc = jnp.exp(gcb)
            gl_row = exp_gc[CHUNK - 1:CHUNK]
            k_beta = kc * betab
            v_beta = vc * betab
            kbe = k_beta * exp_gc
            q_dec = qc * exp_gc
            k_tail = kc * jnp.exp(gcb[CHUNK - 1:CHUNK] - gcb)

            r_w = jnp.where(tri_strict, _head_bcast(g_c, CHUNK), 0.0)
            r_hi, r_lo = _split2(r_w)
            d2 = _dot(lower_ones, jnp.concatenate([r_hi, r_lo], axis=1))
            gamma = jnp.exp(jnp.where(tri_incl, d2[:, :WIDE] + d2[:, WIDE:], NEG_BIG))

            k16 = kc.astype(BF16)
            kbd = jnp.concatenate([k16] * GDN_HEADS, axis=0) * kbd_mask
            sc = _dot_nt(jnp.concatenate([k_beta.astype(BF16), qc.astype(BF16)], axis=0), kbd)
            a_strict = jnp.where(tri_strict, sc[:CHUNK] * gamma, 0.0)
            a_intra = jnp.where(tri_incl, sc[CHUNK:] * gamma, 0.0)
            staged.append((a_strict, a_intra, v_beta, kbe, q_dec, k_tail, gl_row))
            yield
        return staged

    def back(g, staged):
        t_inv = yield from _wide_unit_lower_inverse([st[0] for st in staged], eye_w, blk16, mm)
        pairs = [(c, h) for c in range(chunks) for h in range(GDN_HEADS)]
        hsl = lambda h: slice(h * GDN_K_DIM, (h + 1) * GDN_K_DIM)
        wsl = lambda h: slice(h * CHUNK, (h + 1) * CHUNK)
        uw16 = {}
        for c, h in pairs:
            _, _, v_beta, kbe, _, _, _ = staged[c]
            t_off = t_inv[c] - eye_w
            rhs = jnp.concatenate([v_beta[:, hsl(h)], kbe[:, hsl(h)]], axis=1)
            uw = rhs + _dot(t_off[:, wsl(h)].astype(BF16), rhs.astype(BF16))
            uw16[c, h] = uw.astype(BF16)
        yield
        for c, h in pairs:
            oc = g * chunks + c
            a_intra, q_dec = staged[c][1], staged[c][4]
            ai_uw = _dot(a_intra[:, wsl(h)].astype(BF16), uw16[c, h])
            pq_ref[oc, GDN_K_DIM:PQ_ROWS, hsl(h)] = (q_dec[:, hsl(h)]
                                                      - ai_uw[:, GDN_V_DIM:]).astype(pq_ref.dtype)
            o0_ref[oc * CHUNK:(oc + 1) * CHUNK, hsl(h)] = ai_uw[:, :GDN_V_DIM].astype(o0_ref.dtype)
        yield
        for c, h in pairs:
            oc = g * chunks + c
            k_tail = staged[c][5]
            kt_uw = _dot_tn(k_tail[:, hsl(h)].astype(BF16), uw16[c, h])
            pq_ref[oc, :GDN_K_DIM, hsl(h)] = kt_uw[:, GDN_V_DIM:].astype(pq_ref.dtype)
            pq_ref[oc, PQ_ROWS:, hsl(h)] = kt_uw[:, :GDN_V_DIM].astype(pq_ref.dtype)
        for c in range(chunks):
            oc = g * chunks + c
            gl_ref[oc] = jnp.broadcast_to(staged[c][6], (GL_ROWS, BRANCH))
        yield

    groups = tr // grp
    staged, = _run_interleaved(front(0))
    for g in range(groups):
        nxt = front(g + 1) if g + 1 < groups else None
        _, staged = _run_interleaved(back(g, staged), nxt)


def _gdn_prep(qb, kb, vb, ab, conv_w, alog_pad, dtb_pad, tr, grp):
    b, s, _ = qb.shape
    assert tr % grp == 0 and grp % CHUNK == 0
    chunks = tr // CHUNK
    blk = lambda bi, i: (bi, i, 0)
    halo = lambda bi, i: (bi, jnp.maximum(i * (tr // HALO) - 1, 0), 0)
    const = lambda bi, i: (0, 0)
    const3 = lambda bi, i: (0, 0, 0)
    per_chunk = lambda bi, i: (bi, i, 0, 0)
    t_idx = jnp.arange(grp)
    h_idx = jnp.arange(HALO)
    shift = jnp.stack([(t_idx[:, None] - sft == t_idx[None, :]) for sft in range(1, CONV_WIDTH)]).astype(BF16)
    hsel = jnp.stack([(h_idx[:, None] - sft + HALO == h_idx[None, :])
                      for sft in range(1, CONV_WIDTH)]).astype(BF16)
    return pl.pallas_call(
        _gdn_prep_kernel,
        grid=(b, s // tr),
        in_specs=[
            pl.BlockSpec((None, tr, BRANCH), blk),
            pl.BlockSpec((None, tr, BRANCH), blk),
            pl.BlockSpec((None, tr, BRANCH), blk),
            pl.BlockSpec((None, HALO, BRANCH), halo),
            pl.BlockSpec((None, HALO, BRANCH), halo),
            pl.BlockSpec((None, HALO, BRANCH), halo),
            pl.BlockSpec((None, tr, LANES), blk),
            pl.BlockSpec(conv_w.shape, const),
            pl.BlockSpec((1, LANES), const),
            pl.BlockSpec((1, LANES), const),
            pl.BlockSpec(shift.shape, const3),
            pl.BlockSpec(hsel.shape, const3),
        ],
        out_specs=[
            pl.BlockSpec((None, chunks, PQN_ROWS, BRANCH), per_chunk),
            pl.BlockSpec((None, chunks, GL_ROWS, BRANCH), per_chunk),
            pl.BlockSpec((None, tr, BRANCH), blk),
        ],
        out_shape=[
            jax.ShapeDtypeStruct((b, s // CHUNK, PQN_ROWS, BRANCH), BF16),
            jax.ShapeDtypeStruct((b, s // CHUNK, GL_ROWS, BRANCH), F32),
            jax.ShapeDtypeStruct((b, s, BRANCH), BF16),
        ],
        compiler_params=pltpu.CompilerParams(
            dimension_semantics=("arbitrary", "arbitrary"), vmem_limit_bytes=VMEM_LIMIT),
        name="gdn_prep",
    )(qb, kb, vb, qb, kb, vb, ab, conv_w, alog_pad, dtb_pad, shift, hsel)


def _gdn_scan_kernel(pq_ref, gl_ref, o0_ref, z_ref, nw_ref, o_ref, state_ref):
    batch, chunks = pq_ref.shape[0], pq_ref.shape[1]

    @pl.when(pl.program_id(0) == 0)
    def _():
        state_ref[...] = jnp.zeros_like(state_ref)

    for c in range(chunks):
        rows = slice(c * CHUNK, (c + 1) * CHUNK)
        for bi in range(batch):
            for h in range(GDN_HEADS):
                hs = slice(h * GDN_K_DIM, (h + 1) * GDN_K_DIM)
                state = state_ref[bi, h]
                r = _dot(pq_ref[bi, c, :PQ_ROWS, hs], state.astype(BF16))
                gl = gl_ref[bi, c, 0:1, hs]
                n_c = pq_ref[bi, c, PQ_ROWS:, hs].astype(F32)
                state_ref[bi, h] = state * gl - r[:GDN_K_DIM] + n_c
                o = r[GDN_K_DIM:] + o0_ref[bi, rows, hs].astype(F32)
                ms = jnp.mean(o * o, axis=-1, keepdims=True)
                y = o * lax.rsqrt(ms + EPS) * nw_ref[...]
                o_ref[bi, rows, hs] = (y * z_ref[bi, rows, hs].astype(F32)).astype(o_ref.dtype)


def _gdn_scan(pq, gl, o0, zb, norm_w, chunks):
    b, n_chunks = pq.shape[0], pq.shape[1]
    tr = chunks * CHUNK
    per_chunk = lambda i: (0, i, 0, 0)
    blk = lambda i: (0, i, 0)
    return pl.pallas_call(
        _gdn_scan_kernel,
        grid=(n_chunks // chunks,),
        in_specs=[
            pl.BlockSpec((b, chunks, PQN_ROWS, BRANCH), per_chunk),
            pl.BlockSpec((b, chunks, GL_ROWS, BRANCH), per_chunk),
            pl.BlockSpec((b, tr, BRANCH), blk),
            pl.BlockSpec((b, tr, BRANCH), blk),
            pl.BlockSpec((1, GDN_V_DIM), lambda i: (0, 0)),
        ],
        out_specs=pl.BlockSpec((b, tr, BRANCH), blk),
        out_shape=jax.ShapeDtypeStruct((b, n_chunks * CHUNK, BRANCH), BF16),
        scratch_shapes=[pltpu.VMEM((b, GDN_HEADS, GDN_K_DIM, GDN_V_DIM), F32)],
        compiler_params=pltpu.CompilerParams(
            dimension_semantics=("arbitrary",), vmem_limit_bytes=VMEM_LIMIT),
        name="gdn_scan",
    )(pq, gl, o0, zb, norm_w)


def _out_proj_kernel(x_ref, ya_ref, yb_ref, gate_ref, wa_ref, wb_ref, wo_ref, o_ref):
    d = x_ref.shape[-1]
    up_a = _dot(ya_ref[...], wa_ref[...])
    up_b = _dot(yb_ref[...], wb_ref[...])
    merged = (gate_ref[:, :d].astype(F32) * up_a + gate_ref[:, d:].astype(F32) * up_b)
    o_ref[...] = x_ref[...] + _dot(merged.astype(BF16), wo_ref[...])


def _out_proj(x2d, ya, yb, gates, w_up_a, w_up_b, w_out, tm):
    n, d = x2d.shape
    const = lambda i: (0, 0)
    row = lambda i: (i, 0)
    return pl.pallas_call(
        _out_proj_kernel,
        grid=(n // tm,),
        in_specs=[
            pl.BlockSpec((tm, d), row),
            pl.BlockSpec((tm, BRANCH), row),
            pl.BlockSpec((tm, BRANCH), row),
            pl.BlockSpec((tm, 2 * d), row),
            pl.BlockSpec(w_up_a.shape, const),
            pl.BlockSpec(w_up_b.shape, const),
            pl.BlockSpec(w_out.shape, const),
        ],
        out_specs=pl.BlockSpec((tm, d), row),
        out_shape=jax.ShapeDtypeStruct((n, d), x2d.dtype),
        compiler_params=pltpu.CompilerParams(
            dimension_semantics=("arbitrary",), vmem_limit_bytes=VMEM_LIMIT),
        name="out_proj",
    )(x2d, ya, yb, gates, w_up_a, w_up_b, w_out)


def _lambda_init_for(layer):
    return 0.8 - 0.6 * math.exp(-0.3 * layer)


def kernel(x, norm_w, w_in, da_q_gain, da_k_gain, da_lambda, da_out_gain, gdn_conv_w, gdn_a_log,
           gdn_dt_bias, gdn_norm_w, w_up_da, w_up_gdn, w_out):
    b, s, d = x.shape
    depth = norm_w.shape[0]
    row_tile = 512
    attn_q_tile = 1024
    attn_k_tile = 256
    gdn_prep_tile = 512
    gdn_prep_group = 256
    gdn_scan_chunks = 4
    group = jnp.arange(BRANCH) // DA_HEAD_DIM
    gsum = (group[:, None] == group[None, :]).astype(BF16)
    groups_per_seg = BRANCH // DA_HEAD_DIM
    q_scale = DA_HEAD_DIM ** -0.5 * math.log2(math.e)

    x2d = x.reshape(b * s, d)
    for l in range(depth):
        w_feat = w_in[l].T
        n_blocked = w_feat.shape[0] // BRANCH * BRANCH
        w_t = _weight_prep(w_feat, n_blocked)
        w_last = jnp.pad(w_feat[n_blocked:], ((0, -(w_feat.shape[0] - n_blocked) % LANES), (0, 0))).astype(BF16)
        q_gain = jnp.tile(da_q_gain[l] * q_scale, groups_per_seg)[None, :]
        k_gain = jnp.tile(da_k_gain[l], groups_per_seg)[None, :]
        qa, ka, vat, za, qb, kb, vb, zb, ab, gates = _in_proj(
            x2d, norm_w[l][None, :], w_t, w_last, q_gain, k_gain, gsum, row_tile)

        r3 = lambda t: t.reshape(b, s, t.shape[-1])
        ya = _diff_attn(r3(qa), r3(ka), vat, r3(za), da_lambda[l], da_out_gain[l][None, :],
                        _lambda_init_for(l), attn_q_tile, attn_k_tile)

        alog_pad = jnp.pad(gdn_a_log[l], (0, LANES - GDN_HEADS))[None, :]
        dtb_pad = jnp.pad(gdn_dt_bias[l], (0, LANES - GDN_HEADS))[None, :]
        pqn, gl, o0 = _gdn_prep(r3(qb), r3(kb), r3(vb), r3(ab), gdn_conv_w[l], alog_pad, dtb_pad,
                               gdn_prep_tile, gdn_prep_group)
        yb = _gdn_scan(pqn, gl, o0, r3(zb), gdn_norm_w[l][None, :], gdn_scan_chunks)

        x2d = _out_proj(x2d, ya.reshape(b * s, -1), yb.reshape(b * s, -1), gates,
                        w_up_da[l].astype(BF16), w_up_gdn[l].astype(BF16), w_out[l].astype(BF16),
                        row_tile)
    return x2d.reshape(b, s, d)
```

```python
import functools
import math

import jax
import jax.numpy as jnp
from jax import lax
from jax.experimental import pallas as pl
from jax.experimental.pallas import tpu as pltpu

F32 = jnp.float32
BF16 = jnp.bfloat16

CHUNK = 64
EPS = 1e-6
DA_HEADS = 4
DA_HEAD_DIM = 64
DA_V_DIM = 2 * DA_HEAD_DIM
GDN_HEADS = 4
GDN_K_DIM = 128
GDN_V_DIM = 128
CONV_WIDTH = 4
BRANCH = 512
LANES = 128
NEG_BIG = -1e30
ONES_ROWS = 16
UNROLL_BLOCKS = 2
VMEM_LIMIT = 56 * 1024 * 1024


def _dot(a, b, precision=None):
    return jnp.dot(a, b, preferred_element_type=F32, precision=precision)


def _dot_nt(a, b, precision=None):
    return lax.dot_general(a, b, (((1,), (1,)), ((), ())),
                           preferred_element_type=F32, precision=precision)


def _dot_tn(a, b, precision=None):
    return lax.dot_general(a, b, (((0,), (0,)), ((), ())),
                           preferred_element_type=F32, precision=precision)


def _sigmoid(x):
    return 1.0 / (1.0 + jnp.exp(-x))


def _silu(x):
    return x * _sigmoid(x)


def _weight_prep_kernel(w_ref, wb_ref):
    wb_ref[...] = w_ref[...].astype(BF16)


def _weight_prep(wt, n_rows):
    d = wt.shape[1]
    blk = pl.BlockSpec((BRANCH, d), lambda j: (j, 0))
    return pl.pallas_call(
        _weight_prep_kernel,
        grid=(n_rows // BRANCH,),
        in_specs=[blk],
        out_specs=blk,
        out_shape=jax.ShapeDtypeStruct((n_rows, d), BF16),
        compiler_params=pltpu.CompilerParams(
            dimension_semantics=("arbitrary",), vmem_limit_bytes=VMEM_LIMIT),
        name="weight_prep",
    )(wt)


def _in_proj_kernel(x_ref, nw_ref, wt_ref, wlast_ref, qg_ref, kg_ref, gsum_ref,
                    qa_ref, ka_ref, vat_ref, za_ref, qb_ref, kb_ref, vb_ref, zb_ref, ab_ref, gate_ref):
    x = x_ref[...]
    ms = jnp.mean(x * x, axis=-1, keepdims=True)
    h = (x * lax.rsqrt(ms + EPS) * nw_ref[...]).astype(BF16)

    def proj(j):
        return _dot_nt(h, wt_ref[j * BRANCH:(j + 1) * BRANCH, :])

    def qk_norm(t, gain):
        ssq = _dot((t * t).astype(BF16), gsum_ref[...])
        return (t * lax.rsqrt(ssq * (1.0 / DA_HEAD_DIM) + EPS) * gain).astype(BF16)

    qa_ref[...] = qk_norm(proj(0), qg_ref[...])
    ka_ref[...] = qk_norm(proj(1), kg_ref[...])
    vat_ref[...] = _dot_nt(wt_ref[2 * BRANCH:3 * BRANCH, :], h).astype(BF16)
    za_ref[...] = _silu(proj(3)).astype(BF16)
    qb_ref[...] = proj(4).astype(BF16)
    kb_ref[...] = proj(5).astype(BF16)
    vb_ref[...] = proj(6).astype(BF16)
    zb_ref[...] = _silu(proj(7)).astype(BF16)
    tail = jnp.concatenate([_dot_nt(h, wt_ref[8 * BRANCH:, :]), _dot_nt(h, wlast_ref[...])], axis=1)
    ab_ref[...] = tail[:, :LANES]
    n_ab = 2 * GDN_HEADS
    gate_ref[...] = _sigmoid(tail[:, n_ab:n_ab + gate_ref.shape[1]]).astype(BF16)


def _in_proj(x2d, norm_w, w_t, w_last, q_gain, k_gain, gsum, tm):
    n, d = x2d.shape
    const = lambda i: (0, 0)
    row = lambda i: (i, 0)
    seg = jax.ShapeDtypeStruct((n, BRANCH), BF16)
    seg_spec = pl.BlockSpec((tm, BRANCH), row)
    out_shape = [seg, seg, jax.ShapeDtypeStruct((BRANCH, n), BF16)] + [seg] * 5 + [
        jax.ShapeDtypeStruct((n, LANES), F32), jax.ShapeDtypeStruct((n, 2 * d), BF16)]
    out_specs = ([seg_spec, seg_spec, pl.BlockSpec((BRANCH, tm), lambda i: (0, i))]
                 + [seg_spec] * 5
                 + [pl.BlockSpec((tm, LANES), row), pl.BlockSpec((tm, 2 * d), row)])
    return pl.pallas_call(
        _in_proj_kernel,
        grid=(n // tm,),
        in_specs=[
            pl.BlockSpec((tm, d), row),
            pl.BlockSpec((1, d), const),
            pl.BlockSpec(w_t.shape, const, pipeline_mode=pl.Buffered(1)),
            pl.BlockSpec(w_last.shape, const, pipeline_mode=pl.Buffered(1)),
            pl.BlockSpec((1, BRANCH), const),
            pl.BlockSpec((1, BRANCH), const),
            pl.BlockSpec((BRANCH, BRANCH), const, pipeline_mode=pl.Buffered(1)),
        ],
        out_specs=out_specs,
        out_shape=out_shape,
        compiler_params=pltpu.CompilerParams(
            dimension_semantics=("arbitrary",), vmem_limit_bytes=VMEM_LIMIT),
        name="in_proj",
    )(x2d, norm_w, w_t, w_last, q_gain, k_gain, gsum)


def _diff_attn_kernel(lam_ref, bias_ref, q_ref, k_ref, vt_ref, z_ref, og_ref, o_ref, s_scr, m_scr, acc_scr,
                      *, tq, tk, lambda_init):
    i = pl.program_id(2)
    q = q_ref[...]
    lane = lax.broadcasted_iota(jnp.int32, q.shape, 1)
    zero = jnp.zeros_like(q)
    qq = jnp.concatenate([jnp.where(lane < DA_HEAD_DIM, q, zero),
                          jnp.where(lane >= DA_HEAD_DIM, q, zero)], axis=0)
    sub_tiles = s_scr.shape[0]
    ones_rows = jnp.ones((ONES_ROWS, tk), BF16)
    m_scr[...] = jnp.full(m_scr.shape, NEG_BIG, F32)
    acc_scr[...] = jnp.zeros(acc_scr.shape, F32)

    def issue(start, slot, q0=0):
        wq = tq - q0
        qsel = qq if q0 == 0 else jnp.concatenate([qq[q0:tq], qq[tq + q0:]], axis=0)
        s = _dot_nt(k_ref[pl.ds(start, tk), :], qsel)
        s_scr[slot, :, :2 * wq] = s
        return jnp.max(s, axis=0, keepdims=True)

    def consume(start, slot, tile_max, masked, q0=0):
        wq = tq - q0
        s = s_scr[slot, :, :2 * wq]
        if masked:
            bias = bias_ref[...]
            pieces = []
            for g in range(2):
                pieces.append(s[:, g * wq:g * wq + tk] + bias)
                if wq > tk:
                    pieces.append(s[:, g * wq + tk:(g + 1) * wq])
            s = jnp.concatenate(pieces, axis=1)
            tile_max = jnp.max(s, axis=0, keepdims=True)
        m_cols = [slice(g * tq + q0, (g + 1) * tq) for g in range(2)]
        m = jnp.concatenate([m_scr[:, mc] for mc in m_cols], axis=1)
        m_new = jnp.maximum(m, tile_max)
        alpha = jnp.exp2(m - m_new)
        p = jnp.exp2(s - m_new)
        pb = p.astype(BF16)
        vt = jnp.concatenate([vt_ref[:, pl.ds(start, tk)], ones_rows], axis=0)
        for g in range(2):
            cols = slice(g * wq, (g + 1) * wq)
            m_scr[:, m_cols[g]] = m_new[:, cols]
            acc_scr[g, :, q0:] = alpha[:, cols] * acc_scr[g, :, q0:] + _dot(vt, pb[:, cols])

    def make_body(blocks):
        def body(jj, tile_max):
            for u in range(blocks):
                for t in range(sub_tiles):
                    start = pl.multiple_of((jj * blocks + u) * tq + t * tk, tk)
                    next_max = issue(start + tk, (t + 1) % sub_tiles)
                    consume(start, t, tile_max, masked=False)
                    tile_max = next_max
            return tile_max
        return body

    paired = i // UNROLL_BLOCKS
    tile_max = lax.fori_loop(0, paired, make_body(UNROLL_BLOCKS), issue(0, 0))
    tile_max = lax.fori_loop(paired * UNROLL_BLOCKS, i, make_body(1), tile_max)
    for t in range(sub_tiles):
        start = pl.multiple_of(i * tq + t * tk, tk)
        if t + 1 < sub_tiles:
            issue(start + tk, t + 1, q0=(t + 1) * tk)
        consume(start, t, None, masked=True, q0=t * tk)
    acc1 = acc_scr[0]
    acc2 = acc_scr[1]

    lam_p = lam_ref[...]
    lam = (jnp.exp(jnp.sum(lam_p[0:1] * lam_p[1:2], axis=-1, keepdims=True))
           - jnp.exp(jnp.sum(lam_p[2:3] * lam_p[3:4], axis=-1, keepdims=True))
           + lambda_init)
    ot = (acc1[:DA_V_DIM] / acc1[DA_V_DIM:DA_V_DIM + 1]
          - lam * (acc2[:DA_V_DIM] / acc2[DA_V_DIM:DA_V_DIM + 1]))
    ms = jnp.mean(ot * ot, axis=0, keepdims=True)
    y = (ot * lax.rsqrt(ms + EPS)).T * og_ref[...] * (1.0 - lambda_init)
    o_ref[...] = (y * z_ref[...].astype(F32)).astype(o_ref.dtype)


def _diff_attn(qa, ka, vat, za, da_lambda, out_gain, lambda_init, tq, tk):
    b, s, _ = qa.shape
    sub_tiles = tq // tk
    assert sub_tiles >= 2 and sub_tiles * tk == tq
    blk = lambda bi, h, i: (bi, i, h)
    const = lambda bi, h, i: (0, 0)
    chunk_of = jnp.arange(tk) // CHUNK
    bias = jnp.where(chunk_of[:, None] <= chunk_of[None, :], 0.0, NEG_BIG).astype(F32)
    return pl.pallas_call(
        functools.partial(_diff_attn_kernel, tq=tq, tk=tk, lambda_init=lambda_init),
        scratch_shapes=[pltpu.VMEM((sub_tiles, tk, 2 * tq), F32),
                        pltpu.VMEM((1, 2 * tq), F32),
                        pltpu.VMEM((2, DA_V_DIM + ONES_ROWS, tq), F32)],
        grid=(b, DA_HEADS, s // tq),
        in_specs=[
            pl.BlockSpec(da_lambda.shape, const),
            pl.BlockSpec(bias.shape, const),
            pl.BlockSpec((None, tq, DA_V_DIM), blk),
            pl.BlockSpec((None, s, DA_V_DIM), lambda bi, h, i: (bi, 0, h)),
            pl.BlockSpec((DA_V_DIM, s), lambda bi, h, i: (h, bi)),
            pl.BlockSpec((None, tq, DA_V_DIM), blk),
            pl.BlockSpec((1, DA_V_DIM), const),
        ],
        out_specs=pl.BlockSpec((None, tq, DA_V_DIM), blk),
        out_shape=jax.ShapeDtypeStruct((b, s, DA_HEADS * DA_V_DIM), BF16),
        compiler_params=pltpu.CompilerParams(
            dimension_semantics=("arbitrary", "arbitrary", "arbitrary"),
            vmem_limit_bytes=VMEM_LIMIT),
        name="diff_attn",
    )(da_lambda, bias, qa, ka, vat, za, out_gain)


WIDE = GDN_HEADS * CHUNK
HALO = 16


def _split2(x):
    hi = x.astype(BF16)
    lo = (x - hi.astype(F32)).astype(BF16)
    return hi, lo


def _block_diag(y, mask01):
    return jnp.concatenate([y] * GDN_HEADS, axis=0) * mask01


def _wide_mm(x, y, bd_mask):
    return _dot(x.astype(BF16), _block_diag(y.astype(BF16), bd_mask))


def _wide_unit_lower_inverse(a_list, eye, blk16, mm):
    d = [jnp.where(blk16, a, 0.0) for a in a_list]
    off = [a - di for a, di in zip(a_list, d)]
    p = [-di for di in d]
    t = [eye + pi for pi in p]
    for _ in range(3):
        p = [mm(pi, pi) for pi in p]
        yield
        t = [ti + mm(ti, pi) for ti, pi in zip(t, p)]
        yield
    e = [mm(ti, oi) for ti, oi in zip(t, off)]
    yield
    e2 = [mm(ei, ei) for ei in e]
    yield
    f = [eye - ei + e2i - mm(ei, e2i) for ei, e2i in zip(e, e2)]
    yield
    return [mm(fi, ti) for fi, ti in zip(f, t)]


def _run_interleaved(*gens):
    results = [None] * len(gens)
    alive = [g is not None for g in gens]
    while any(alive):
        for n, gen in enumerate(gens):
            if alive[n]:
                try:
                    next(gen)
                except StopIteration as stop:
                    results[n] = stop.value
                    alive[n] = False
    return results


def _head_l2norm(x):
    parts = []
    for h in range(GDN_HEADS):
        xh = x[:, h * GDN_K_DIM:(h + 1) * GDN_K_DIM]
        parts.append(xh * lax.rsqrt(jnp.sum(xh * xh, axis=-1, keepdims=True) + EPS))
    return jnp.concatenate(parts, axis=1)


def _head_bcast(col4, width):
    rows = col4.shape[0]
    return jnp.concatenate([jnp.broadcast_to(col4[:, h:h + 1], (rows, width))
                            for h in range(GDN_HEADS)], axis=1)


def _gdn_kernel(q_ref, k_ref, v_ref, qh_ref, kh_ref, vh_ref, ab_ref, z_ref, cw_ref, alog_ref, dtb_ref,
                nw_ref, shift_ref, hsel_ref, o_ref, state_ref):
    tr = q_ref.shape[0]

    @pl.when(pl.program_id(1) == 0)
    def _():
        state_ref[...] = jnp.zeros_like(state_ref)

    grp = shift_ref.shape[1]
    chunks = grp // CHUNK

    row_w = lax.broadcasted_iota(jnp.int32, (CHUNK, WIDE), 0)
    col_w = lax.broadcasted_iota(jnp.int32, (CHUNK, WIDE), 1) % CHUNK
    tri_incl = col_w <= row_w
    tri_strict = col_w < row_w
    eye_w = jnp.where(col_w == row_w, 1.0, 0.0).astype(F32)
    blk16 = (row_w // 16) == (col_w // 16)
    bd_mask = jnp.where(lax.broadcasted_iota(jnp.int32, (WIDE, WIDE), 0) // CHUNK
                        == lax.broadcasted_iota(jnp.int32, (WIDE, WIDE), 1) // CHUNK,
                        1.0, 0.0).astype(BF16)
    kbd_mask = jnp.where(lax.broadcasted_iota(jnp.int32, (WIDE, BRANCH), 0) // CHUNK
                         == lax.broadcasted_iota(jnp.int32, (WIDE, BRANCH), 1) // GDN_K_DIM,
                         1.0, 0.0).astype(BF16)
    r64 = lax.broadcasted_iota(jnp.int32, (CHUNK, CHUNK), 0)
    c64 = lax.broadcasted_iota(jnp.int32, (CHUNK, CHUNK), 1)
    lower_ones = jnp.where(c64 <= r64, 1.0, 0.0).astype(BF16)
    mm = functools.partial(_wide_mm, bd_mask=bd_mask)

    def conv_silu(x_ref, x_halo_ref, seg, gr, first_group):
        cols = slice(seg * BRANCH, (seg + 1) * BRANCH)
        cur = x_ref[gr, :]
        if first_group:
            halo = x_halo_ref[...]
            halo = jnp.where(pl.program_id(1) == 0, jnp.zeros((), halo.dtype), halo)
        else:
            halo = x_ref[gr.start - HALO:gr.start, :]
        conv = cur.astype(F32) * cw_ref[CONV_WIDTH - 1:CONV_WIDTH, cols]
        for s in range(1, CONV_WIDTH):
            shifted = _dot(shift_ref[s - 1], cur)
            head = shifted[:HALO] + _dot(hsel_ref[s - 1], halo)
            shifted = jnp.concatenate([head, shifted[HALO:]], axis=0)
            conv = conv + shifted * cw_ref[CONV_WIDTH - 1 - s:CONV_WIDTH - s, cols]
        return _silu(conv)

    def front(g):
        gr = slice(g * grp, (g + 1) * grp)
        q = _head_l2norm(conv_silu(q_ref, qh_ref, 0, gr, g == 0)) * (GDN_K_DIM ** -0.5)
        yield
        k = _head_l2norm(conv_silu(k_ref, kh_ref, 1, gr, g == 0))
        yield
        v = conv_silu(v_ref, vh_ref, 2, gr, g == 0)
        yield

        ab = ab_ref[gr, :]
        xg = ab + dtb_ref[...]
        softplus = jnp.maximum(xg, 0.0) + jnp.log1p(jnp.exp(-jnp.abs(xg)))
        g_all = -jnp.exp(alog_ref[...]) * softplus
        beta_all = _sigmoid(ab)

        staged = []
        for c in range(chunks):
            rows = slice(c * CHUNK, (c + 1) * CHUNK)
            qc, kc, vc = q[rows], k[rows], v[rows]
            g_c = g_all[rows]
            g_hi, g_lo = _split2(g_c)
            gc2 = _dot(lower_ones, jnp.concatenate([g_hi, g_lo], axis=1))
            gc_all = gc2[:, :LANES] + gc2[:, LANES:]
            gcb = _head_bcast(gc_all, GDN_K_DIM)
            betab = _head_bcast(beta_all[rows, GDN_HEADS:], GDN_K_DIM)
            exp_gc = jnp.exp(gcb)
            gl_row = exp_gc[CHUNK - 1:CHUNK]
            k_beta = kc * betab
            v_beta = vc * betab
            kbe = k_beta * exp_gc
            q_dec = qc * exp_gc
            k_tail = kc * jnp.exp(gcb[CHUNK - 1:CHUNK] - gcb)

            r_w = jnp.where(tri_strict, _head_bcast(g_c, CHUNK), 0.0)
            r_hi, r_lo = _split2(r_w)
            d2 = _dot(lower_ones, jnp.concatenate([r_hi, r_lo], axis=1))
            gamma = jnp.exp(jnp.where(tri_incl, d2[:, :WIDE] + d2[:, WIDE:], NEG_BIG))

            k16 = kc.astype(BF16)
            kbd = jnp.concatenate([k16] * GDN_HEADS, axis=0) * kbd_mask
            sc = _dot_nt(jnp.concatenate([k_beta.astype(BF16), qc.astype(BF16)], axis=0), kbd)
            a_strict = jnp.where(tri_strict, sc[:CHUNK] * gamma, 0.0)
            a_intra = jnp.where(tri_incl, sc[CHUNK:] * gamma, 0.0)
            staged.append((a_strict, a_intra, v_beta, kbe, q_dec, k_tail, gl_row))
            yield
        return staged

    def back(g, staged):
        t_inv = yield from _wide_unit_lower_inverse([st[0] for st in staged], eye_w, blk16, mm)
        pairs = [(c, h) for c in range(chunks) for h in range(GDN_HEADS)]
        hsl = lambda h: slice(h * GDN_K_DIM, (h + 1) * GDN_K_DIM)
        wsl = lambda h: slice(h * CHUNK, (h + 1) * CHUNK)
        uw16 = {}
        for c, h in pairs:
            _, _, v_beta, kbe, _, _, _ = staged[c]
            t_off = t_inv[c] - eye_w
            rhs = jnp.concatenate([v_beta[:, hsl(h)], kbe[:, hsl(h)]], axis=1)
            uw = rhs + _dot(t_off[:, wsl(h)].astype(BF16), rhs.astype(BF16))
            uw16[c, h] = uw.astype(BF16)
        yield
        q16, o0 = {}, {}
        for c, h in pairs:
            a_intra, q_dec = staged[c][1], staged[c][4]
            ai_uw = _dot(a_intra[:, wsl(h)].astype(BF16), uw16[c, h])
            q16[c, h] = (q_dec[:, hsl(h)] - ai_uw[:, GDN_V_DIM:]).astype(BF16)
            o0[c, h] = ai_uw[:, :GDN_V_DIM]
        yield
        ops = {}
        for c, h in pairs:
            k_tail = staged[c][5]
            kt_uw = _dot_tn(k_tail[:, hsl(h)].astype(BF16), uw16[c, h])
            pq16 = jnp.concatenate([kt_uw[:, GDN_V_DIM:].astype(BF16), q16[c, h]], axis=0)
            ops[c, h] = (pq16, kt_uw[:, :GDN_V_DIM], o0[c, h], staged[c][6][:, hsl(h)])
        yield
        return ops

    def scan(g, ops):
        hsl = lambda h: slice(h * GDN_K_DIM, (h + 1) * GDN_K_DIM)
        for c in range(chunks):
            rows = slice((g * chunks + c) * CHUNK, (g * chunks + c + 1) * CHUNK)
            for h in range(GDN_HEADS):
                pq16, n_c, o0_c, gl = ops[c, h]
                state = state_ref[h]
                r = _dot(pq16, state.astype(BF16))
                state_ref[h] = state * gl - r[:GDN_K_DIM] + n_c
                o = r[GDN_K_DIM:] + o0_c
                ms = jnp.mean(o * o, axis=-1, keepdims=True)
                y = o * lax.rsqrt(ms + EPS) * nw_ref[...]
                o_ref[rows, hsl(h)] = (y * z_ref[rows, hsl(h)].astype(F32)).astype(o_ref.dtype)
            yield

    groups = tr // grp
    staged, = _run_interleaved(front(0))
    ops = None
    for g in range(groups):
        nxt = front(g + 1) if g + 1 < groups else None
        prev_scan = scan(g - 1, ops) if g > 0 else None
        ops, staged, _ = _run_interleaved(back(g, staged), nxt, prev_scan)
    _run_interleaved(scan(groups - 1, ops))


def _gdn(qb, kb, vb, ab, zb, conv_w, alog_pad, dtb_pad, norm_w, tr, grp):
    b, s, _ = qb.shape
    assert tr % grp == 0 and grp % CHUNK == 0
    blk = lambda bi, i: (bi, i, 0)
    halo = lambda bi, i: (bi, jnp.maximum(i * (tr // HALO) - 1, 0), 0)
    const = lambda bi, i: (0, 0)
    const3 = lambda bi, i: (0, 0, 0)
    t_idx = jnp.arange(grp)
    h_idx = jnp.arange(HALO)
    shift = jnp.stack([(t_idx[:, None] - sft == t_idx[None, :]) for sft in range(1, CONV_WIDTH)]).astype(BF16)
    hsel = jnp.stack([(h_idx[:, None] - sft + HALO == h_idx[None, :])
                      for sft in range(1, CONV_WIDTH)]).astype(BF16)
    return pl.pallas_call(
        _gdn_kernel,
        grid=(b, s // tr),
        in_specs=[
            pl.BlockSpec((None, tr, BRANCH), blk),
            pl.BlockSpec((None, tr, BRANCH), blk),
            pl.BlockSpec((None, tr, BRANCH), blk),
            pl.BlockSpec((None, HALO, BRANCH), halo),
            pl.BlockSpec((None, HALO, BRANCH), halo),
            pl.BlockSpec((None, HALO, BRANCH), halo),
            pl.BlockSpec((None, tr, LANES), blk),
            pl.BlockSpec((None, tr, BRANCH), blk),
            pl.BlockSpec(conv_w.shape, const),
            pl.BlockSpec((1, LANES), const),
            pl.BlockSpec((1, LANES), const),
            pl.BlockSpec((1, GDN_V_DIM), const),
            pl.BlockSpec(shift.shape, const3),
            pl.BlockSpec(hsel.shape, const3),
        ],
        out_specs=pl.BlockSpec((None, tr, BRANCH), blk),
        out_shape=jax.ShapeDtypeStruct((b, s, BRANCH), BF16),
        scratch_shapes=[pltpu.VMEM((GDN_HEADS, GDN_K_DIM, GDN_V_DIM), F32)],
        compiler_params=pltpu.CompilerParams(
            dimension_semantics=("arbitrary", "arbitrary"), vmem_limit_bytes=VMEM_LIMIT),
        name="gdn",
    )(qb, kb, vb, qb, kb, vb, ab, zb, conv_w, alog_pad, dtb_pad, norm_w, shift, hsel)


def _out_proj_kernel(x_ref, ya_ref, yb_ref, gate_ref, wa_ref, wb_ref, wo_ref, o_ref):
    d = x_ref.shape[-1]
    up_a = _dot(ya_ref[...], wa_ref[...])
    up_b = _dot(yb_ref[...], wb_ref[...])
    merged = (gate_ref[:, :d].astype(F32) * up_a + gate_ref[:, d:].astype(F32) * up_b)
    o_ref[...] = x_ref[...] + _dot(merged.astype(BF16), wo_ref[...])


def _out_proj(x2d, ya, yb, gates, w_up_a, w_up_b, w_out, tm):
    n, d = x2d.shape
    const = lambda i: (0, 0)
    row = lambda i: (i, 0)
    return pl.pallas_call(
        _out_proj_kernel,
        grid=(n // tm,),
        in_specs=[
            pl.BlockSpec((tm, d), row),
            pl.BlockSpec((tm, BRANCH), row),
            pl.BlockSpec((tm, BRANCH), row),
            pl.BlockSpec((tm, 2 * d), row),
            pl.BlockSpec(w_up_a.shape, const),
            pl.BlockSpec(w_up_b.shape, const),
            pl.BlockSpec(w_out.shape, const),
        ],
        out_specs=pl.BlockSpec((tm, d), row),
        out_shape=jax.ShapeDtypeStruct((n, d), x2d.dtype),
        compiler_params=pltpu.CompilerParams(
            dimension_semantics=("arbitrary",), vmem_limit_bytes=VMEM_LIMIT),
        name="out_proj",
    )(x2d, ya, yb, gates, w_up_a, w_up_b, w_out)


def _lambda_init_for(layer):
    return 0.8 - 0.6 * math.exp(-0.3 * layer)


def kernel(x, norm_w, w_in, da_q_gain, da_k_gain, da_lambda, da_out_gain, gdn_conv_w, gdn_a_log,
           gdn_dt_bias, gdn_norm_w, w_up_da, w_up_gdn, w_out):
    b, s, d = x.shape
    depth = norm_w.shape[0]
    row_tile = 512
    attn_q_tile = 1024
    attn_k_tile = 256
    gdn_tile = 512
    gdn_group = 256
    group = jnp.arange(BRANCH) // DA_HEAD_DIM
    gsum = (group[:, None] == group[None, :]).astype(BF16)
    groups_per_seg = BRANCH // DA_HEAD_DIM
    q_scale = DA_HEAD_DIM ** -0.5 * math.log2(math.e)

    x2d = x.reshape(b * s, d)
    for l in range(depth):
        w_feat = w_in[l].T
        n_blocked = w_feat.shape[0] // BRANCH * BRANCH
        w_t = _weight_prep(w_feat, n_blocked)
        w_last = jnp.pad(w_feat[n_blocked:], ((0, -(w_feat.shape[0] - n_blocked) % LANES), (0, 0))).astype(BF16)
        q_gain = jnp.tile(da_q_gain[l] * q_scale, groups_per_seg)[None, :]
        k_gain = jnp.tile(da_k_gain[l], groups_per_seg)[None, :]
        qa, ka, vat, za, qb, kb, vb, zb, ab, gates = _in_proj(
            x2d, norm_w[l][None, :], w_t, w_last, q_gain, k_gain, gsum, row_tile)

        r3 = lambda t: t.reshape(b, s, t.shape[-1])
        ya = _diff_attn(r3(qa), r3(ka), vat, r3(za), da_lambda[l], da_out_gain[l][None, :],
                        _lambda_init_for(l), attn_q_tile, attn_k_tile)

        alog_pad = jnp.pad(gdn_a_log[l], (0, LANES - GDN_HEADS))[None, :]
        dtb_pad = jnp.pad(gdn_dt_bias[l], (0, LANES - GDN_HEADS))[None, :]
        yb = _gdn(r3(qb), r3(kb), r3(vb), r3(ab), r3(zb), gdn_conv_w[l], alog_pad, dtb_pad,
                  gdn_norm_w[l][None, :], gdn_tile, gdn_group)

        x2d = _out_proj(x2d, ya.reshape(b * s, -1), yb.reshape(b * s, -1), gates,
                        w_up_da[l].astype(BF16), w_up_gdn[l].astype(BF16), w_out[l].astype(BF16),
                        row_tile)
    return x2d.reshape(b, s, d)
```

```python
import functools
import math

import jax
import jax.numpy as jnp
from jax import lax
from jax.experimental import pallas as pl
from jax.experimental.pallas import tpu as pltpu

F32 = jnp.float32
BF16 = jnp.bfloat16

CHUNK = 64
EPS = 1e-6
DA_HEADS = 4
DA_HEAD_DIM = 64
DA_V_DIM = 2 * DA_HEAD_DIM
GDN_HEADS = 4
GDN_K_DIM = 128
GDN_V_DIM = 128
CONV_WIDTH = 4
BRANCH = 512
LANES = 128
NEG_BIG = -1e30
ONES_ROWS = 16
UNROLL_BLOCKS = 2
VMEM_LIMIT = 56 * 1024 * 1024


def _dot(a, b, precision=None):
    return jnp.dot(a, b, preferred_element_type=F32, precision=precision)


def _dot_nt(a, b, precision=None):
    return lax.dot_general(a, b, (((1,), (1,)), ((), ())),
                           preferred_element_type=F32, precision=precision)


def _dot_tn(a, b, precision=None):
    return lax.dot_general(a, b, (((0,), (0,)), ((), ())),
                           preferred_element_type=F32, precision=precision)


def _sigmoid(x):
    return 1.0 / (1.0 + jnp.exp(-x))


def _silu(x):
    return x * _sigmoid(x)


def _weight_prep_kernel(w_ref, wb_ref):
    wb_ref[...] = w_ref[...].astype(BF16)


def _weight_prep(wt, n_rows):
    d = wt.shape[1]
    blk = pl.BlockSpec((BRANCH, d), lambda j: (j, 0))
    return pl.pallas_call(
        _weight_prep_kernel,
        grid=(n_rows // BRANCH,),
        in_specs=[blk],
        out_specs=blk,
        out_shape=jax.ShapeDtypeStruct((n_rows, d), BF16),
        compiler_params=pltpu.CompilerParams(
            dimension_semantics=("arbitrary",), vmem_limit_bytes=VMEM_LIMIT),
        name="weight_prep",
    )(wt)


def _in_proj_kernel(x_ref, nw_ref, wt_ref, wlast_ref, qg_ref, kg_ref, gsum_ref,
                    qa_ref, ka_ref, vat_ref, za_ref, qb_ref, kb_ref, vb_ref, zb_ref, ab_ref, gate_ref):
    x = x_ref[...]
    ms = jnp.mean(x * x, axis=-1, keepdims=True)
    h = (x * lax.rsqrt(ms + EPS) * nw_ref[...]).astype(BF16)

    def proj(j):
        return _dot_nt(h, wt_ref[j * BRANCH:(j + 1) * BRANCH, :])

    def qk_norm(t, gain):
        ssq = _dot((t * t).astype(BF16), gsum_ref[...])
        return (t * lax.rsqrt(ssq * (1.0 / DA_HEAD_DIM) + EPS) * gain).astype(BF16)

    qa_ref[...] = qk_norm(proj(0), qg_ref[...])
    ka_ref[...] = qk_norm(proj(1), kg_ref[...])
    vat_ref[...] = _dot_nt(wt_ref[2 * BRANCH:3 * BRANCH, :], h).astype(BF16)
    za_ref[...] = _silu(proj(3)).astype(BF16)
    qb_ref[...] = proj(4).astype(BF16)
    kb_ref[...] = proj(5).astype(BF16)
    vb_ref[...] = proj(6).astype(BF16)
    zb_ref[...] = _silu(proj(7)).astype(BF16)
    tail = jnp.concatenate([_dot_nt(h, wt_ref[8 * BRANCH:, :]), _dot_nt(h, wlast_ref[...])], axis=1)
    ab_ref[...] = tail[:, :LANES]
    n_ab = 2 * GDN_HEADS
    gate_ref[...] = _sigmoid(tail[:, n_ab:n_ab + gate_ref.shape[1]]).astype(BF16)


def _in_proj(x2d, norm_w, w_t, w_last, q_gain, k_gain, gsum, tm):
    n, d = x2d.shape
    const = lambda i: (0, 0)
    row = lambda i: (i, 0)
    seg = jax.ShapeDtypeStruct((n, BRANCH), BF16)
    seg_spec = pl.BlockSpec((tm, BRANCH), row)
    out_shape = [seg, seg, jax.ShapeDtypeStruct((BRANCH, n), BF16)] + [seg] * 5 + [
        jax.ShapeDtypeStruct((n, LANES), F32), jax.ShapeDtypeStruct((n, 2 * d), BF16)]
    out_specs = ([seg_spec, seg_spec, pl.BlockSpec((BRANCH, tm), lambda i: (0, i))]
                 + [seg_spec] * 5
                 + [pl.BlockSpec((tm, LANES), row), pl.BlockSpec((tm, 2 * d), row)])
    return pl.pallas_call(
        _in_proj_kernel,
        grid=(n // tm,),
        in_specs=[
            pl.BlockSpec((tm, d), row),
            pl.BlockSpec((1, d), const),
            pl.BlockSpec(w_t.shape, const, pipeline_mode=pl.Buffered(1)),
            pl.BlockSpec(w_last.shape, const, pipeline_mode=pl.Buffered(1)),
            pl.BlockSpec((1, BRANCH), const),
            pl.BlockSpec((1, BRANCH), const),
            pl.BlockSpec((BRANCH, BRANCH), const, pipeline_mode=pl.Buffered(1)),
        ],
        out_specs=out_specs,
        out_shape=out_shape,
        compiler_params=pltpu.CompilerParams(
            dimension_semantics=("arbitrary",), vmem_limit_bytes=VMEM_LIMIT),
        name="in_proj",
    )(x2d, norm_w, w_t, w_last, q_gain, k_gain, gsum)


def _diff_attn_kernel(lam_ref, bias_ref, q_ref, k_ref, vt_ref, z_ref, og_ref, o_ref, s_scr, m_scr, acc_scr,
                      *, tq, tk, lambda_init):
    i = pl.program_id(2)
    q = q_ref[...]
    lane = lax.broadcasted_iota(jnp.int32, q.shape, 1)
    zero = jnp.zeros_like(q)
    qq = jnp.concatenate([jnp.where(lane < DA_HEAD_DIM, q, zero),
                          jnp.where(lane >= DA_HEAD_DIM, q, zero)], axis=0)
    sub_tiles = s_scr.shape[0]
    ones_rows = jnp.ones((ONES_ROWS, tk), BF16)
    m_scr[...] = jnp.full(m_scr.shape, NEG_BIG, F32)
    acc_scr[...] = jnp.zeros(acc_scr.shape, F32)

    def issue(start, slot, q0=0):
        wq = tq - q0
        qsel = qq if q0 == 0 else jnp.concatenate([qq[q0:tq], qq[tq + q0:]], axis=0)
        s = _dot_nt(k_ref[pl.ds(start, tk), :], qsel)
        s_scr[slot, :, :2 * wq] = s
        return jnp.max(s, axis=0, keepdims=True)

    def consume(start, slot, tile_max, masked, q0=0):
        wq = tq - q0
        s = s_scr[slot, :, :2 * wq]
        if masked:
            bias = bias_ref[...]
            pieces = []
            for g in range(2):
                pieces.append(s[:, g * wq:g * wq + tk] + bias)
                if wq > tk:
                    pieces.append(s[:, g * wq + tk:(g + 1) * wq])
            s = jnp.concatenate(pieces, axis=1)
            tile_max = jnp.max(s, axis=0, keepdims=True)
        m_cols = [slice(g * tq + q0, (g + 1) * tq) for g in range(2)]
        m = jnp.concatenate([m_scr[:, mc] for mc in m_cols], axis=1)
        m_new = jnp.maximum(m, tile_max)
        alpha = jnp.exp2(m - m_new)
        p = jnp.exp2(s - m_new)
        pb = p.astype(BF16)
        vt = jnp.concatenate([vt_ref[:, pl.ds(start, tk)], ones_rows], axis=0)
        for g in range(2):
            cols = slice(g * wq, (g + 1) * wq)
            m_scr[:, m_cols[g]] = m_new[:, cols]
            acc_scr[g, :, q0:] = alpha[:, cols] * acc_scr[g, :, q0:] + _dot(vt, pb[:, cols])

    def make_body(blocks):
        def body(jj, tile_max):
            for u in range(blocks):
                for t in range(sub_tiles):
                    start = pl.multiple_of((jj * blocks + u) * tq + t * tk, tk)
                    next_max = issue(start + tk, (t + 1) % sub_tiles)
                    consume(start, t, tile_max, masked=False)
                    tile_max = next_max
            return tile_max
        return body

    paired = i // UNROLL_BLOCKS
    tile_max = lax.fori_loop(0, paired, make_body(UNROLL_BLOCKS), issue(0, 0))
    tile_max = lax.fori_loop(paired * UNROLL_BLOCKS, i, make_body(1), tile_max)
    for t in range(sub_tiles):
        start = pl.multiple_of(i * tq + t * tk, tk)
        if t + 1 < sub_tiles:
            issue(start + tk, t + 1, q0=(t + 1) * tk)
        consume(start, t, None, masked=True, q0=t * tk)
    acc1 = acc_scr[0]
    acc2 = acc_scr[1]

    lam_p = lam_ref[...]
    lam = (jnp.exp(jnp.sum(lam_p[0:1] * lam_p[1:2], axis=-1, keepdims=True))
           - jnp.exp(jnp.sum(lam_p[2:3] * lam_p[3:4], axis=-1, keepdims=True))
           + lambda_init)
    ot = (acc1[:DA_V_DIM] / acc1[DA_V_DIM:DA_V_DIM + 1]
          - lam * (acc2[:DA_V_DIM] / acc2[DA_V_DIM:DA_V_DIM + 1]))
    ms = jnp.mean(ot * ot, axis=0, keepdims=True)
    y = (ot * lax.rsqrt(ms + EPS)).T * og_ref[...] * (1.0 - lambda_init)
    o_ref[...] = (y * z_ref[...].astype(F32)).astype(o_ref.dtype)


def _diff_attn(qa, ka, vat, za, da_lambda, out_gain, lambda_init, tq, tk):
    b, s, _ = qa.shape
    sub_tiles = tq // tk
    assert sub_tiles >= 2 and sub_tiles * tk == tq
    blk = lambda bi, h, i: (bi, i, h)
    const = lambda bi, h, i: (0, 0)
    chunk_of = jnp.arange(tk) // CHUNK
    bias = jnp.where(chunk_of[:, None] <= chunk_of[None, :], 0.0, NEG_BIG).astype(F32)
    return pl.pallas_call(
        functools.partial(_diff_attn_kernel, tq=tq, tk=tk, lambda_init=lambda_init),
        scratch_shapes=[pltpu.VMEM((sub_tiles, tk, 2 * tq), F32),
                        pltpu.VMEM((1, 2 * tq), F32),
                        pltpu.VMEM((2, DA_V_DIM + ONES_ROWS, tq), F32)],
        grid=(b, DA_HEADS, s // tq),
        in_specs=[
            pl.BlockSpec(da_lambda.shape, const),
            pl.BlockSpec(bias.shape, const),
            pl.BlockSpec((None, tq, DA_V_DIM), blk),
            pl.BlockSpec((None, s, DA_V_DIM), lambda bi, h, i: (bi, 0, h)),
            pl.BlockSpec((DA_V_DIM, s), lambda bi, h, i: (h, bi)),
            pl.BlockSpec((None, tq, DA_V_DIM), blk),
            pl.BlockSpec((1, DA_V_DIM), const),
        ],
        out_specs=pl.BlockSpec((None, tq, DA_V_DIM), blk),
        out_shape=jax.ShapeDtypeStruct((b, s, DA_HEADS * DA_V_DIM), BF16),
        compiler_params=pltpu.CompilerParams(
            dimension_semantics=("arbitrary", "arbitrary", "arbitrary"),
            vmem_limit_bytes=VMEM_LIMIT),
        name="diff_attn",
    )(da_lambda, bias, qa, ka, vat, za, out_gain)


WIDE = GDN_HEADS * CHUNK
HALO = 16


def _split2(x):
    hi = x.astype(BF16)
    lo = (x - hi.astype(F32)).astype(BF16)
    return hi, lo


def _block_diag(y, mask01):
    return jnp.concatenate([y] * GDN_HEADS, axis=0) * mask01


def _wide_mm(x, y, bd_mask):
    return _dot(x.astype(BF16), _block_diag(y.astype(BF16), bd_mask))


def _wide_unit_lower_inverse(a_list, eye, blk16, mm):
    d = [jnp.where(blk16, a, 0.0) for a in a_list]
    off = [a - di for a, di in zip(a_list, d)]
    p = [-di for di in d]
    t = [eye + pi for pi in p]
    for _ in range(3):
        p = [mm(pi, pi) for pi in p]
        yield
        t = [ti + mm(ti, pi) for ti, pi in zip(t, p)]
        yield
    e = [mm(ti, oi) for ti, oi in zip(t, off)]
    yield
    e2 = [mm(ei, ei) for ei in e]
    yield
    f = [eye - ei + e2i - mm(ei, e2i) for ei, e2i in zip(e, e2)]
    yield
    return [mm(fi, ti) for fi, ti in zip(f, t)]


def _run_interleaved(*gens):
    results = [None] * len(gens)
    alive = [g is not None for g in gens]
    while any(alive):
        for n, gen in enumerate(gens):
            if alive[n]:
                try:
                    next(gen)
                except StopIteration as stop:
                    results[n] = stop.value
                    alive[n] = False
    return results


def _head_l2norm(x):
    parts = []
    for h in range(GDN_HEADS):
        xh = x[:, h * GDN_K_DIM:(h + 1) * GDN_K_DIM]
        parts.append(xh * lax.rsqrt(jnp.sum(xh * xh, axis=-1, keepdims=True) + EPS))
    return jnp.concatenate(parts, axis=1)


def _head_bcast(col4, width):
    rows = col4.shape[0]
    return jnp.concatenate([jnp.broadcast_to(col4[:, h:h + 1], (rows, width))
                            for h in range(GDN_HEADS)], axis=1)


def _gdn_kernel(q_ref, k_ref, v_ref, qh_ref, kh_ref, vh_ref, ab_ref, z_ref, cw_ref, alog_ref, dtb_ref,
                nw_ref, shift_ref, hsel_ref, o_ref, state_ref):
    tr = q_ref.shape[0]

    @pl.when(pl.program_id(1) == 0)
    def _():
        state_ref[...] = jnp.zeros_like(state_ref)

    grp = shift_ref.shape[1]
    chunks = grp // CHUNK

    row_w = lax.broadcasted_iota(jnp.int32, (CHUNK, WIDE), 0)
    col_w = lax.broadcasted_iota(jnp.int32, (CHUNK, WIDE), 1) % CHUNK
    tri_incl = col_w <= row_w
    tri_strict = col_w < row_w
    eye_w = jnp.where(col_w == row_w, 1.0, 0.0).astype(F32)
    blk16 = (row_w // 16) == (col_w // 16)
    bd_mask = jnp.where(lax.broadcasted_iota(jnp.int32, (WIDE, WIDE), 0) // CHUNK
                        == lax.broadcasted_iota(jnp.int32, (WIDE, WIDE), 1) // CHUNK,
                        1.0, 0.0).astype(BF16)
    kbd_mask = jnp.where(lax.broadcasted_iota(jnp.int32, (WIDE, BRANCH), 0) // CHUNK
                         == lax.broadcasted_iota(jnp.int32, (WIDE, BRANCH), 1) // GDN_K_DIM,
                         1.0, 0.0).astype(BF16)
    r64 = lax.broadcasted_iota(jnp.int32, (CHUNK, CHUNK), 0)
    c64 = lax.broadcasted_iota(jnp.int32, (CHUNK, CHUNK), 1)
    lower_ones = jnp.where(c64 <= r64, 1.0, 0.0).astype(BF16)
    mm = functools.partial(_wide_mm, bd_mask=bd_mask)

    def conv_silu(x_ref, x_halo_ref, seg, gr, first_group):
        cols = slice(seg * BRANCH, (seg + 1) * BRANCH)
        cur = x_ref[gr, :]
        if first_group:
            halo = x_halo_ref[...]
            halo = jnp.where(pl.program_id(1) == 0, jnp.zeros((), halo.dtype), halo)
        else:
            halo = x_ref[gr.start - HALO:gr.start, :]
        conv = cur.astype(F32) * cw_ref[CONV_WIDTH - 1:CONV_WIDTH, cols]
        for s in range(1, CONV_WIDTH):
            shifted = _dot(shift_ref[s - 1], cur)
            head = shifted[:HALO] + _dot(hsel_ref[s - 1], halo)
            shifted = jnp.concatenate([head, shifted[HALO:]], axis=0)
            conv = conv + shifted * cw_ref[CONV_WIDTH - 1 - s:CONV_WIDTH - s, cols]
        return _silu(conv)

    def front(g):
        gr = slice(g * grp, (g + 1) * grp)
        q = _head_l2norm(conv_silu(q_ref, qh_ref, 0, gr, g == 0)) * (GDN_K_DIM ** -0.5)
        yield
        k = _head_l2norm(conv_silu(k_ref, kh_ref, 1, gr, g == 0))
        yield
        v = conv_silu(v_ref, vh_ref, 2, gr, g == 0)
        yield

        ab = ab_ref[gr, :]
        xg = ab + dtb_ref[...]
        softplus = jnp.maximum(xg, 0.0) + jnp.log1p(jnp.exp(-jnp.abs(xg)))
        g_all = -jnp.exp(alog_ref[...]) * softplus
        beta_all = _sigmoid(ab)

        staged = []
        for c in range(chunks):
            rows = slice(c * CHUNK, (c + 1) * CHUNK)
            qc, kc, vc = q[rows], k[rows], v[rows]
            g_c = g_all[rows]
            g_hi, g_lo = _split2(g_c)
            gc2 = _dot(lower_ones, jnp.concatenate([g_hi, g_lo], axis=1))
            gc_all = gc2[:, :LANES] + gc2[:, LANES:]
            gcb = _head_bcast(gc_all, GDN_K_DIM)
            betab = _head_bcast(beta_all[rows, GDN_HEADS:], GDN_K_DIM)
            exp_gc = jnp.exp(gcb)
            gl_row = exp_gc[CHUNK - 1:CHUNK]
            k_beta = kc * betab
            v_beta = vc * betab
            kbe = k_beta * exp_gc
            q_dec = qc * exp_gc
            k_tail = kc * jnp.exp(gcb[CHUNK - 1:CHUNK] - gcb)

            r_w = jnp.where(tri_strict, _head_bcast(g_c, CHUNK), 0.0)
            r_hi, r_lo = _split2(r_w)
            d2 = _dot(lower_ones, jnp.concatenate([r_hi, r_lo], axis=1))
            gamma = jnp.exp(jnp.where(tri_incl, d2[:, :WIDE] + d2[:, WIDE:], NEG_BIG))

            k16 = kc.astype(BF16)
            kbd = jnp.concatenate([k16] * GDN_HEADS, axis=0) * kbd_mask
            sc = _dot_nt(jnp.concatenate([k_beta.astype(BF16), qc.astype(BF16)], axis=0), kbd)
            a_strict = jnp.where(tri_strict, sc[:CHUNK] * gamma, 0.0)
            a_intra = jnp.where(tri_incl, sc[CHUNK:] * gamma, 0.0)
            staged.append((a_strict, a_intra, v_beta, kbe, q_dec, k_tail, gl_row))
            yield
        return staged

    def back(g, staged):
        t_inv = yield from _wide_unit_lower_inverse([st[0] for st in staged], eye_w, blk16, mm)
        pairs = [(c, h) for c in range(chunks) for h in range(GDN_HEADS)]
        hsl = lambda h: slice(h * GDN_K_DIM, (h + 1) * GDN_K_DIM)
        wsl = lambda h: slice(h * CHUNK, (h + 1) * CHUNK)
        uw16 = {}
        for c, h in pairs:
            _, _, v_beta, kbe, _, _, _ = staged[c]
            t_off = t_inv[c] - eye_w
            rhs = jnp.concatenate([v_beta[:, hsl(h)], kbe[:, hsl(h)]], axis=1)
            uw = rhs + _dot(t_off[:, wsl(h)].astype(BF16), rhs.astype(BF16))
            uw16[c, h] = uw.astype(BF16)
        yield
        q16, o0 = {}, {}
        for c, h in pairs:
            a_intra, q_dec = staged[c][1], staged[c][4]
            ai_uw = _dot(a_intra[:, wsl(h)].astype(BF16), uw16[c, h])
            q16[c, h] = (q_dec[:, hsl(h)] - ai_uw[:, GDN_V_DIM:]).astype(BF16)
            o0[c, h] = ai_uw[:, :GDN_V_DIM]
        yield
        ops = {}
        for c, h in pairs:
            k_tail = staged[c][5]
            kt_uw = _dot_tn(k_tail[:, hsl(h)].astype(BF16), uw16[c, h])
            pq16 = jnp.concatenate([kt_uw[:, GDN_V_DIM:].astype(BF16), q16[c, h]], axis=0)
            ops[c, h] = (pq16, kt_uw[:, :GDN_V_DIM], o0[c, h], staged[c][6][:, hsl(h)])
        yield
        return ops

    def scan(g, ops):
        hsl = lambda h: slice(h * GDN_K_DIM, (h + 1) * GDN_K_DIM)
        for c in range(chunks):
            rows = slice((g * chunks + c) * CHUNK, (g * chunks + c + 1) * CHUNK)
            for h in range(GDN_HEADS):
                pq16, n_c, o0_c, gl = ops[c, h]
                state = state_ref[h]
                r = _dot(pq16, state.astype(BF16))
                state_ref[h] = state * gl - r[:GDN_K_DIM] + n_c
                o = r[GDN_K_DIM:] + o0_c
                ms = jnp.mean(o * o, axis=-1, keepdims=True)
                y = o * lax.rsqrt(ms + EPS) * nw_ref[...]
                o_ref[rows, hsl(h)] = (y * z_ref[rows, hsl(h)].astype(F32)).astype(o_ref.dtype)
            yield

    groups = tr // grp
    staged, = _run_interleaved(front(0))
    ops = None
    for g in range(groups):
        nxt = front(g + 1) if g + 1 < groups else None
        prev_scan = scan(g - 1, ops) if g > 0 else None
        ops, staged, _ = _run_interleaved(back(g, staged), nxt, prev_scan)
    _run_interleaved(scan(groups - 1, ops))


def _gdn(qb, kb, vb, ab, zb, conv_w, alog_pad, dtb_pad, norm_w, tr, grp):
    b, s, _ = qb.shape
    assert tr % grp == 0 and grp % CHUNK == 0
    blk = lambda bi, i: (bi, i, 0)
    halo = lambda bi, i: (bi, jnp.maximum(i * (tr // HALO) - 1, 0), 0)
    const = lambda bi, i: (0, 0)
    const3 = lambda bi, i: (0, 0, 0)
    t_idx = jnp.arange(grp)
    h_idx = jnp.arange(HALO)
    shift = jnp.stack([(t_idx[:, None] - sft == t_idx[None, :]) for sft in range(1, CONV_WIDTH)]).astype(BF16)
    hsel = jnp.stack([(h_idx[:, None] - sft + HALO == h_idx[None, :])
                      for sft in range(1, CONV_WIDTH)]).astype(BF16)
    return pl.pallas_call(
        _gdn_kernel,
        grid=(b, s // tr),
        in_specs=[
            pl.BlockSpec((None, tr, BRANCH), blk),
            pl.BlockSpec((None, tr, BRANCH), blk),
            pl.BlockSpec((None, tr, BRANCH), blk),
            pl.BlockSpec((None, HALO, BRANCH), halo),
            pl.BlockSpec((None, HALO, BRANCH), halo),
            pl.BlockSpec((None, HALO, BRANCH), halo),
            pl.BlockSpec((None, tr, LANES), blk),
            pl.BlockSpec((None, tr, BRANCH), blk),
            pl.BlockSpec(conv_w.shape, const),
            pl.BlockSpec((1, LANES), const),
            pl.BlockSpec((1, LANES), const),
            pl.BlockSpec((1, GDN_V_DIM), const),
            pl.BlockSpec(shift.shape, const3),
            pl.BlockSpec(hsel.shape, const3),
        ],
        out_specs=pl.BlockSpec((None, tr, BRANCH), blk),
        out_shape=jax.ShapeDtypeStruct((b, s, BRANCH), BF16),
        scratch_shapes=[pltpu.VMEM((GDN_HEADS, GDN_K_DIM, GDN_V_DIM), F32)],
        compiler_params=pltpu.CompilerParams(
            dimension_semantics=("arbitrary", "arbitrary"), vmem_limit_bytes=VMEM_LIMIT),
        name="gdn",
    )(qb, kb, vb, qb, kb, vb, ab, zb, conv_w, alog_pad, dtb_pad, norm_w, shift, hsel)


def _out_proj_kernel(x_ref, ya_ref, yb_ref, gate_ref, wa_ref, wb_ref, wo_ref, o_ref):
    d = x_ref.shape[-1]
    up_a = _dot(ya_ref[...], wa_ref[...])
    up_b = _dot(yb_ref[...], wb_ref[...])
    merged = (gate_ref[:, :d].astype(F32) * up_a + gate_ref[:, d:].astype(F32) * up_b)
    o_ref[...] = x_ref[...] + _dot(merged.astype(BF16), wo_ref[...])


def _out_proj(x2d, ya, yb, gates, w_up_a, w_up_b, w_out, tm):
    n, d = x2d.shape
    const = lambda i: (0, 0)
    row = lambda i: (i, 0)
    return pl.pallas_call(
        _out_proj_kernel,
        grid=(n // tm,),
        in_specs=[
            pl.BlockSpec((tm, d), row),
            pl.BlockSpec((tm, BRANCH), row),
            pl.BlockSpec((tm, BRANCH), row),
            pl.BlockSpec((tm, 2 * d), row),
            pl.BlockSpec(w_up_a.shape, const),
            pl.BlockSpec(w_up_b.shape, const),
            pl.BlockSpec(w_out.shape, const),
        ],
        out_specs=pl.BlockSpec((tm, d), row),
        out_shape=jax.ShapeDtypeStruct((n, d), x2d.dtype),
        compiler_params=pltpu.CompilerParams(
            dimension_semantics=("arbitrary",), vmem_limit_bytes=VMEM_LIMIT),
        name="out_proj",
    )(x2d, ya, yb, gates, w_up_a, w_up_b, w_out)


def _lambda_init_for(layer):
    return 0.8 - 0.6 * math.exp(-0.3 * layer)


def kernel(x, norm_w, w_in, da_q_gain, da_k_gain, da_lambda, da_out_gain, gdn_conv_w, gdn_a_log,
           gdn_dt_bias, gdn_norm_w, w_up_da, w_up_gdn, w_out):
    b, s, d = x.shape
    depth = norm_w.shape[0]
    in_proj_tile = 1024
    out_proj_tile = 1024
    attn_q_tile = 1024
    attn_k_tile = 256
    gdn_tile = 1024
    gdn_group = 256
    group = jnp.arange(BRANCH) // DA_HEAD_DIM
    gsum = (group[:, None] == group[None, :]).astype(BF16)
    groups_per_seg = BRANCH // DA_HEAD_DIM
    q_scale = DA_HEAD_DIM ** -0.5 * math.log2(math.e)

    x2d = x.reshape(b * s, d)
    for l in range(depth):
        w_feat = w_in[l].T
        n_blocked = w_feat.shape[0] // BRANCH * BRANCH
        w_t = _weight_prep(w_feat, n_blocked)
        w_last = jnp.pad(w_feat[n_blocked:], ((0, -(w_feat.shape[0] - n_blocked) % LANES), (0, 0))).astype(BF16)
        q_gain = jnp.tile(da_q_gain[l] * q_scale, groups_per_seg)[None, :]
        k_gain = jnp.tile(da_k_gain[l], groups_per_seg)[None, :]
        qa, ka, vat, za, qb, kb, vb, zb, ab, gates = _in_proj(
            x2d, norm_w[l][None, :], w_t, w_last, q_gain, k_gain, gsum, in_proj_tile)

        r3 = lambda t: t.reshape(b, s, t.shape[-1])
        ya = _diff_attn(r3(qa), r3(ka), vat, r3(za), da_lambda[l], da_out_gain[l][None, :],
                        _lambda_init_for(l), attn_q_tile, attn_k_tile)

        alog_pad = jnp.pad(gdn_a_log[l], (0, LANES - GDN_HEADS))[None, :]
        dtb_pad = jnp.pad(gdn_dt_bias[l], (0, LANES - GDN_HEADS))[None, :]
        yb = _gdn(r3(qb), r3(kb), r3(vb), r3(ab), r3(zb), gdn_conv_w[l], alog_pad, dtb_pad,
                  gdn_norm_w[l][None, :], gdn_tile, gdn_group)

        x2d = _out_proj(x2d, ya.reshape(b * s, -1), yb.reshape(b * s, -1), gates,
                        w_up_da[l].astype(BF16), w_up_gdn[l].astype(BF16), w_out[l].astype(BF16),
                        out_proj_tile)
    return x2d.reshape(b, s, d)
```

```python
import functools
import math

import jax
import jax.numpy as jnp
from jax import lax
from jax.experimental import pallas as pl
from jax.experimental.pallas import tpu as pltpu

F32 = jnp.float32
BF16 = jnp.bfloat16

CHUNK = 64
EPS = 1e-6
DA_HEADS = 4
DA_HEAD_DIM = 64
DA_V_DIM = 2 * DA_HEAD_DIM
GDN_HEADS = 4
GDN_K_DIM = 128
GDN_V_DIM = 128
CONV_WIDTH = 4
BRANCH = 512
LANES = 128
NEG_BIG = -1e30
ONES_ROWS = 16
UNROLL_BLOCKS = 2
VMEM_LIMIT = 56 * 1024 * 1024


def _dot(a, b, precision=None):
    return jnp.dot(a, b, preferred_element_type=F32, precision=precision)


def _dot_nt(a, b, precision=None):
    return lax.dot_general(a, b, (((1,), (1,)), ((), ())),
                           preferred_element_type=F32, precision=precision)


def _dot_tn(a, b, precision=None):
    return lax.dot_general(a, b, (((0,), (0,)), ((), ())),
                           preferred_element_type=F32, precision=precision)


def _sigmoid(x):
    return 1.0 / (1.0 + jnp.exp(-x))


def _silu(x):
    return x * _sigmoid(x)


def _weight_prep_kernel(w_ref, wb_ref):
    wb_ref[...] = w_ref[...].astype(BF16)


def _weight_prep(wt, n_rows):
    d = wt.shape[1]
    blk = pl.BlockSpec((BRANCH, d), lambda j: (j, 0))
    return pl.pallas_call(
        _weight_prep_kernel,
        grid=(n_rows // BRANCH,),
        in_specs=[blk],
        out_specs=blk,
        out_shape=jax.ShapeDtypeStruct((n_rows, d), BF16),
        compiler_params=pltpu.CompilerParams(
            dimension_semantics=("arbitrary",), vmem_limit_bytes=VMEM_LIMIT),
        name="weight_prep",
    )(wt)


def _in_proj_kernel(x_ref, nw_ref, wt_ref, wlast_ref, qg_ref, kg_ref, gsum_ref,
                    qa_ref, ka_ref, vat_ref, za_ref, qb_ref, kb_ref, vb_ref, zb_ref, ab_ref, gate_ref):
    x = x_ref[...]
    ms = jnp.mean(x * x, axis=-1, keepdims=True)
    h = (x * lax.rsqrt(ms + EPS) * nw_ref[...]).astype(BF16)

    def proj(j):
        return _dot_nt(h, wt_ref[j * BRANCH:(j + 1) * BRANCH, :])

    def qk_norm(t, gain):
        ssq = _dot((t * t).astype(BF16), gsum_ref[...])
        return (t * lax.rsqrt(ssq * (1.0 / DA_HEAD_DIM) + EPS) * gain).astype(BF16)

    qa_ref[...] = qk_norm(proj(0), qg_ref[...])
    ka_ref[...] = qk_norm(proj(1), kg_ref[...])
    vat_ref[...] = _dot_nt(wt_ref[2 * BRANCH:3 * BRANCH, :], h).astype(BF16)
    za_ref[...] = _silu(proj(3)).astype(BF16)
    qb_ref[...] = proj(4).astype(BF16)
    kb_ref[...] = proj(5).astype(BF16)
    vb_ref[...] = proj(6).astype(BF16)
    zb_ref[...] = _silu(proj(7)).astype(BF16)
    tail = jnp.concatenate([_dot_nt(h, wt_ref[8 * BRANCH:, :]), _dot_nt(h, wlast_ref[...])], axis=1)
    ab_ref[...] = tail[:, :LANES]
    n_ab = 2 * GDN_HEADS
    gate_ref[...] = _sigmoid(tail[:, n_ab:n_ab + gate_ref.shape[1]]).astype(BF16)


def _in_proj(x2d, norm_w, w_t, w_last, q_gain, k_gain, gsum, tm):
    n, d = x2d.shape
    const = lambda i: (0, 0)
    row = lambda i: (i, 0)
    seg = jax.ShapeDtypeStruct((n, BRANCH), BF16)
    seg_spec = pl.BlockSpec((tm, BRANCH), row)
    out_shape = [seg, seg, jax.ShapeDtypeStruct((BRANCH, n), BF16)] + [seg] * 5 + [
        jax.ShapeDtypeStruct((n, LANES), F32), jax.ShapeDtypeStruct((n, 2 * d), BF16)]
    out_specs = ([seg_spec, seg_spec, pl.BlockSpec((BRANCH, tm), lambda i: (0, i))]
                 + [seg_spec] * 5
                 + [pl.BlockSpec((tm, LANES), row), pl.BlockSpec((tm, 2 * d), row)])
    return pl.pallas_call(
        _in_proj_kernel,
        grid=(n // tm,),
        in_specs=[
            pl.BlockSpec((tm, d), row),
            pl.BlockSpec((1, d), const),
            pl.BlockSpec(w_t.shape, const, pipeline_mode=pl.Buffered(1)),
            pl.BlockSpec(w_last.shape, const, pipeline_mode=pl.Buffered(1)),
            pl.BlockSpec((1, BRANCH), const),
            pl.BlockSpec((1, BRANCH), const),
            pl.BlockSpec((BRANCH, BRANCH), const, pipeline_mode=pl.Buffered(1)),
        ],
        out_specs=out_specs,
        out_shape=out_shape,
        compiler_params=pltpu.CompilerParams(
            dimension_semantics=("arbitrary",), vmem_limit_bytes=VMEM_LIMIT),
        name="in_proj",
    )(x2d, norm_w, w_t, w_last, q_gain, k_gain, gsum)


def _diff_attn_kernel(lam_ref, bias_ref, q_ref, k_ref, vt_ref, z_ref, og_ref, o_ref, s_scr, m_scr, acc_scr,
                      *, tq, tk, lambda_init):
    i = pl.program_id(2)
    q = q_ref[...]
    lane = lax.broadcasted_iota(jnp.int32, q.shape, 1)
    zero = jnp.zeros_like(q)
    qq = jnp.concatenate([jnp.where(lane < DA_HEAD_DIM, q, zero),
                          jnp.where(lane >= DA_HEAD_DIM, q, zero)], axis=0)
    sub_tiles = s_scr.shape[0]
    ones_rows = jnp.ones((ONES_ROWS, tk), BF16)
    m_scr[...] = jnp.full(m_scr.shape, NEG_BIG, F32)
    acc_scr[...] = jnp.zeros(acc_scr.shape, F32)

    def issue(start, slot, q0=0):
        wq = tq - q0
        qsel = qq if q0 == 0 else jnp.concatenate([qq[q0:tq], qq[tq + q0:]], axis=0)
        s = _dot_nt(k_ref[pl.ds(start, tk), :], qsel)
        s_scr[slot, :, :2 * wq] = s
        return jnp.max(s, axis=0, keepdims=True)

    def consume(start, slot, tile_max, masked, q0=0):
        wq = tq - q0
        s = s_scr[slot, :, :2 * wq]
        if masked:
            bias = bias_ref[...]
            pieces = []
            for g in range(2):
                pieces.append(s[:, g * wq:g * wq + tk] + bias)
                if wq > tk:
                    pieces.append(s[:, g * wq + tk:(g + 1) * wq])
            s = jnp.concatenate(pieces, axis=1)
            tile_max = jnp.max(s, axis=0, keepdims=True)
        m_cols = [slice(g * tq + q0, (g + 1) * tq) for g in range(2)]
        m = jnp.concatenate([m_scr[:, mc] for mc in m_cols], axis=1)
        m_new = jnp.maximum(m, tile_max)
        alpha = jnp.exp2(m - m_new)
        p = jnp.exp2(s - m_new)
        pb = p.astype(BF16)
        vt = jnp.concatenate([vt_ref[:, pl.ds(start, tk)], ones_rows], axis=0)
        for g in range(2):
            cols = slice(g * wq, (g + 1) * wq)
            m_scr[:, m_cols[g]] = m_new[:, cols]
            acc_scr[g, :, q0:] = alpha[:, cols] * acc_scr[g, :, q0:] + _dot(vt, pb[:, cols])

    def make_body(blocks):
        def body(jj, tile_max):
            for u in range(blocks):
                for t in range(sub_tiles):
                    start = pl.multiple_of((jj * blocks + u) * tq + t * tk, tk)
                    next_max = issue(start + tk, (t + 1) % sub_tiles)
                    consume(start, t, tile_max, masked=False)
                    tile_max = next_max
            return tile_max
        return body

    paired = i // UNROLL_BLOCKS
    tile_max = lax.fori_loop(0, paired, make_body(UNROLL_BLOCKS), issue(0, 0))
    tile_max = lax.fori_loop(paired * UNROLL_BLOCKS, i, make_body(1), tile_max)
    for t in range(sub_tiles):
        start = pl.multiple_of(i * tq + t * tk, tk)
        if t + 1 < sub_tiles:
            issue(start + tk, t + 1, q0=(t + 1) * tk)
        consume(start, t, None, masked=True, q0=t * tk)
    acc1 = acc_scr[0]
    acc2 = acc_scr[1]

    lam_p = lam_ref[...]
    lam = (jnp.exp(jnp.sum(lam_p[0:1] * lam_p[1:2], axis=-1, keepdims=True))
           - jnp.exp(jnp.sum(lam_p[2:3] * lam_p[3:4], axis=-1, keepdims=True))
           + lambda_init)
    ot = (acc1[:DA_V_DIM] / acc1[DA_V_DIM:DA_V_DIM + 1]
          - lam * (acc2[:DA_V_DIM] / acc2[DA_V_DIM:DA_V_DIM + 1]))
    ms = jnp.mean(ot * ot, axis=0, keepdims=True)
    y = (ot * lax.rsqrt(ms + EPS)).T * og_ref[...] * (1.0 - lambda_init)
    o_ref[...] = (y * z_ref[...].astype(F32)).astype(o_ref.dtype)


def _diff_attn(qa, ka, vat, za, da_lambda, out_gain, lambda_init, tq, tk):
    b, s, _ = qa.shape
    sub_tiles = tq // tk
    assert sub_tiles >= 2 and sub_tiles * tk == tq
    blk = lambda bi, h, i: (bi, i, h)
    const = lambda bi, h, i: (0, 0)
    chunk_of = jnp.arange(tk) // CHUNK
    bias = jnp.where(chunk_of[:, None] <= chunk_of[None, :], 0.0, NEG_BIG).astype(F32)
    return pl.pallas_call(
        functools.partial(_diff_attn_kernel, tq=tq, tk=tk, lambda_init=lambda_init),
        scratch_shapes=[pltpu.VMEM((sub_tiles, tk, 2 * tq), F32),
                        pltpu.VMEM((1, 2 * tq), F32),
                        pltpu.VMEM((2, DA_V_DIM + ONES_ROWS, tq), F32)],
        grid=(b, DA_HEADS, s // tq),
        in_specs=[
            pl.BlockSpec(da_lambda.shape, const),
            pl.BlockSpec(bias.shape, const),
            pl.BlockSpec((None, tq, DA_V_DIM), blk),
            pl.BlockSpec((None, s, DA_V_DIM), lambda bi, h, i: (bi, 0, h)),
            pl.BlockSpec((DA_V_DIM, s), lambda bi, h, i: (h, bi)),
            pl.BlockSpec((None, tq, DA_V_DIM), blk),
            pl.BlockSpec((1, DA_V_DIM), const),
        ],
        out_specs=pl.BlockSpec((None, tq, DA_V_DIM), blk),
        out_shape=jax.ShapeDtypeStruct((b, s, DA_HEADS * DA_V_DIM), BF16),
        compiler_params=pltpu.CompilerParams(
            dimension_semantics=("arbitrary", "arbitrary", "arbitrary"),
            vmem_limit_bytes=VMEM_LIMIT),
        name="diff_attn",
    )(da_lambda, bias, qa, ka, vat, za, out_gain)


WIDE = GDN_HEADS * CHUNK
HALO = 16


def _split2(x):
    hi = x.astype(BF16)
    lo = (x - hi.astype(F32)).astype(BF16)
    return hi, lo


def _block_diag(y, mask01):
    return jnp.concatenate([y] * GDN_HEADS, axis=0) * mask01


def _wide_mm(x, y, bd_mask):
    return _dot(x.astype(BF16), _block_diag(y.astype(BF16), bd_mask))


def _wide_unit_lower_inverse(a_list, eye, blk16, mm):
    d = [jnp.where(blk16, a, 0.0) for a in a_list]
    off = [a - di for a, di in zip(a_list, d)]
    p = [-di for di in d]
    t = [eye + pi for pi in p]
    for _ in range(3):
        p = [mm(pi, pi) for pi in p]
        yield
        t = [ti + mm(ti, pi) for ti, pi in zip(t, p)]
        yield
    e = [mm(ti, oi) for ti, oi in zip(t, off)]
    yield
    e2 = [mm(ei, ei) for ei in e]
    yield
    f = [eye - ei + e2i - mm(ei, e2i) for ei, e2i in zip(e, e2)]
    yield
    return [mm(fi, ti) for fi, ti in zip(f, t)]


def _run_interleaved(*gens):
    results = [None] * len(gens)
    alive = [g is not None for g in gens]
    while any(alive):
        for n, gen in enumerate(gens):
            if alive[n]:
                try:
                    next(gen)
                except StopIteration as stop:
                    results[n] = stop.value
                    alive[n] = False
    return results


def _head_l2norm(x):
    parts = []
    for h in range(GDN_HEADS):
        xh = x[:, h * GDN_K_DIM:(h + 1) * GDN_K_DIM]
        parts.append(xh * lax.rsqrt(jnp.sum(xh * xh, axis=-1, keepdims=True) + EPS))
    return jnp.concatenate(parts, axis=1)


def _head_bcast(col4, width):
    rows = col4.shape[0]
    return jnp.concatenate([jnp.broadcast_to(col4[:, h:h + 1], (rows, width))
                            for h in range(GDN_HEADS)], axis=1)


def _gdn_kernel(q_ref, k_ref, v_ref, qh_ref, kh_ref, vh_ref, ab_ref, z_ref, cw_ref, alog_ref, dtb_ref,
                nw_ref, shift_ref, hsel_ref, o_ref, state_ref):
    tr = q_ref.shape[0]

    @pl.when(pl.program_id(1) == 0)
    def _():
        state_ref[...] = jnp.zeros_like(state_ref)

    grp = shift_ref.shape[1]
    chunks = grp // CHUNK

    row_w = lax.broadcasted_iota(jnp.int32, (CHUNK, WIDE), 0)
    col_w = lax.broadcasted_iota(jnp.int32, (CHUNK, WIDE), 1) % CHUNK
    tri_incl = col_w <= row_w
    tri_strict = col_w < row_w
    eye_w = jnp.where(col_w == row_w, 1.0, 0.0).astype(F32)
    blk16 = (row_w // 16) == (col_w // 16)
    bd_mask = jnp.where(lax.broadcasted_iota(jnp.int32, (WIDE, WIDE), 0) // CHUNK
                        == lax.broadcasted_iota(jnp.int32, (WIDE, WIDE), 1) // CHUNK,
                        1.0, 0.0).astype(BF16)
    kbd_mask = jnp.where(lax.broadcasted_iota(jnp.int32, (WIDE, BRANCH), 0) // CHUNK
                         == lax.broadcasted_iota(jnp.int32, (WIDE, BRANCH), 1) // GDN_K_DIM,
                         1.0, 0.0).astype(BF16)
    r64 = lax.broadcasted_iota(jnp.int32, (CHUNK, CHUNK), 0)
    c64 = lax.broadcasted_iota(jnp.int32, (CHUNK, CHUNK), 1)
    lower_ones = jnp.where(c64 <= r64, 1.0, 0.0).astype(BF16)
    mm = functools.partial(_wide_mm, bd_mask=bd_mask)

    def conv_silu(x_ref, x_halo_ref, seg, gr, first_group):
        cols = slice(seg * BRANCH, (seg + 1) * BRANCH)
        cur = x_ref[gr, :]
        if first_group:
            halo = x_halo_ref[...]
            halo = jnp.where(pl.program_id(1) == 0, jnp.zeros((), halo.dtype), halo)
        else:
            halo = x_ref[gr.start - HALO:gr.start, :]
        conv = cur.astype(F32) * cw_ref[CONV_WIDTH - 1:CONV_WIDTH, cols]
        for s in range(1, CONV_WIDTH):
            shifted = _dot(shift_ref[s - 1], cur)
            head = shifted[:HALO] + _dot(hsel_ref[s - 1], halo)
            shifted = jnp.concatenate([head, shifted[HALO:]], axis=0)
            conv = conv + shifted * cw_ref[CONV_WIDTH - 1 - s:CONV_WIDTH - s, cols]
        return _silu(conv)

    def front(g):
        gr = slice(g * grp, (g + 1) * grp)
        q = _head_l2norm(conv_silu(q_ref, qh_ref, 0, gr, g == 0)) * (GDN_K_DIM ** -0.5)
        yield
        k = _head_l2norm(conv_silu(k_ref, kh_ref, 1, gr, g == 0))
        yield
        v = conv_silu(v_ref, vh_ref, 2, gr, g == 0)
        yield

        ab = ab_ref[gr, :]
        xg = ab + dtb_ref[...]
        softplus = jnp.maximum(xg, 0.0) + jnp.log1p(jnp.exp(-jnp.abs(xg)))
        g_all = -jnp.exp(alog_ref[...]) * softplus
        beta_all = _sigmoid(ab)

        staged = []
        for c in range(chunks):
            rows = slice(c * CHUNK, (c + 1) * CHUNK)
            qc, kc, vc = q[rows], k[rows], v[rows]
            g_c = g_all[rows]
            g_hi, g_lo = _split2(g_c)
            gc2 = _dot(lower_ones, jnp.concatenate([g_hi, g_lo], axis=1))
            gc_all = gc2[:, :LANES] + gc2[:, LANES:]
            gcb = _head_bcast(gc_all, GDN_K_DIM)
            betab = _head_bcast(beta_all[rows, GDN_HEADS:], GDN_K_DIM)
            exp_gc = jnp.exp(gcb)
            gl_row = exp_gc[CHUNK - 1:CHUNK]
            k_beta = kc * betab
            v_beta = vc * betab
            kbe = k_beta * exp_gc
            q_dec = qc * exp_gc
            k_tail = kc * jnp.exp(gcb[CHUNK - 1:CHUNK] - gcb)

            r_w = jnp.where(tri_strict, _head_bcast(g_c, CHUNK), 0.0)
            r_hi, r_lo = _split2(r_w)
            d2 = _dot(lower_ones, jnp.concatenate([r_hi, r_lo], axis=1))
            gamma = jnp.exp(jnp.where(tri_incl, d2[:, :WIDE] + d2[:, WIDE:], NEG_BIG))

            k16 = kc.astype(BF16)
            kbd = jnp.concatenate([k16] * GDN_HEADS, axis=0) * kbd_mask
            sc = _dot_nt(jnp.concatenate([k_beta.astype(BF16), qc.astype(BF16)], axis=0), kbd)
            a_strict = jnp.where(tri_strict, sc[:CHUNK] * gamma, 0.0)
            a_intra = jnp.where(tri_incl, sc[CHUNK:] * gamma, 0.0)
            staged.append((a_strict, a_intra, v_beta, kbe, q_dec, k_tail, gl_row))
            yield
        return staged

    def back(g, staged):
        t_inv = yield from _wide_unit_lower_inverse([st[0] for st in staged], eye_w, blk16, mm)
        pairs = [(c, h) for c in range(chunks) for h in range(GDN_HEADS)]
        hsl = lambda h: slice(h * GDN_K_DIM, (h + 1) * GDN_K_DIM)
        wsl = lambda h: slice(h * CHUNK, (h + 1) * CHUNK)
        uw16 = {}
        for c, h in pairs:
            _, _, v_beta, kbe, _, _, _ = staged[c]
            t_off = t_inv[c] - eye_w
            rhs = jnp.concatenate([v_beta[:, hsl(h)], kbe[:, hsl(h)]], axis=1)
            uw = rhs + _dot(t_off[:, wsl(h)].astype(BF16), rhs.astype(BF16))
            uw16[c, h] = uw.astype(BF16)
        yield
        q16, o0 = {}, {}
        for c, h in pairs:
            a_intra, q_dec = staged[c][1], staged[c][4]
            ai_uw = _dot(a_intra[:, wsl(h)].astype(BF16), uw16[c, h])
            q16[c, h] = (q_dec[:, hsl(h)] - ai_uw[:, GDN_V_DIM:]).astype(BF16)
            o0[c, h] = ai_uw[:, :GDN_V_DIM]
        yield
        ops = {}
        for c, h in pairs:
            k_tail = staged[c][5]
            kt_uw = _dot_tn(k_tail[:, hsl(h)].astype(BF16), uw16[c, h])
            pq16 = jnp.concatenate([kt_uw[:, GDN_V_DIM:].astype(BF16), q16[c, h]], axis=0)
            ops[c, h] = (pq16, kt_uw[:, :GDN_V_DIM], o0[c, h], staged[c][6][:, hsl(h)])
        yield
        return ops

    def scan(g, ops):
        hsl = lambda h: slice(h * GDN_K_DIM, (h + 1) * GDN_K_DIM)
        for c in range(chunks):
            rows = slice((g * chunks + c) * CHUNK, (g * chunks + c + 1) * CHUNK)
            for h in range(GDN_HEADS):
                pq16, n_c, o0_c, gl = ops[c, h]
                state = state_ref[h]
                r = _dot(pq16, state.astype(BF16))
                state_ref[h] = state * gl - r[:GDN_K_DIM] + n_c
                o = r[GDN_K_DIM:] + o0_c
                ms = jnp.mean(o * o, axis=-1, keepdims=True)
                y = o * lax.rsqrt(ms + EPS) * nw_ref[...]
                o_ref[rows, hsl(h)] = (y * z_ref[rows, hsl(h)].astype(F32)).astype(o_ref.dtype)
            yield

    groups = tr // grp
    staged, = _run_interleaved(front(0))
    ops = None
    for g in range(groups):
        nxt = front(g + 1) if g + 1 < groups else None
        prev_scan = scan(g - 1, ops) if g > 0 else None
        ops, staged, _ = _run_interleaved(back(g, staged), nxt, prev_scan)
    _run_interleaved(scan(groups - 1, ops))


def _gdn(qb, kb, vb, ab, zb, conv_w, alog_pad, dtb_pad, norm_w, tr, grp):
    b, s, _ = qb.shape
    assert tr % grp == 0 and grp % CHUNK == 0
    blk = lambda bi, i: (bi, i, 0)
    halo = lambda bi, i: (bi, jnp.maximum(i * (tr // HALO) - 1, 0), 0)
    const = lambda bi, i: (0, 0)
    const3 = lambda bi, i: (0, 0, 0)
    t_idx = jnp.arange(grp)
    h_idx = jnp.arange(HALO)
    shift = jnp.stack([(t_idx[:, None] - sft == t_idx[None, :]) for sft in range(1, CONV_WIDTH)]).astype(BF16)
    hsel = jnp.stack([(h_idx[:, None] - sft + HALO == h_idx[None, :])
                      for sft in range(1, CONV_WIDTH)]).astype(BF16)
    return pl.pallas_call(
        _gdn_kernel,
        grid=(b, s // tr),
        in_specs=[
            pl.BlockSpec((None, tr, BRANCH), blk),
            pl.BlockSpec((None, tr, BRANCH), blk),
            pl.BlockSpec((None, tr, BRANCH), blk),
            pl.BlockSpec((None, HALO, BRANCH), halo),
            pl.BlockSpec((None, HALO, BRANCH), halo),
            pl.BlockSpec((None, HALO, BRANCH), halo),
            pl.BlockSpec((None, tr, LANES), blk),
            pl.BlockSpec((None, tr, BRANCH), blk),
            pl.BlockSpec(conv_w.shape, const),
            pl.BlockSpec((1, LANES), const),
            pl.BlockSpec((1, LANES), const),
            pl.BlockSpec((1, GDN_V_DIM), const),
            pl.BlockSpec(shift.shape, const3),
            pl.BlockSpec(hsel.shape, const3),
        ],
        out_specs=pl.BlockSpec((None, tr, BRANCH), blk),
        out_shape=jax.ShapeDtypeStruct((b, s, BRANCH), BF16),
        scratch_shapes=[pltpu.VMEM((GDN_HEADS, GDN_K_DIM, GDN_V_DIM), F32)],
        compiler_params=pltpu.CompilerParams(
            dimension_semantics=("arbitrary", "arbitrary"), vmem_limit_bytes=VMEM_LIMIT),
        name="gdn",
    )(qb, kb, vb, qb, kb, vb, ab, zb, conv_w, alog_pad, dtb_pad, norm_w, shift, hsel)


def _out_proj_kernel(x_ref, ya_ref, yb_ref, gate_ref, wa_ref, wb_ref, wo_ref, o_ref):
    d = x_ref.shape[-1]
    up_a = _dot(ya_ref[...], wa_ref[...])
    up_b = _dot(yb_ref[...], wb_ref[...])
    merged = (gate_ref[:, :d].astype(F32) * up_a + gate_ref[:, d:].astype(F32) * up_b)
    o_ref[...] = x_ref[...] + _dot(merged.astype(BF16), wo_ref[...])


def _out_proj(x2d, ya, yb, gates, w_up_a, w_up_b, w_out, tm):
    n, d = x2d.shape
    const = lambda i: (0, 0)
    row = lambda i: (i, 0)
    return pl.pallas_call(
        _out_proj_kernel,
        grid=(n // tm,),
        in_specs=[
            pl.BlockSpec((tm, d), row),
            pl.BlockSpec((tm, BRANCH), row),
            pl.BlockSpec((tm, BRANCH), row),
            pl.BlockSpec((tm, 2 * d), row),
            pl.BlockSpec(w_up_a.shape, const),
            pl.BlockSpec(w_up_b.shape, const),
            pl.BlockSpec(w_out.shape, const),
        ],
        out_specs=pl.BlockSpec((tm, d), row),
        out_shape=jax.ShapeDtypeStruct((n, d), x2d.dtype),
        compiler_params=pltpu.CompilerParams(
            dimension_semantics=("arbitrary",), vmem_limit_bytes=VMEM_LIMIT),
        name="out_proj",
    )(x2d, ya, yb, gates, w_up_a, w_up_b, w_out)


def _lambda_init_for(layer):
    return 0.8 - 0.6 * math.exp(-0.3 * layer)


def kernel(x, norm_w, w_in, da_q_gain, da_k_gain, da_lambda, da_out_gain, gdn_conv_w, gdn_a_log,
           gdn_dt_bias, gdn_norm_w, w_up_da, w_up_gdn, w_out):
    b, s, d = x.shape
    depth = norm_w.shape[0]
    in_proj_tile = 1024
    out_proj_tile = 1024
    attn_q_tile = 1024
    attn_k_tile = 512
    gdn_tile = 512
    gdn_group = 256
    group = jnp.arange(BRANCH) // DA_HEAD_DIM
    gsum = (group[:, None] == group[None, :]).astype(BF16)
    groups_per_seg = BRANCH // DA_HEAD_DIM
    q_scale = DA_HEAD_DIM ** -0.5 * math.log2(math.e)

    x2d = x.reshape(b * s, d)
    for l in range(depth):
        w_feat = w_in[l].T
        n_blocked = w_feat.shape[0] // BRANCH * BRANCH
        w_t = _weight_prep(w_feat, n_blocked)
        w_last = jnp.pad(w_feat[n_blocked:], ((0, -(w_feat.shape[0] - n_blocked) % LANES), (0, 0))).astype(BF16)
        q_gain = jnp.tile(da_q_gain[l] * q_scale, groups_per_seg)[None, :]
        k_gain = jnp.tile(da_k_gain[l], groups_per_seg)[None, :]
        qa, ka, vat, za, qb, kb, vb, zb, ab, gates = _in_proj(
            x2d, norm_w[l][None, :], w_t, w_last, q_gain, k_gain, gsum, in_proj_tile)

        r3 = lambda t: t.reshape(b, s, t.shape[-1])
        ya = _diff_attn(r3(qa), r3(ka), vat, r3(za), da_lambda[l], da_out_gain[l][None, :],
                        _lambda_init_for(l), attn_q_tile, attn_k_tile)

        alog_pad = jnp.pad(gdn_a_log[l], (0, LANES - GDN_HEADS))[None, :]
        dtb_pad = jnp.pad(gdn_dt_bias[l], (0, LANES - GDN_HEADS))[None, :]
        yb = _gdn(r3(qb), r3(kb), r3(vb), r3(ab), r3(zb), gdn_conv_w[l], alog_pad, dtb_pad,
                  gdn_norm_w[l][None, :], gdn_tile, gdn_group)

        x2d = _out_proj(x2d, ya.reshape(b * s, -1), yb.reshape(b * s, -1), gates,
                        w_up_da[l].astype(BF16), w_up_gdn[l].astype(BF16), w_out[l].astype(BF16),
                        out_proj_tile)
    return x2d.reshape(b, s, d)
```

```python
import functools
import math

import jax
import jax.numpy as jnp
from jax import lax
from jax.experimental import pallas as pl
from jax.experimental.pallas import tpu as pltpu

F32 = jnp.float32
BF16 = jnp.bfloat16

CHUNK = 64
EPS = 1e-6
DA_HEADS = 4
DA_HEAD_DIM = 64
DA_V_DIM = 2 * DA_HEAD_DIM
GDN_HEADS = 4
GDN_K_DIM = 128
GDN_V_DIM = 128
CONV_WIDTH = 4
BRANCH = 512
LANES = 128
NEG_BIG = -1e30
ONES_ROWS = 16
UNROLL_BLOCKS = 2
VMEM_LIMIT = 56 * 1024 * 1024


def _dot(a, b, precision=None):
    return jnp.dot(a, b, preferred_element_type=F32, precision=precision)


def _dot_nt(a, b, precision=None):
    return lax.dot_general(a, b, (((1,), (1,)), ((), ())),
                           preferred_element_type=F32, precision=precision)


def _dot_tn(a, b, precision=None):
    return lax.dot_general(a, b, (((0,), (0,)), ((), ())),
                           preferred_element_type=F32, precision=precision)


def _sigmoid(x):
    return 1.0 / (1.0 + jnp.exp(-x))


def _silu(x):
    return x * _sigmoid(x)


def _weight_prep_kernel(w_ref, wb_ref):
    wb_ref[...] = w_ref[...].astype(BF16)


def _weight_prep(wt, n_rows):
    d = wt.shape[1]
    blk = pl.BlockSpec((BRANCH, d), lambda j: (j, 0))
    return pl.pallas_call(
        _weight_prep_kernel,
        grid=(n_rows // BRANCH,),
        in_specs=[blk],
        out_specs=blk,
        out_shape=jax.ShapeDtypeStruct((n_rows, d), BF16),
        compiler_params=pltpu.CompilerParams(
            dimension_semantics=("arbitrary",), vmem_limit_bytes=VMEM_LIMIT),
        name="weight_prep",
    )(wt)


def _in_proj_kernel(x_ref, nw_ref, wt_ref, wlast_ref, qg_ref, kg_ref,
                    qa_ref, ka_ref, vat_ref, za_ref, qb_ref, kb_ref, vb_ref, zb_ref, ab_ref, gate_ref):
    x = x_ref[...]
    ms = jnp.mean(x * x, axis=-1, keepdims=True)
    h = (x * lax.rsqrt(ms + EPS) * nw_ref[...]).astype(BF16)

    def proj(j):
        return _dot_nt(h, wt_ref[j * BRANCH:(j + 1) * BRANCH, :])

    def qk_norm(t, gain):
        sq = t * t
        low = lax.broadcasted_iota(jnp.int32, (t.shape[0], LANES), 1) < DA_HEAD_DIM
        parts = []
        for j in range(t.shape[1] // LANES):
            blk = sq[:, j * LANES:(j + 1) * LANES]
            s_low = jnp.sum(jnp.where(low, blk, 0.0), axis=-1, keepdims=True)
            s_all = jnp.sum(blk, axis=-1, keepdims=True)
            parts.append(jnp.where(low, s_low, s_all - s_low))
        ssq = jnp.concatenate(parts, axis=1)
        return (t * lax.rsqrt(ssq * (1.0 / DA_HEAD_DIM) + EPS) * gain).astype(BF16)

    qa_ref[...] = qk_norm(proj(0), qg_ref[...])
    ka_ref[...] = qk_norm(proj(1), kg_ref[...])
    vat_ref[...] = _dot_nt(wt_ref[2 * BRANCH:3 * BRANCH, :], h).astype(BF16)
    za_ref[...] = _silu(proj(3)).astype(BF16)
    qb_ref[...] = proj(4).astype(BF16)
    kb_ref[...] = proj(5).astype(BF16)
    vb_ref[...] = proj(6).astype(BF16)
    zb_ref[...] = _silu(proj(7)).astype(BF16)
    tail = jnp.concatenate([_dot_nt(h, wt_ref[8 * BRANCH:, :]), _dot_nt(h, wlast_ref[...])], axis=1)
    ab_ref[...] = tail[:, :LANES]
    n_ab = 2 * GDN_HEADS
    gate_ref[...] = _sigmoid(tail[:, n_ab:n_ab + gate_ref.shape[1]]).astype(BF16)


def _in_proj(x2d, norm_w, w_t, w_last, q_gain, k_gain, tm):
    n, d = x2d.shape
    const = lambda i: (0, 0)
    row = lambda i: (i, 0)
    seg = jax.ShapeDtypeStruct((n, BRANCH), BF16)
    seg_spec = pl.BlockSpec((tm, BRANCH), row)
    out_shape = [seg, seg, jax.ShapeDtypeStruct((BRANCH, n), BF16)] + [seg] * 5 + [
        jax.ShapeDtypeStruct((n, LANES), F32), jax.ShapeDtypeStruct((n, 2 * d), BF16)]
    out_specs = ([seg_spec, seg_spec, pl.BlockSpec((BRANCH, tm), lambda i: (0, i))]
                 + [seg_spec] * 5
                 + [pl.BlockSpec((tm, LANES), row), pl.BlockSpec((tm, 2 * d), row)])
    return pl.pallas_call(
        _in_proj_kernel,
        grid=(n // tm,),
        in_specs=[
            pl.BlockSpec((tm, d), row),
            pl.BlockSpec((1, d), const),
            pl.BlockSpec(w_t.shape, const, pipeline_mode=pl.Buffered(1)),
            pl.BlockSpec(w_last.shape, const, pipeline_mode=pl.Buffered(1)),
            pl.BlockSpec((1, BRANCH), const),
            pl.BlockSpec((1, BRANCH), const),
        ],
        out_specs=out_specs,
        out_shape=out_shape,
        compiler_params=pltpu.CompilerParams(
            dimension_semantics=("arbitrary",), vmem_limit_bytes=VMEM_LIMIT),
        name="in_proj",
    )(x2d, norm_w, w_t, w_last, q_gain, k_gain)


def _diff_attn_kernel(lam_ref, bias_ref, q_ref, k_ref, vt_ref, z_ref, og_ref, o_ref, s_scr, m_scr, acc_scr,
                      *, tq, tk, lambda_init):
    i = pl.program_id(2)
    q = q_ref[...]
    lane = lax.broadcasted_iota(jnp.int32, q.shape, 1)
    zero = jnp.zeros_like(q)
    qq = jnp.concatenate([jnp.where(lane < DA_HEAD_DIM, q, zero),
                          jnp.where(lane >= DA_HEAD_DIM, q, zero)], axis=0)
    sub_tiles = s_scr.shape[0]
    ones_rows = jnp.ones((ONES_ROWS, tk), BF16)
    m_scr[...] = jnp.full(m_scr.shape, NEG_BIG, F32)
    acc_scr[...] = jnp.zeros(acc_scr.shape, F32)

    def issue(start, slot, q0=0):
        wq = tq - q0
        qsel = qq if q0 == 0 else jnp.concatenate([qq[q0:tq], qq[tq + q0:]], axis=0)
        s = _dot_nt(k_ref[pl.ds(start, tk), :], qsel)
        s_scr[slot, :, :2 * wq] = s
        return jnp.max(s, axis=0, keepdims=True)

    def consume(start, slot, tile_max, masked, q0=0):
        wq = tq - q0
        s = s_scr[slot, :, :2 * wq]
        if masked:
            bias = bias_ref[...]
            pieces = []
            for g in range(2):
                pieces.append(s[:, g * wq:g * wq + tk] + bias)
                if wq > tk:
                    pieces.append(s[:, g * wq + tk:(g + 1) * wq])
            s = jnp.concatenate(pieces, axis=1)
            tile_max = jnp.max(s, axis=0, keepdims=True)
        m_cols = [slice(g * tq + q0, (g + 1) * tq) for g in range(2)]
        m = jnp.concatenate([m_scr[:, mc] for mc in m_cols], axis=1)
        m_new = jnp.maximum(m, tile_max)
        alpha = jnp.exp2(m - m_new)
        p = jnp.exp2(s - m_new)
        pb = p.astype(BF16)
        vt = jnp.concatenate([vt_ref[:, pl.ds(start, tk)], ones_rows], axis=0)
        for g in range(2):
            cols = slice(g * wq, (g + 1) * wq)
            m_scr[:, m_cols[g]] = m_new[:, cols]
            acc_scr[g, :, q0:] = alpha[:, cols] * acc_scr[g, :, q0:] + _dot(vt, pb[:, cols])

    def make_body(blocks):
        def body(jj, tile_max):
            for u in range(blocks):
                for t in range(sub_tiles):
                    start = pl.multiple_of((jj * blocks + u) * tq + t * tk, tk)
                    next_max = issue(start + tk, (t + 1) % sub_tiles)
                    consume(start, t, tile_max, masked=False)
                    tile_max = next_max
            return tile_max
        return body

    paired = i // UNROLL_BLOCKS
    tile_max = lax.fori_loop(0, paired, make_body(UNROLL_BLOCKS), issue(0, 0))
    tile_max = lax.fori_loop(paired * UNROLL_BLOCKS, i, make_body(1), tile_max)
    for t in range(sub_tiles):
        start = pl.multiple_of(i * tq + t * tk, tk)
        if t + 1 < sub_tiles:
            issue(start + tk, t + 1, q0=(t + 1) * tk)
        consume(start, t, None, masked=True, q0=t * tk)
    acc1 = acc_scr[0]
    acc2 = acc_scr[1]

    lam_p = lam_ref[...]
    lam = (jnp.exp(jnp.sum(lam_p[0:1] * lam_p[1:2], axis=-1, keepdims=True))
           - jnp.exp(jnp.sum(lam_p[2:3] * lam_p[3:4], axis=-1, keepdims=True))
           + lambda_init)
    ot = (acc1[:DA_V_DIM] / acc1[DA_V_DIM:DA_V_DIM + 1]
          - lam * (acc2[:DA_V_DIM] / acc2[DA_V_DIM:DA_V_DIM + 1]))
    ms = jnp.mean(ot * ot, axis=0, keepdims=True)
    y = (ot * lax.rsqrt(ms + EPS)).T * og_ref[...] * (1.0 - lambda_init)
    o_ref[...] = (y * z_ref[...].astype(F32)).astype(o_ref.dtype)


def _diff_attn(qa, ka, vat, za, da_lambda, out_gain, lambda_init, tq, tk):
    b, s, _ = qa.shape
    sub_tiles = tq // tk
    assert sub_tiles >= 2 and sub_tiles * tk == tq
    blk = lambda bi, h, i: (bi, i, h)
    const = lambda bi, h, i: (0, 0)
    chunk_of = jnp.arange(tk) // CHUNK
    bias = jnp.where(chunk_of[:, None] <= chunk_of[None, :], 0.0, NEG_BIG).astype(F32)
    return pl.pallas_call(
        functools.partial(_diff_attn_kernel, tq=tq, tk=tk, lambda_init=lambda_init),
        scratch_shapes=[pltpu.VMEM((sub_tiles, tk, 2 * tq), F32),
                        pltpu.VMEM((1, 2 * tq), F32),
                        pltpu.VMEM((2, DA_V_DIM + ONES_ROWS, tq), F32)],
        grid=(b, DA_HEADS, s // tq),
        in_specs=[
            pl.BlockSpec(da_lambda.shape, const),
            pl.BlockSpec(bias.shape, const),
            pl.BlockSpec((None, tq, DA_V_DIM), blk),
            pl.BlockSpec((None, s, DA_V_DIM), lambda bi, h, i: (bi, 0, h)),
            pl.BlockSpec((DA_V_DIM, s), lambda bi, h, i: (h, bi)),
            pl.BlockSpec((None, tq, DA_V_DIM), blk),
            pl.BlockSpec((1, DA_V_DIM), const),
        ],
        out_specs=pl.BlockSpec((None, tq, DA_V_DIM), blk),
        out_shape=jax.ShapeDtypeStruct((b, s, DA_HEADS * DA_V_DIM), BF16),
        compiler_params=pltpu.CompilerParams(
            dimension_semantics=("arbitrary", "arbitrary", "arbitrary"),
            vmem_limit_bytes=VMEM_LIMIT),
        name="diff_attn",
    )(da_lambda, bias, qa, ka, vat, za, out_gain)


WIDE = GDN_HEADS * CHUNK
HALO = 16


def _split2(x):
    hi = x.astype(BF16)
    lo = (x - hi.astype(F32)).astype(BF16)
    return hi, lo


def _block_diag(y, mask01):
    return jnp.concatenate([y] * GDN_HEADS, axis=0) * mask01


def _wide_mm(x, y, bd_mask):
    return _dot(x.astype(BF16), _block_diag(y.astype(BF16), bd_mask))


def _wide_unit_lower_inverse(a_list, eye, blk16, mm):
    d = [jnp.where(blk16, a, 0.0) for a in a_list]
    off = [a - di for a, di in zip(a_list, d)]
    p = [-di for di in d]
    t = [eye + pi for pi in p]
    for _ in range(3):
        p = [mm(pi, pi) for pi in p]
        yield
        t = [ti + mm(ti, pi) for ti, pi in zip(t, p)]
        yield
    e = [mm(ti, oi) for ti, oi in zip(t, off)]
    yield
    e2 = [mm(ei, ei) for ei in e]
    yield
    f = [eye - ei + e2i - mm(ei, e2i) for ei, e2i in zip(e, e2)]
    yield
    return [mm(fi, ti) for fi, ti in zip(f, t)]


def _run_interleaved(*gens):
    results = [None] * len(gens)
    alive = [g is not None for g in gens]
    while any(alive):
        for n, gen in enumerate(gens):
            if alive[n]:
                try:
                    next(gen)
                except StopIteration as stop:
                    results[n] = stop.value
                    alive[n] = False
    return results


def _head_l2norm(x):
    parts = []
    for h in range(GDN_HEADS):
        xh = x[:, h * GDN_K_DIM:(h + 1) * GDN_K_DIM]
        parts.append(xh * lax.rsqrt(jnp.sum(xh * xh, axis=-1, keepdims=True) + EPS))
    return jnp.concatenate(parts, axis=1)


def _head_bcast(col4, width):
    rows = col4.shape[0]
    return jnp.concatenate([jnp.broadcast_to(col4[:, h:h + 1], (rows, width))
                            for h in range(GDN_HEADS)], axis=1)


def _gdn_kernel(q_ref, k_ref, v_ref, qh_ref, kh_ref, vh_ref, ab_ref, z_ref, cw_ref, alog_ref, dtb_ref,
                nw_ref, shift_ref, hsel_ref, o_ref, state_ref):
    tr = q_ref.shape[0]

    @pl.when(pl.program_id(1) == 0)
    def _():
        state_ref[...] = jnp.zeros_like(state_ref)

    grp = shift_ref.shape[1]
    chunks = grp // CHUNK

    row_w = lax.broadcasted_iota(jnp.int32, (CHUNK, WIDE), 0)
    col_w = lax.broadcasted_iota(jnp.int32, (CHUNK, WIDE), 1) % CHUNK
    tri_incl = col_w <= row_w
    tri_strict = col_w < row_w
    eye_w = jnp.where(col_w == row_w, 1.0, 0.0).astype(F32)
    blk16 = (row_w // 16) == (col_w // 16)
    bd_mask = jnp.where(lax.broadcasted_iota(jnp.int32, (WIDE, WIDE), 0) // CHUNK
                        == lax.broadcasted_iota(jnp.int32, (WIDE, WIDE), 1) // CHUNK,
                        1.0, 0.0).astype(BF16)
    kbd_mask = jnp.where(lax.broadcasted_iota(jnp.int32, (WIDE, BRANCH), 0) // CHUNK
                         == lax.broadcasted_iota(jnp.int32, (WIDE, BRANCH), 1) // GDN_K_DIM,
                         1.0, 0.0).astype(BF16)
    r64 = lax.broadcasted_iota(jnp.int32, (CHUNK, CHUNK), 0)
    c64 = lax.broadcasted_iota(jnp.int32, (CHUNK, CHUNK), 1)
    lower_ones = jnp.where(c64 <= r64, 1.0, 0.0).astype(BF16)
    mm = functools.partial(_wide_mm, bd_mask=bd_mask)

    def conv_silu(x_ref, x_halo_ref, seg, gr, first_group):
        cols = slice(seg * BRANCH, (seg + 1) * BRANCH)
        cur = x_ref[gr, :]
        if first_group:
            halo = x_halo_ref[...]
            halo = jnp.where(pl.program_id(1) == 0, jnp.zeros((), halo.dtype), halo)
        else:
            halo = x_ref[gr.start - HALO:gr.start, :]
        conv = cur.astype(F32) * cw_ref[CONV_WIDTH - 1:CONV_WIDTH, cols]
        for s in range(1, CONV_WIDTH):
            shifted = _dot(shift_ref[s - 1], cur)
            head = shifted[:HALO] + _dot(hsel_ref[s - 1], halo)
            shifted = jnp.concatenate([head, shifted[HALO:]], axis=0)
            conv = conv + shifted * cw_ref[CONV_WIDTH - 1 - s:CONV_WIDTH - s, cols]
        return _silu(conv)

    def front(g):
        gr = slice(g * grp, (g + 1) * grp)
        q = _head_l2norm(conv_silu(q_ref, qh_ref, 0, gr, g == 0)) * (GDN_K_DIM ** -0.5)
        yield
        k = _head_l2norm(conv_silu(k_ref, kh_ref, 1, gr, g == 0))
        yield
        v = conv_silu(v_ref, vh_ref, 2, gr, g == 0)
        yield

        ab = ab_ref[gr, :]
        xg = ab + dtb_ref[...]
        softplus = jnp.maximum(xg, 0.0) + jnp.log1p(jnp.exp(-jnp.abs(xg)))
        g_all = -jnp.exp(alog_ref[...]) * softplus
        beta_all = _sigmoid(ab)

        staged = []
        for c in range(chunks):
            rows = slice(c * CHUNK, (c + 1) * CHUNK)
            qc, kc, vc = q[rows], k[rows], v[rows]
            g_c = g_all[rows]
            g_hi, g_lo = _split2(g_c)
            gc2 = _dot(lower_ones, jnp.concatenate([g_hi, g_lo], axis=1))
            gc_all = gc2[:, :LANES] + gc2[:, LANES:]
            gcb = _head_bcast(gc_all, GDN_K_DIM)
            betab = _head_bcast(beta_all[rows, GDN_HEADS:], GDN_K_DIM)
            exp_gc = jnp.exp(gcb)
            gl_row = exp_gc[CHUNK - 1:CHUNK]
            k_beta = kc * betab
            v_beta = vc * betab
            kbe = k_beta * exp_gc
            q_dec = qc * exp_gc
            k_tail = kc * jnp.exp(gcb[CHUNK - 1:CHUNK] - gcb)

            r_w = jnp.where(tri_strict, _head_bcast(g_c, CHUNK), 0.0)
            r_hi, r_lo = _split2(r_w)
            d2 = _dot(lower_ones, jnp.concatenate([r_hi, r_lo], axis=1))
            gamma = jnp.exp(jnp.where(tri_incl, d2[:, :WIDE] + d2[:, WIDE:], NEG_BIG))

            k16 = kc.astype(BF16)
            kbd = jnp.concatenate([k16] * GDN_HEADS, axis=0) * kbd_mask
            sc = _dot_nt(jnp.concatenate([k_beta.astype(BF16), qc.astype(BF16)], axis=0), kbd)
            a_strict = jnp.where(tri_strict, sc[:CHUNK] * gamma, 0.0)
            a_intra = jnp.where(tri_incl, sc[CHUNK:] * gamma, 0.0)
            staged.append((a_strict, a_intra, v_beta, kbe, q_dec, k_tail, gl_row))
            yield
        return staged

    def back(g, staged):
        t_inv = yield from _wide_unit_lower_inverse([st[0] for st in staged], eye_w, blk16, mm)
        pairs = [(c, h) for c in range(chunks) for h in range(GDN_HEADS)]
        hsl = lambda h: slice(h * GDN_K_DIM, (h + 1) * GDN_K_DIM)
        wsl = lambda h: slice(h * CHUNK, (h + 1) * CHUNK)
        uw16 = {}
        for c, h in pairs:
            _, _, v_beta, kbe, _, _, _ = staged[c]
            t_off = t_inv[c] - eye_w
            rhs = jnp.concatenate([v_beta[:, hsl(h)], kbe[:, hsl(h)]], axis=1)
            uw = rhs + _dot(t_off[:, wsl(h)].astype(BF16), rhs.astype(BF16))
            uw16[c, h] = uw.astype(BF16)
        yield
        q16, o0 = {}, {}
        for c, h in pairs:
            a_intra, q_dec = staged[c][1], staged[c][4]
            ai_uw = _dot(a_intra[:, wsl(h)].astype(BF16), uw16[c, h])
            q16[c, h] = (q_dec[:, hsl(h)] - ai_uw[:, GDN_V_DIM:]).astype(BF16)
            o0[c, h] = ai_uw[:, :GDN_V_DIM]
        yield
        ops = {}
        for c, h in pairs:
            k_tail = staged[c][5]
            kt_uw = _dot_tn(k_tail[:, hsl(h)].astype(BF16), uw16[c, h])
            pq16 = jnp.concatenate([kt_uw[:, GDN_V_DIM:].astype(BF16), q16[c, h]], axis=0)
            ops[c, h] = (pq16, kt_uw[:, :GDN_V_DIM], o0[c, h], staged[c][6][:, hsl(h)])
        yield
        return ops

    def scan(g, ops):
        hsl = lambda h: slice(h * GDN_K_DIM, (h + 1) * GDN_K_DIM)
        for c in range(chunks):
            rows = slice((g * chunks + c) * CHUNK, (g * chunks + c + 1) * CHUNK)
            for h in range(GDN_HEADS):
                pq16, n_c, o0_c, gl = ops[c, h]
                state = state_ref[h]
                r = _dot(pq16, state.astype(BF16))
                state_ref[h] = state * gl - r[:GDN_K_DIM] + n_c
                o = r[GDN_K_DIM:] + o0_c
                ms = jnp.mean(o * o, axis=-1, keepdims=True)
                y = o * lax.rsqrt(ms + EPS) * nw_ref[...]
                o_ref[rows, hsl(h)] = (y * z_ref[rows, hsl(h)].astype(F32)).astype(o_ref.dtype)
            yield

    groups = tr // grp
    staged, = _run_interleaved(front(0))
    ops = None
    for g in range(groups):
        nxt = front(g + 1) if g + 1 < groups else None
        prev_scan = scan(g - 1, ops) if g > 0 else None
        ops, staged, _ = _run_interleaved(back(g, staged), nxt, prev_scan)
    _run_interleaved(scan(groups - 1, ops))


def _gdn(qb, kb, vb, ab, zb, conv_w, alog_pad, dtb_pad, norm_w, tr, grp):
    b, s, _ = qb.shape
    assert tr % grp == 0 and grp % CHUNK == 0
    blk = lambda bi, i: (bi, i, 0)
    halo = lambda bi, i: (bi, jnp.maximum(i * (tr // HALO) - 1, 0), 0)
    const = lambda bi, i: (0, 0)
    const3 = lambda bi, i: (0, 0, 0)
    t_idx = jnp.arange(grp)
    h_idx = jnp.arange(HALO)
    shift = jnp.stack([(t_idx[:, None] - sft == t_idx[None, :]) for sft in range(1, CONV_WIDTH)]).astype(BF16)
    hsel = jnp.stack([(h_idx[:, None] - sft + HALO == h_idx[None, :])
                      for sft in range(1, CONV_WIDTH)]).astype(BF16)
    return pl.pallas_call(
        _gdn_kernel,
        grid=(b, s // tr),
        in_specs=[
            pl.BlockSpec((None, tr, BRANCH), blk),
            pl.BlockSpec((None, tr, BRANCH), blk),
            pl.BlockSpec((None, tr, BRANCH), blk),
            pl.BlockSpec((None, HALO, BRANCH), halo),
            pl.BlockSpec((None, HALO, BRANCH), halo),
            pl.BlockSpec((None, HALO, BRANCH), halo),
            pl.BlockSpec((None, tr, LANES), blk),
            pl.BlockSpec((None, tr, BRANCH), blk),
            pl.BlockSpec(conv_w.shape, const),
            pl.BlockSpec((1, LANES), const),
            pl.BlockSpec((1, LANES), const),
            pl.BlockSpec((1, GDN_V_DIM), const),
            pl.BlockSpec(shift.shape, const3),
            pl.BlockSpec(hsel.shape, const3),
        ],
        out_specs=pl.BlockSpec((None, tr, BRANCH), blk),
        out_shape=jax.ShapeDtypeStruct((b, s, BRANCH), BF16),
        scratch_shapes=[pltpu.VMEM((GDN_HEADS, GDN_K_DIM, GDN_V_DIM), F32)],
        compiler_params=pltpu.CompilerParams(
            dimension_semantics=("arbitrary", "arbitrary"), vmem_limit_bytes=VMEM_LIMIT),
        name="gdn",
    )(qb, kb, vb, qb, kb, vb, ab, zb, conv_w, alog_pad, dtb_pad, norm_w, shift, hsel)


def _out_proj_kernel(x_ref, ya_ref, yb_ref, gate_ref, wa_ref, wb_ref, wo_ref, o_ref):
    d = x_ref.shape[-1]
    up_a = _dot(ya_ref[...], wa_ref[...])
    up_b = _dot(yb_ref[...], wb_ref[...])
    merged = (gate_ref[:, :d].astype(F32) * up_a + gate_ref[:, d:].astype(F32) * up_b)
    o_ref[...] = x_ref[...] + _dot(merged.astype(BF16), wo_ref[...])


def _out_proj(x2d, ya, yb, gates, w_up_a, w_up_b, w_out, tm):
    n, d = x2d.shape
    const = lambda i: (0, 0)
    row = lambda i: (i, 0)
    return pl.pallas_call(
        _out_proj_kernel,
        grid=(n // tm,),
        in_specs=[
            pl.BlockSpec((tm, d), row),
            pl.BlockSpec((tm, BRANCH), row),
            pl.BlockSpec((tm, BRANCH), row),
            pl.BlockSpec((tm, 2 * d), row),
            pl.BlockSpec(w_up_a.shape, const),
            pl.BlockSpec(w_up_b.shape, const),
            pl.BlockSpec(w_out.shape, const),
        ],
        out_specs=pl.BlockSpec((tm, d), row),
        out_shape=jax.ShapeDtypeStruct((n, d), x2d.dtype),
        compiler_params=pltpu.CompilerParams(
            dimension_semantics=("arbitrary",), vmem_limit_bytes=VMEM_LIMIT),
        name="out_proj",
    )(x2d, ya, yb, gates, w_up_a, w_up_b, w_out)


def _lambda_init_for(layer):
    return 0.8 - 0.6 * math.exp(-0.3 * layer)


def kernel(x, norm_w, w_in, da_q_gain, da_k_gain, da_lambda, da_out_gain, gdn_conv_w, gdn_a_log,
           gdn_dt_bias, gdn_norm_w, w_up_da, w_up_gdn, w_out):
    b, s, d = x.shape
    depth = norm_w.shape[0]
    in_proj_tile = 1024
    out_proj_tile = 1024
    attn_q_tile = 1024
    attn_k_tile = 256
    gdn_tile = 512
    gdn_group = 256
    groups_per_seg = BRANCH // DA_HEAD_DIM
    q_scale = DA_HEAD_DIM ** -0.5 * math.log2(math.e)

    x2d = x.reshape(b * s, d)
    for l in range(depth):
        w_feat = w_in[l].T
        n_blocked = w_feat.shape[0] // BRANCH * BRANCH
        w_t = _weight_prep(w_feat, n_blocked)
        w_last = jnp.pad(w_feat[n_blocked:], ((0, -(w_feat.shape[0] - n_blocked) % LANES), (0, 0))).astype(BF16)
        q_gain = jnp.tile(da_q_gain[l] * q_scale, groups_per_seg)[None, :]
        k_gain = jnp.tile(da_k_gain[l], groups_per_seg)[None, :]
        qa, ka, vat, za, qb, kb, vb, zb, ab, gates = _in_proj(
            x2d, norm_w[l][None, :], w_t, w_last, q_gain, k_gain, in_proj_tile)

        r3 = lambda t: t.reshape(b, s, t.shape[-1])
        ya = _diff_attn(r3(qa), r3(ka), vat, r3(za), da_lambda[l], da_out_gain[l][None, :],
                        _lambda_init_for(l), attn_q_tile, attn_k_tile)

        alog_pad = jnp.pad(gdn_a_log[l], (0, LANES - GDN_HEADS))[None, :]
        dtb_pad = jnp.pad(gdn_dt_bias[l], (0, LANES - GDN_HEADS))[None, :]
        yb = _gdn(r3(qb), r3(kb), r3(vb), r3(ab), r3(zb), gdn_conv_w[l], alog_pad, dtb_pad,
                  gdn_norm_w[l][None, :], gdn_tile, gdn_group)

        x2d = _out_proj(x2d, ya.reshape(b * s, -1), yb.reshape(b * s, -1), gates,
                        w_up_da[l].astype(BF16), w_up_gdn[l].astype(BF16), w_out[l].astype(BF16),
                        out_proj_tile)
    return x2d.reshape(b, s, d)
```

```python
import functools
import math

import jax
import jax.numpy as jnp
from jax import lax
from jax.experimental import pallas as pl
from jax.experimental.pallas import tpu as pltpu

F32 = jnp.float32
BF16 = jnp.bfloat16

CHUNK = 64
EPS = 1e-6
DA_HEADS = 4
DA_HEAD_DIM = 64
DA_V_DIM = 2 * DA_HEAD_DIM
GDN_HEADS = 4
GDN_K_DIM = 128
GDN_V_DIM = 128
CONV_WIDTH = 4
BRANCH = 512
LANES = 128
NEG_BIG = -1e30
ONES_ROWS = 16
UNROLL_BLOCKS = 2
VMEM_LIMIT = 56 * 1024 * 1024


def _dot(a, b, precision=None):
    return jnp.dot(a, b, preferred_element_type=F32, precision=precision)


def _dot_nt(a, b, precision=None):
    return lax.dot_general(a, b, (((1,), (1,)), ((), ())),
                           preferred_element_type=F32, precision=precision)


def _dot_tn(a, b, precision=None):
    return lax.dot_general(a, b, (((0,), (0,)), ((), ())),
                           preferred_element_type=F32, precision=precision)


def _sigmoid(x):
    return 1.0 / (1.0 + jnp.exp(-x))


def _silu(x):
    return x * _sigmoid(x)


def _weight_prep_kernel(w_ref, wb_ref):
    wb_ref[...] = w_ref[...].astype(BF16)


def _weight_prep(wt, n_rows):
    d = wt.shape[1]
    blk = pl.BlockSpec((BRANCH, d), lambda j: (j, 0))
    return pl.pallas_call(
        _weight_prep_kernel,
        grid=(n_rows // BRANCH,),
        in_specs=[blk],
        out_specs=blk,
        out_shape=jax.ShapeDtypeStruct((n_rows, d), BF16),
        compiler_params=pltpu.CompilerParams(
            dimension_semantics=("arbitrary",), vmem_limit_bytes=VMEM_LIMIT),
        name="weight_prep",
    )(wt)


def _in_proj_kernel(x_ref, nw_ref, wt_ref, wlast_ref, qg_ref, kg_ref,
                    qa_ref, ka_ref, vat_ref, za_ref, qb_ref, kb_ref, vb_ref, zb_ref, ab_ref, gate_ref):
    x = x_ref[...]
    ms = jnp.mean(x * x, axis=-1, keepdims=True)
    h = (x * lax.rsqrt(ms + EPS) * nw_ref[...]).astype(BF16)

    def proj(j):
        return _dot_nt(h, wt_ref[j * BRANCH:(j + 1) * BRANCH, :])

    def qk_norm(t, gain):
        sq = t * t
        low = lax.broadcasted_iota(jnp.int32, (t.shape[0], LANES), 1) < DA_HEAD_DIM
        parts = []
        for j in range(t.shape[1] // LANES):
            blk = sq[:, j * LANES:(j + 1) * LANES]
            s_low = jnp.sum(jnp.where(low, blk, 0.0), axis=-1, keepdims=True)
            s_all = jnp.sum(blk, axis=-1, keepdims=True)
            parts.append(jnp.where(low, s_low, s_all - s_low))
        ssq = jnp.concatenate(parts, axis=1)
        return (t * lax.rsqrt(ssq * (1.0 / DA_HEAD_DIM) + EPS) * gain).astype(BF16)

    qa_ref[...] = qk_norm(proj(0), qg_ref[...])
    ka_ref[...] = qk_norm(proj(1), kg_ref[...])
    vat_ref[...] = _dot_nt(wt_ref[2 * BRANCH:3 * BRANCH, :], h).astype(BF16)
    za_ref[...] = _silu(proj(3)).astype(BF16)
    qb_ref[...] = proj(4).astype(BF16)
    kb_ref[...] = proj(5).astype(BF16)
    vb_ref[...] = proj(6).astype(BF16)
    zb_ref[...] = _silu(proj(7)).astype(BF16)
    tail = jnp.concatenate([_dot_nt(h, wt_ref[8 * BRANCH:, :]), _dot_nt(h, wlast_ref[...])], axis=1)
    ab_ref[...] = tail[:, :LANES]
    n_ab = 2 * GDN_HEADS
    gate_ref[...] = _sigmoid(tail[:, n_ab:n_ab + gate_ref.shape[1]]).astype(BF16)


def _in_proj(x2d, norm_w, w_t, w_last, q_gain, k_gain, tm):
    n, d = x2d.shape
    const = lambda i: (0, 0)
    row = lambda i: (i, 0)
    seg = jax.ShapeDtypeStruct((n, BRANCH), BF16)
    seg_spec = pl.BlockSpec((tm, BRANCH), row)
    out_shape = [seg, seg, jax.ShapeDtypeStruct((BRANCH, n), BF16)] + [seg] * 5 + [
        jax.ShapeDtypeStruct((n, LANES), F32), jax.ShapeDtypeStruct((n, 2 * d), BF16)]
    out_specs = ([seg_spec, seg_spec, pl.BlockSpec((BRANCH, tm), lambda i: (0, i))]
                 + [seg_spec] * 5
                 + [pl.BlockSpec((tm, LANES), row), pl.BlockSpec((tm, 2 * d), row)])
    return pl.pallas_call(
        _in_proj_kernel,
        grid=(n // tm,),
        in_specs=[
            pl.BlockSpec((tm, d), row),
            pl.BlockSpec((1, d), const),
            pl.BlockSpec(w_t.shape, const, pipeline_mode=pl.Buffered(1)),
            pl.BlockSpec(w_last.shape, const, pipeline_mode=pl.Buffered(1)),
            pl.BlockSpec((1, BRANCH), const),
            pl.BlockSpec((1, BRANCH), const),
        ],
        out_specs=out_specs,
        out_shape=out_shape,
        compiler_params=pltpu.CompilerParams(
            dimension_semantics=("arbitrary",), vmem_limit_bytes=VMEM_LIMIT),
        name="in_proj",
    )(x2d, norm_w, w_t, w_last, q_gain, k_gain)


def _diff_attn_kernel(lam_ref, bias_ref, q_ref, k_ref, vt_ref, z_ref, og_ref, o_ref, s_scr, m_scr, acc_scr,
                      *, tq, tk, lambda_init):
    i = pl.program_id(2)
    q = q_ref[...]
    lane = lax.broadcasted_iota(jnp.int32, q.shape, 1)
    zero = jnp.zeros_like(q)
    qq = jnp.concatenate([jnp.where(lane < DA_HEAD_DIM, q, zero),
                          jnp.where(lane >= DA_HEAD_DIM, q, zero)], axis=0)
    qq_t = qq.T
    sub_tiles = s_scr.shape[0]
    ones_rows = jnp.ones((ONES_ROWS, tk), BF16)
    m_scr[...] = jnp.full(m_scr.shape, NEG_BIG, F32)
    acc_scr[...] = jnp.zeros(acc_scr.shape, F32)

    def issue(start, slot, q0=0):
        wq = tq - q0
        qsel = qq_t if q0 == 0 else jnp.concatenate([qq_t[:, q0:tq], qq_t[:, tq + q0:]], axis=1)
        s = _dot(k_ref[pl.ds(start, tk), :], qsel)
        s_scr[slot, :, :2 * wq] = s
        return jnp.max(s, axis=0, keepdims=True)

    def consume(start, slot, tile_max, masked, q0=0):
        wq = tq - q0
        s = s_scr[slot, :, :2 * wq]
        if masked:
            bias = bias_ref[...]
            pieces = []
            for g in range(2):
                pieces.append(s[:, g * wq:g * wq + tk] + bias)
                if wq > tk:
                    pieces.append(s[:, g * wq + tk:(g + 1) * wq])
            s = jnp.concatenate(pieces, axis=1)
            tile_max = jnp.max(s, axis=0, keepdims=True)
        m_cols = [slice(g * tq + q0, (g + 1) * tq) for g in range(2)]
        m = jnp.concatenate([m_scr[:, mc] for mc in m_cols], axis=1)
        m_new = jnp.maximum(m, tile_max)
        alpha = jnp.exp2(m - m_new)
        p = jnp.exp2(s - m_new)
        pb = p.astype(BF16)
        vt = jnp.concatenate([vt_ref[:, pl.ds(start, tk)], ones_rows], axis=0)
        for g in range(2):
            cols = slice(g * wq, (g + 1) * wq)
            m_scr[:, m_cols[g]] = m_new[:, cols]
            acc_scr[g, :, q0:] = alpha[:, cols] * acc_scr[g, :, q0:] + _dot(vt, pb[:, cols])

    def make_body(blocks):
        def body(jj, tile_max):
            for u in range(blocks):
                for t in range(sub_tiles):
                    start = pl.multiple_of((jj * blocks + u) * tq + t * tk, tk)
                    next_max = issue(start + tk, (t + 1) % sub_tiles)
                    consume(start, t, tile_max, masked=False)
                    tile_max = next_max
            return tile_max
        return body

    paired = i // UNROLL_BLOCKS
    tile_max = lax.fori_loop(0, paired, make_body(UNROLL_BLOCKS), issue(0, 0))
    tile_max = lax.fori_loop(paired * UNROLL_BLOCKS, i, make_body(1), tile_max)
    for t in range(sub_tiles):
        start = pl.multiple_of(i * tq + t * tk, tk)
        if t + 1 < sub_tiles:
            issue(start + tk, t + 1, q0=(t + 1) * tk)
        consume(start, t, None, masked=True, q0=t * tk)
    acc1 = acc_scr[0]
    acc2 = acc_scr[1]

    lam_p = lam_ref[...]
    lam = (jnp.exp(jnp.sum(lam_p[0:1] * lam_p[1:2], axis=-1, keepdims=True))
           - jnp.exp(jnp.sum(lam_p[2:3] * lam_p[3:4], axis=-1, keepdims=True))
           + lambda_init)
    ot = (acc1[:DA_V_DIM] / acc1[DA_V_DIM:DA_V_DIM + 1]
          - lam * (acc2[:DA_V_DIM] / acc2[DA_V_DIM:DA_V_DIM + 1]))
    ms = jnp.mean(ot * ot, axis=0, keepdims=True)
    y = (ot * lax.rsqrt(ms + EPS)).T * og_ref[...] * (1.0 - lambda_init)
    o_ref[...] = (y * z_ref[...].astype(F32)).astype(o_ref.dtype)


def _diff_attn(qa, ka, vat, za, da_lambda, out_gain, lambda_init, tq, tk):
    b, s, _ = qa.shape
    sub_tiles = tq // tk
    assert sub_tiles >= 2 and sub_tiles * tk == tq
    blk = lambda bi, h, i: (bi, i, h)
    const = lambda bi, h, i: (0, 0)
    chunk_of = jnp.arange(tk) // CHUNK
    bias = jnp.where(chunk_of[:, None] <= chunk_of[None, :], 0.0, NEG_BIG).astype(F32)
    return pl.pallas_call(
        functools.partial(_diff_attn_kernel, tq=tq, tk=tk, lambda_init=lambda_init),
        scratch_shapes=[pltpu.VMEM((sub_tiles, tk, 2 * tq), F32),
                        pltpu.VMEM((1, 2 * tq), F32),
                        pltpu.VMEM((2, DA_V_DIM + ONES_ROWS, tq), F32)],
        grid=(b, DA_HEADS, s // tq),
        in_specs=[
            pl.BlockSpec(da_lambda.shape, const),
            pl.BlockSpec(bias.shape, const),
            pl.BlockSpec((None, tq, DA_V_DIM), blk),
            pl.BlockSpec((None, s, DA_V_DIM), lambda bi, h, i: (bi, 0, h)),
            pl.BlockSpec((DA_V_DIM, s), lambda bi, h, i: (h, bi)),
            pl.BlockSpec((None, tq, DA_V_DIM), blk),
            pl.BlockSpec((1, DA_V_DIM), const),
        ],
        out_specs=pl.BlockSpec((None, tq, DA_V_DIM), blk),
        out_shape=jax.ShapeDtypeStruct((b, s, DA_HEADS * DA_V_DIM), BF16),
        compiler_params=pltpu.CompilerParams(
            dimension_semantics=("arbitrary", "arbitrary", "arbitrary"),
            vmem_limit_bytes=VMEM_LIMIT),
        name="diff_attn",
    )(da_lambda, bias, qa, ka, vat, za, out_gain)


WIDE = GDN_HEADS * CHUNK
HALO = 16


def _split2(x):
    hi = x.astype(BF16)
    lo = (x - hi.astype(F32)).astype(BF16)
    return hi, lo


def _block_diag(y, mask01):
    return jnp.concatenate([y] * GDN_HEADS, axis=0) * mask01


def _wide_mm(x, y, bd_mask):
    return _dot(x.astype(BF16), _block_diag(y.astype(BF16), bd_mask))


def _wide_unit_lower_inverse(a_list, eye, blk16, mm):
    d = [jnp.where(blk16, a, 0.0) for a in a_list]
    off = [a - di for a, di in zip(a_list, d)]
    p = [-di for di in d]
    t = [eye + pi for pi in p]
    for _ in range(3):
        p = [mm(pi, pi) for pi in p]
        yield
        t = [ti + mm(ti, pi) for ti, pi in zip(t, p)]
        yield
    e = [mm(ti, oi) for ti, oi in zip(t, off)]
    yield
    e2 = [mm(ei, ei) for ei in e]
    yield
    f = [eye - ei + e2i - mm(ei, e2i) for ei, e2i in zip(e, e2)]
    yield
    return [mm(fi, ti) for fi, ti in zip(f, t)]


def _run_interleaved(*gens):
    results = [None] * len(gens)
    alive = [g is not None for g in gens]
    while any(alive):
        for n, gen in enumerate(gens):
            if alive[n]:
                try:
                    next(gen)
                except StopIteration as stop:
                    results[n] = stop.value
                    alive[n] = False
    return results


def _head_l2norm(x):
    parts = []
    for h in range(GDN_HEADS):
        xh = x[:, h * GDN_K_DIM:(h + 1) * GDN_K_DIM]
        parts.append(xh * lax.rsqrt(jnp.sum(xh * xh, axis=-1, keepdims=True) + EPS))
    return jnp.concatenate(parts, axis=1)


def _head_bcast(col4, width):
    rows = col4.shape[0]
    return jnp.concatenate([jnp.broadcast_to(col4[:, h:h + 1], (rows, width))
                            for h in range(GDN_HEADS)], axis=1)


def _gdn_kernel(q_ref, k_ref, v_ref, qh_ref, kh_ref, vh_ref, ab_ref, z_ref, cw_ref, alog_ref, dtb_ref,
                nw_ref, shift_ref, hsel_ref, o_ref, state_ref):
    tr = q_ref.shape[0]

    @pl.when(pl.program_id(1) == 0)
    def _():
        state_ref[...] = jnp.zeros_like(state_ref)

    grp = shift_ref.shape[1]
    chunks = grp // CHUNK

    row_w = lax.broadcasted_iota(jnp.int32, (CHUNK, WIDE), 0)
    col_w = lax.broadcasted_iota(jnp.int32, (CHUNK, WIDE), 1) % CHUNK
    tri_incl = col_w <= row_w
    tri_strict = col_w < row_w
    eye_w = jnp.where(col_w == row_w, 1.0, 0.0).astype(F32)
    blk16 = (row_w // 16) == (col_w // 16)
    bd_mask = jnp.where(lax.broadcasted_iota(jnp.int32, (WIDE, WIDE), 0) // CHUNK
                        == lax.broadcasted_iota(jnp.int32, (WIDE, WIDE), 1) // CHUNK,
                        1.0, 0.0).astype(BF16)
    kbd_mask = jnp.where(lax.broadcasted_iota(jnp.int32, (WIDE, BRANCH), 0) // CHUNK
                         == lax.broadcasted_iota(jnp.int32, (WIDE, BRANCH), 1) // GDN_K_DIM,
                         1.0, 0.0).astype(BF16)
    r64 = lax.broadcasted_iota(jnp.int32, (CHUNK, CHUNK), 0)
    c64 = lax.broadcasted_iota(jnp.int32, (CHUNK, CHUNK), 1)
    lower_ones = jnp.where(c64 <= r64, 1.0, 0.0).astype(BF16)
    mm = functools.partial(_wide_mm, bd_mask=bd_mask)

    def conv_silu(x_ref, x_halo_ref, seg, gr, first_group):
        cols = slice(seg * BRANCH, (seg + 1) * BRANCH)
        cur = x_ref[gr, :]
        if first_group:
            halo = x_halo_ref[...]
            halo = jnp.where(pl.program_id(1) == 0, jnp.zeros((), halo.dtype), halo)
        else:
            halo = x_ref[gr.start - HALO:gr.start, :]
        conv = cur.astype(F32) * cw_ref[CONV_WIDTH - 1:CONV_WIDTH, cols]
        for s in range(1, CONV_WIDTH):
            shifted = _dot(shift_ref[s - 1], cur)
            head = shifted[:HALO] + _dot(hsel_ref[s - 1], halo)
            shifted = jnp.concatenate([head, shifted[HALO:]], axis=0)
            conv = conv + shifted * cw_ref[CONV_WIDTH - 1 - s:CONV_WIDTH - s, cols]
        return _silu(conv)

    def front(g):
        gr = slice(g * grp, (g + 1) * grp)
        q = _head_l2norm(conv_silu(q_ref, qh_ref, 0, gr, g == 0)) * (GDN_K_DIM ** -0.5)
        yield
        k = _head_l2norm(conv_silu(k_ref, kh_ref, 1, gr, g == 0))
        yield
        v = conv_silu(v_ref, vh_ref, 2, gr, g == 0)
        yield

        ab = ab_ref[gr, :]
        xg = ab + dtb_ref[...]
        softplus = jnp.maximum(xg, 0.0) + jnp.log1p(jnp.exp(-jnp.abs(xg)))
        g_all = -jnp.exp(alog_ref[...]) * softplus
        beta_all = _sigmoid(ab)

        staged = []
        for c in range(chunks):
            rows = slice(c * CHUNK, (c + 1) * CHUNK)
            qc, kc, vc = q[rows], k[rows], v[rows]
            g_c = g_all[rows]
            g_hi, g_lo = _split2(g_c)
            gc2 = _dot(lower_ones, jnp.concatenate([g_hi, g_lo], axis=1))
            gc_all = gc2[:, :LANES] + gc2[:, LANES:]
            gcb = _head_bcast(gc_all, GDN_K_DIM)
            betab = _head_bcast(beta_all[rows, GDN_HEADS:], GDN_K_DIM)
            exp_gc = jnp.exp(gcb)
            gl_row = exp_gc[CHUNK - 1:CHUNK]
            k_beta = kc * betab
            v_beta = vc * betab
            kbe = k_beta * exp_gc
            q_dec = qc * exp_gc
            k_tail = kc * jnp.exp(gcb[CHUNK - 1:CHUNK] - gcb)

            r_w = jnp.where(tri_strict, _head_bcast(g_c, CHUNK), 0.0)
            r_hi, r_lo = _split2(r_w)
            d2 = _dot(lower_ones, jnp.concatenate([r_hi, r_lo], axis=1))
            gamma = jnp.exp(jnp.where(tri_incl, d2[:, :WIDE] + d2[:, WIDE:], NEG_BIG))

            k16 = kc.astype(BF16)
            kbd = jnp.concatenate([k16] * GDN_HEADS, axis=0) * kbd_mask
            sc = _dot_nt(jnp.concatenate([k_beta.astype(BF16), qc.astype(BF16)], axis=0), kbd)
            a_strict = jnp.where(tri_strict, sc[:CHUNK] * gamma, 0.0)
            a_intra = jnp.where(tri_incl, sc[CHUNK:] * gamma, 0.0)
            staged.append((a_strict, a_intra, v_beta, kbe, q_dec, k_tail, gl_row))
            yield
        return staged

    def back(g, staged):
        t_inv = yield from _wide_unit_lower_inverse([st[0] for st in staged], eye_w, blk16, mm)
        pairs = [(c, h) for c in range(chunks) for h in range(GDN_HEADS)]
        hsl = lambda h: slice(h * GDN_K_DIM, (h + 1) * GDN_K_DIM)
        wsl = lambda h: slice(h * CHUNK, (h + 1) * CHUNK)
        uw16 = {}
        for c, h in pairs:
            _, _, v_beta, kbe, _, _, _ = staged[c]
            t_off = t_inv[c] - eye_w
            rhs = jnp.concatenate([v_beta[:, hsl(h)], kbe[:, hsl(h)]], axis=1)
            uw = rhs + _dot(t_off[:, wsl(h)].astype(BF16), rhs.astype(BF16))
            uw16[c, h] = uw.astype(BF16)
        yield
        q16, o0 = {}, {}
        for c, h in pairs:
            a_intra, q_dec = staged[c][1], staged[c][4]
            ai_uw = _dot(a_intra[:, wsl(h)].astype(BF16), uw16[c, h])
            q16[c, h] = (q_dec[:, hsl(h)] - ai_uw[:, GDN_V_DIM:]).astype(BF16)
            o0[c, h] = ai_uw[:, :GDN_V_DIM]
        yield
        ops = {}
        for c, h in pairs:
            k_tail = staged[c][5]
            kt_uw = _dot_tn(k_tail[:, hsl(h)].astype(BF16), uw16[c, h])
            pq16 = jnp.concatenate([kt_uw[:, GDN_V_DIM:].astype(BF16), q16[c, h]], axis=0)
            ops[c, h] = (pq16, kt_uw[:, :GDN_V_DIM], o0[c, h], staged[c][6][:, hsl(h)])
        yield
        return ops

    def scan(g, ops):
        hsl = lambda h: slice(h * GDN_K_DIM, (h + 1) * GDN_K_DIM)
        for c in range(chunks):
            rows = slice((g * chunks + c) * CHUNK, (g * chunks + c + 1) * CHUNK)
            for h in range(GDN_HEADS):
                pq16, n_c, o0_c, gl = ops[c, h]
                state = state_ref[h]
                r = _dot(pq16, state.astype(BF16))
                state_ref[h] = state * gl - r[:GDN_K_DIM] + n_c
                o = r[GDN_K_DIM:] + o0_c
                ms = jnp.mean(o * o, axis=-1, keepdims=True)
                y = o * lax.rsqrt(ms + EPS) * nw_ref[...]
                o_ref[rows, hsl(h)] = (y * z_ref[rows, hsl(h)].astype(F32)).astype(o_ref.dtype)
            yield

    groups = tr // grp
    staged, = _run_interleaved(front(0))
    ops = None
    for g in range(groups):
        nxt = front(g + 1) if g + 1 < groups else None
        prev_scan = scan(g - 1, ops) if g > 0 else None
        ops, staged, _ = _run_interleaved(back(g, staged), nxt, prev_scan)
    _run_interleaved(scan(groups - 1, ops))


def _gdn(qb, kb, vb, ab, zb, conv_w, alog_pad, dtb_pad, norm_w, tr, grp):
    b, s, _ = qb.shape
    assert tr % grp == 0 and grp % CHUNK == 0
    blk = lambda bi, i: (bi, i, 0)
    halo = lambda bi, i: (bi, jnp.maximum(i * (tr // HALO) - 1, 0), 0)
    const = lambda bi, i: (0, 0)
    const3 = lambda bi, i: (0, 0, 0)
    t_idx = jnp.arange(grp)
    h_idx = jnp.arange(HALO)
    shift = jnp.stack([(t_idx[:, None] - sft == t_idx[None, :]) for sft in range(1, CONV_WIDTH)]).astype(BF16)
    hsel = jnp.stack([(h_idx[:, None] - sft + HALO == h_idx[None, :])
                      for sft in range(1, CONV_WIDTH)]).astype(BF16)
    return pl.pallas_call(
        _gdn_kernel,
        grid=(b, s // tr),
        in_specs=[
            pl.BlockSpec((None, tr, BRANCH), blk),
            pl.BlockSpec((None, tr, BRANCH), blk),
            pl.BlockSpec((None, tr, BRANCH), blk),
            pl.BlockSpec((None, HALO, BRANCH), halo),
            pl.BlockSpec((None, HALO, BRANCH), halo),
            pl.BlockSpec((None, HALO, BRANCH), halo),
            pl.BlockSpec((None, tr, LANES), blk),
            pl.BlockSpec((None, tr, BRANCH), blk),
            pl.BlockSpec(conv_w.shape, const),
            pl.BlockSpec((1, LANES), const),
            pl.BlockSpec((1, LANES), const),
            pl.BlockSpec((1, GDN_V_DIM), const),
            pl.BlockSpec(shift.shape, const3),
            pl.BlockSpec(hsel.shape, const3),
        ],
        out_specs=pl.BlockSpec((None, tr, BRANCH), blk),
        out_shape=jax.ShapeDtypeStruct((b, s, BRANCH), BF16),
        scratch_shapes=[pltpu.VMEM((GDN_HEADS, GDN_K_DIM, GDN_V_DIM), F32)],
        compiler_params=pltpu.CompilerParams(
            dimension_semantics=("arbitrary", "arbitrary"), vmem_limit_bytes=VMEM_LIMIT),
        name="gdn",
    )(qb, kb, vb, qb, kb, vb, ab, zb, conv_w, alog_pad, dtb_pad, norm_w, shift, hsel)


def _out_proj_kernel(x_ref, ya_ref, yb_ref, gate_ref, wa_ref, wb_ref, wo_ref, o_ref):
    d = x_ref.shape[-1]
    up_a = _dot(ya_ref[...], wa_ref[...])
    up_b = _dot(yb_ref[...], wb_ref[...])
    merged = (gate_ref[:, :d].astype(F32) * up_a + gate_ref[:, d:].astype(F32) * up_b)
    o_ref[...] = x_ref[...] + _dot(merged.astype(BF16), wo_ref[...])


def _out_proj(x2d, ya, yb, gates, w_up_a, w_up_b, w_out, tm):
    n, d = x2d.shape
    const = lambda i: (0, 0)
    row = lambda i: (i, 0)
    return pl.pallas_call(
        _out_proj_kernel,
        grid=(n // tm,),
        in_specs=[
            pl.BlockSpec((tm, d), row),
            pl.BlockSpec((tm, BRANCH), row),
            pl.BlockSpec((tm, BRANCH), row),
            pl.BlockSpec((tm, 2 * d), row),
            pl.BlockSpec(w_up_a.shape, const),
            pl.BlockSpec(w_up_b.shape, const),
            pl.BlockSpec(w_out.shape, const),
        ],
        out_specs=pl.BlockSpec((tm, d), row),
        out_shape=jax.ShapeDtypeStruct((n, d), x2d.dtype),
        compiler_params=pltpu.CompilerParams(
            dimension_semantics=("arbitrary",), vmem_limit_bytes=VMEM_LIMIT),
        name="out_proj",
    )(x2d, ya, yb, gates, w_up_a, w_up_b, w_out)


def _lambda_init_for(layer):
    return 0.8 - 0.6 * math.exp(-0.3 * layer)


def kernel(x, norm_w, w_in, da_q_gain, da_k_gain, da_lambda, da_out_gain, gdn_conv_w, gdn_a_log,
           gdn_dt_bias, gdn_norm_w, w_up_da, w_up_gdn, w_out):
    b, s, d = x.shape
    depth = norm_w.shape[0]
    in_proj_tile = 1024
    out_proj_tile = 1024
    attn_q_tile = 1024
    attn_k_tile = 256
    gdn_tile = 512
    gdn_group = 256
    groups_per_seg = BRANCH // DA_HEAD_DIM
    q_scale = DA_HEAD_DIM ** -0.5 * math.log2(math.e)

    x2d = x.reshape(b * s, d)
    for l in range(depth):
        w_feat = w_in[l].T
        n_blocked = w_feat.shape[0] // BRANCH * BRANCH
        w_t = _weight_prep(w_feat, n_blocked)
        w_last = jnp.pad(w_feat[n_blocked:], ((0, -(w_feat.shape[0] - n_blocked) % LANES), (0, 0))).astype(BF16)
        q_gain = jnp.tile(da_q_gain[l] * q_scale, groups_per_seg)[None, :]
        k_gain = jnp.tile(da_k_gain[l], groups_per_seg)[None, :]
        qa, ka, vat, za, qb, kb, vb, zb, ab, gates = _in_proj(
            x2d, norm_w[l][None, :], w_t, w_last, q_gain, k_gain, in_proj_tile)

        r3 = lambda t: t.reshape(b, s, t.shape[-1])
        ya = _diff_attn(r3(qa), r3(ka), vat, r3(za), da_lambda[l], da_out_gain[l][None, :],
                        _lambda_init_for(l), attn_q_tile, attn_k_tile)

        alog_pad = jnp.pad(gdn_a_log[l], (0, LANES - GDN_HEADS))[None, :]
        dtb_pad = jnp.pad(gdn_dt_bias[l], (0, LANES - GDN_HEADS))[None, :]
        yb = _gdn(r3(qb), r3(kb), r3(vb), r3(ab), r3(zb), gdn_conv_w[l], alog_pad, dtb_pad,
                  gdn_norm_w[l][None, :], gdn_tile, gdn_group)

        x2d = _out_proj(x2d, ya.reshape(b * s, -1), yb.reshape(b * s, -1), gates,
                        w_up_da[l].astype(BF16), w_up_gdn[l].astype(BF16), w_out[l].astype(BF16),
                        out_proj_tile)
    return x2d.reshape(b, s, d)
```

```python
import functools
import math

import jax
import jax.numpy as jnp
from jax import lax
from jax.experimental import pallas as pl
from jax.experimental.pallas import tpu as pltpu

F32 = jnp.float32
BF16 = jnp.bfloat16

CHUNK = 64
EPS = 1e-6
DA_HEADS = 4
DA_HEAD_DIM = 64
DA_V_DIM = 2 * DA_HEAD_DIM
GDN_HEADS = 4
GDN_K_DIM = 128
GDN_V_DIM = 128
CONV_WIDTH = 4
BRANCH = 512
LANES = 128
NEG_BIG = -1e30
ONES_ROWS = 16
UNROLL_BLOCKS = 2
VMEM_LIMIT = 56 * 1024 * 1024


def _dot(a, b, precision=None):
    return jnp.dot(a, b, preferred_element_type=F32, precision=precision)


def _dot_nt(a, b, precision=None):
    return lax.dot_general(a, b, (((1,), (1,)), ((), ())),
                           preferred_element_type=F32, precision=precision)


def _dot_tn(a, b, precision=None):
    return lax.dot_general(a, b, (((0,), (0,)), ((), ())),
                           preferred_element_type=F32, precision=precision)


def _sigmoid(x):
    return 1.0 / (1.0 + jnp.exp(-x))


def _silu(x):
    return x * _sigmoid(x)


def _weight_prep_kernel(w_ref, wb_ref):
    wb_ref[...] = w_ref[...].astype(BF16)


def _weight_prep(wt, n_rows):
    d = wt.shape[1]
    blk = pl.BlockSpec((BRANCH, d), lambda j: (j, 0))
    return pl.pallas_call(
        _weight_prep_kernel,
        grid=(n_rows // BRANCH,),
        in_specs=[blk],
        out_specs=blk,
        out_shape=jax.ShapeDtypeStruct((n_rows, d), BF16),
        compiler_params=pltpu.CompilerParams(
            dimension_semantics=("arbitrary",), vmem_limit_bytes=VMEM_LIMIT),
        name="weight_prep",
    )(wt)


def _in_proj_kernel(x_ref, nw_ref, wt_ref, wlast_ref, qg_ref, kg_ref,
                    qa_ref, ka_ref, vat_ref, za_ref, qb_ref, kb_ref, vb_ref, zb_ref, ab_ref, gate_ref):
    x = x_ref[...]
    ms = jnp.mean(x * x, axis=-1, keepdims=True)
    h = (x * lax.rsqrt(ms + EPS) * nw_ref[...]).astype(BF16)

    def proj(j):
        return _dot_nt(h, wt_ref[j * BRANCH:(j + 1) * BRANCH, :])

    def qk_norm(t, gain):
        sq = t * t
        low = lax.broadcasted_iota(jnp.int32, (t.shape[0], LANES), 1) < DA_HEAD_DIM
        parts = []
        for j in range(t.shape[1] // LANES):
            blk = sq[:, j * LANES:(j + 1) * LANES]
            s_low = jnp.sum(jnp.where(low, blk, 0.0), axis=-1, keepdims=True)
            s_all = jnp.sum(blk, axis=-1, keepdims=True)
            parts.append(jnp.where(low, s_low, s_all - s_low))
        ssq = jnp.concatenate(parts, axis=1)
        return (t * lax.rsqrt(ssq * (1.0 / DA_HEAD_DIM) + EPS) * gain).astype(BF16)

    qa_ref[...] = qk_norm(proj(0), qg_ref[...])
    ka_ref[...] = qk_norm(proj(1), kg_ref[...])
    vat_ref[...] = _dot_nt(wt_ref[2 * BRANCH:3 * BRANCH, :], h).astype(BF16)
    za_ref[...] = _silu(proj(3)).astype(BF16)
    qb_ref[...] = proj(4).astype(BF16)
    kb_ref[...] = proj(5).astype(BF16)
    vb_ref[...] = proj(6).astype(BF16)
    zb_ref[...] = _silu(proj(7)).astype(BF16)
    tail = jnp.concatenate([_dot_nt(h, wt_ref[8 * BRANCH:, :]), _dot_nt(h, wlast_ref[...])], axis=1)
    ab_ref[...] = tail[:, :LANES]
    n_ab = 2 * GDN_HEADS
    gate_ref[...] = _sigmoid(tail[:, n_ab:n_ab + gate_ref.shape[1]]).astype(BF16)


def _in_proj(x2d, norm_w, w_t, w_last, q_gain, k_gain, tm):
    n, d = x2d.shape
    const = lambda i: (0, 0)
    row = lambda i: (i, 0)
    seg = jax.ShapeDtypeStruct((n, BRANCH), BF16)
    seg_spec = pl.BlockSpec((tm, BRANCH), row)
    out_shape = [seg, seg, jax.ShapeDtypeStruct((BRANCH, n), BF16)] + [seg] * 5 + [
        jax.ShapeDtypeStruct((n, LANES), F32), jax.ShapeDtypeStruct((n, 2 * d), BF16)]
    out_specs = ([seg_spec, seg_spec, pl.BlockSpec((BRANCH, tm), lambda i: (0, i))]
                 + [seg_spec] * 5
                 + [pl.BlockSpec((tm, LANES), row), pl.BlockSpec((tm, 2 * d), row)])
    return pl.pallas_call(
        _in_proj_kernel,
        grid=(n // tm,),
        in_specs=[
            pl.BlockSpec((tm, d), row),
            pl.BlockSpec((1, d), const),
            pl.BlockSpec(w_t.shape, const, pipeline_mode=pl.Buffered(1)),
            pl.BlockSpec(w_last.shape, const, pipeline_mode=pl.Buffered(1)),
            pl.BlockSpec((1, BRANCH), const),
            pl.BlockSpec((1, BRANCH), const),
        ],
        out_specs=out_specs,
        out_shape=out_shape,
        compiler_params=pltpu.CompilerParams(
            dimension_semantics=("arbitrary",), vmem_limit_bytes=VMEM_LIMIT),
        name="in_proj",
    )(x2d, norm_w, w_t, w_last, q_gain, k_gain)


def _diff_attn_kernel(lam_ref, bias_ref, q_ref, k_ref, vt_ref, z_ref, og_ref, o_ref, s_scr, m_scr, acc_scr,
                      *, tq, tk, lambda_init):
    i = pl.program_id(2)
    q = q_ref[...]
    lane = lax.broadcasted_iota(jnp.int32, q.shape, 1)
    zero = jnp.zeros_like(q)
    qq = jnp.concatenate([jnp.where(lane < DA_HEAD_DIM, q, zero),
                          jnp.where(lane >= DA_HEAD_DIM, q, zero)], axis=0)
    qq_t = qq.T
    sub_tiles = s_scr.shape[0]
    ones_rows = jnp.ones((ONES_ROWS, tk), BF16)
    m_scr[...] = jnp.full(m_scr.shape, NEG_BIG, F32)
    acc_scr[...] = jnp.zeros(acc_scr.shape, F32)

    def issue(start, slot, q0=0):
        wq = tq - q0
        qsel = qq_t if q0 == 0 else jnp.concatenate([qq_t[:, q0:tq], qq_t[:, tq + q0:]], axis=1)
        s = _dot(k_ref[pl.ds(start, tk), :], qsel)
        s_scr[slot, :, :2 * wq] = s
        return jnp.max(s, axis=0, keepdims=True)

    def consume(start, slot, tile_max, masked, q0=0):
        wq = tq - q0
        s = s_scr[slot, :, :2 * wq]
        if masked:
            bias = bias_ref[...]
            pieces = []
            for g in range(2):
                pieces.append(s[:, g * wq:g * wq + tk] + bias)
                if wq > tk:
                    pieces.append(s[:, g * wq + tk:(g + 1) * wq])
            s = jnp.concatenate(pieces, axis=1)
            tile_max = jnp.max(s, axis=0, keepdims=True)
        m_cols = [slice(g * tq + q0, (g + 1) * tq) for g in range(2)]
        m = jnp.concatenate([m_scr[:, mc] for mc in m_cols], axis=1)
        m_new = jnp.maximum(m, tile_max)
        alpha = jnp.exp2(m - m_new)
        p = jnp.exp2(s - m_new)
        pb = p.astype(BF16)
        vt = jnp.concatenate([vt_ref[:, pl.ds(start, tk)], ones_rows], axis=0)
        for g in range(2):
            cols = slice(g * wq, (g + 1) * wq)
            m_scr[:, m_cols[g]] = m_new[:, cols]
            acc_scr[g, :, q0:] = alpha[:, cols] * acc_scr[g, :, q0:] + _dot(vt, pb[:, cols])

    def make_body(blocks):
        def body(jj, tile_max):
            for u in range(blocks):
                for t in range(sub_tiles):
                    start = pl.multiple_of((jj * blocks + u) * tq + t * tk, tk)
                    next_max = issue(start + tk, (t + 1) % sub_tiles)
                    consume(start, t, tile_max, masked=False)
                    tile_max = next_max
            return tile_max
        return body

    paired = i // UNROLL_BLOCKS
    tile_max = lax.fori_loop(0, paired, make_body(UNROLL_BLOCKS), issue(0, 0))
    tile_max = lax.fori_loop(paired * UNROLL_BLOCKS, i, make_body(1), tile_max)
    for t in range(sub_tiles):
        start = pl.multiple_of(i * tq + t * tk, tk)
        if t + 1 < sub_tiles:
            issue(start + tk, t + 1, q0=(t + 1) * tk)
        consume(start, t, None, masked=True, q0=t * tk)
    acc1 = acc_scr[0]
    acc2 = acc_scr[1]

    lam_p = lam_ref[...]
    lam = (jnp.exp(jnp.sum(lam_p[0:1] * lam_p[1:2], axis=-1, keepdims=True))
           - jnp.exp(jnp.sum(lam_p[2:3] * lam_p[3:4], axis=-1, keepdims=True))
           + lambda_init)
    ot = (acc1[:DA_V_DIM] / acc1[DA_V_DIM:DA_V_DIM + 1]
          - lam * (acc2[:DA_V_DIM] / acc2[DA_V_DIM:DA_V_DIM + 1]))
    ms = jnp.mean(ot * ot, axis=0, keepdims=True)
    y = (ot * lax.rsqrt(ms + EPS)).T * og_ref[...] * (1.0 - lambda_init)
    o_ref[...] = (y * z_ref[...].astype(F32)).astype(o_ref.dtype)


def _diff_attn(qa, ka, vat, za, da_lambda, out_gain, lambda_init, tq, tk):
    b, s, _ = qa.shape
    sub_tiles = tq // tk
    assert sub_tiles >= 2 and sub_tiles * tk == tq
    blk = lambda bi, h, i: (bi, i, h)
    const = lambda bi, h, i: (0, 0)
    chunk_of = jnp.arange(tk) // CHUNK
    bias = jnp.where(chunk_of[:, None] <= chunk_of[None, :], 0.0, NEG_BIG).astype(F32)
    return pl.pallas_call(
        functools.partial(_diff_attn_kernel, tq=tq, tk=tk, lambda_init=lambda_init),
        scratch_shapes=[pltpu.VMEM((sub_tiles, tk, 2 * tq), F32),
                        pltpu.VMEM((1, 2 * tq), F32),
                        pltpu.VMEM((2, DA_V_DIM + ONES_ROWS, tq), F32)],
        grid=(b, DA_HEADS, s // tq),
        in_specs=[
            pl.BlockSpec(da_lambda.shape, const),
            pl.BlockSpec(bias.shape, const),
            pl.BlockSpec((None, tq, DA_V_DIM), blk),
            pl.BlockSpec((None, s, DA_V_DIM), lambda bi, h, i: (bi, 0, h)),
            pl.BlockSpec((DA_V_DIM, s), lambda bi, h, i: (h, bi)),
            pl.BlockSpec((None, tq, DA_V_DIM), blk),
            pl.BlockSpec((1, DA_V_DIM), const),
        ],
        out_specs=pl.BlockSpec((None, tq, DA_V_DIM), blk),
        out_shape=jax.ShapeDtypeStruct((b, s, DA_HEADS * DA_V_DIM), BF16),
        compiler_params=pltpu.CompilerParams(
            dimension_semantics=("arbitrary", "arbitrary", "arbitrary"),
            vmem_limit_bytes=VMEM_LIMIT),
        name="diff_attn",
    )(da_lambda, bias, qa, ka, vat, za, out_gain)


WIDE = GDN_HEADS * CHUNK
HALO = 16


def _split2(x):
    hi = x.astype(BF16)
    lo = (x - hi.astype(F32)).astype(BF16)
    return hi, lo


def _block_diag(y, mask01):
    return jnp.concatenate([y] * GDN_HEADS, axis=0) * mask01


def _wide_mm(x, y, bd_mask):
    return _dot(x.astype(BF16), _block_diag(y.astype(BF16), bd_mask))


def _wide_unit_lower_inverse(a_list, eye, blk16, mm):
    d = [jnp.where(blk16, a, 0.0) for a in a_list]
    off = [a - di for a, di in zip(a_list, d)]
    x = [-di for di in d]
    t = [eye + xi for xi in x]
    p = [mm(xi, xi) for xi in x]
    yield
    for _ in range(2):
        both = [mm(jnp.concatenate([ti, pi], axis=0), pi) for ti, pi in zip(t, p)]
        t = [ti + bi[:CHUNK] for ti, bi in zip(t, both)]
        p = [bi[CHUNK:] for bi in both]
        yield
    t = [ti + mm(ti, pi) for ti, pi in zip(t, p)]
    yield
    e = [mm(ti, oi) for ti, oi in zip(t, off)]
    yield
    e2 = [mm(ei, ei) for ei in e]
    yield
    f = [eye - ei + e2i - mm(ei, e2i) for ei, e2i in zip(e, e2)]
    yield
    return [mm(fi, ti) for fi, ti in zip(f, t)]


def _run_interleaved(*gens):
    results = [None] * len(gens)
    alive = [g is not None for g in gens]
    while any(alive):
        for n, gen in enumerate(gens):
            if alive[n]:
                try:
                    next(gen)
                except StopIteration as stop:
                    results[n] = stop.value
                    alive[n] = False
    return results


def _head_l2norm(x):
    parts = []
    for h in range(GDN_HEADS):
        xh = x[:, h * GDN_K_DIM:(h + 1) * GDN_K_DIM]
        parts.append(xh * lax.rsqrt(jnp.sum(xh * xh, axis=-1, keepdims=True) + EPS))
    return jnp.concatenate(parts, axis=1)


def _head_bcast(col4, width):
    rows = col4.shape[0]
    return jnp.concatenate([jnp.broadcast_to(col4[:, h:h + 1], (rows, width))
                            for h in range(GDN_HEADS)], axis=1)


def _gdn_kernel(q_ref, k_ref, v_ref, qh_ref, kh_ref, vh_ref, ab_ref, z_ref, cw_ref, alog_ref, dtb_ref,
                nw_ref, shift_ref, hsel_ref, o_ref, state_ref):
    tr = q_ref.shape[0]

    @pl.when(pl.program_id(1) == 0)
    def _():
        state_ref[...] = jnp.zeros_like(state_ref)

    grp = shift_ref.shape[1]
    chunks = grp // CHUNK

    row_w = lax.broadcasted_iota(jnp.int32, (CHUNK, WIDE), 0)
    col_w = lax.broadcasted_iota(jnp.int32, (CHUNK, WIDE), 1) % CHUNK
    tri_incl = col_w <= row_w
    tri_strict = col_w < row_w
    eye_w = jnp.where(col_w == row_w, 1.0, 0.0).astype(F32)
    blk16 = (row_w // 16) == (col_w // 16)
    bd_mask = jnp.where(lax.broadcasted_iota(jnp.int32, (WIDE, WIDE), 0) // CHUNK
                        == lax.broadcasted_iota(jnp.int32, (WIDE, WIDE), 1) // CHUNK,
                        1.0, 0.0).astype(BF16)
    kbd_mask = jnp.where(lax.broadcasted_iota(jnp.int32, (WIDE, BRANCH), 0) // CHUNK
                         == lax.broadcasted_iota(jnp.int32, (WIDE, BRANCH), 1) // GDN_K_DIM,
                         1.0, 0.0).astype(BF16)
    r64 = lax.broadcasted_iota(jnp.int32, (CHUNK, CHUNK), 0)
    c64 = lax.broadcasted_iota(jnp.int32, (CHUNK, CHUNK), 1)
    lower_ones = jnp.where(c64 <= r64, 1.0, 0.0).astype(BF16)
    mm = functools.partial(_wide_mm, bd_mask=bd_mask)

    def conv_silu(x_ref, x_halo_ref, seg, gr, first_group):
        cols = slice(seg * BRANCH, (seg + 1) * BRANCH)
        cur = x_ref[gr, :]
        if first_group:
            halo = x_halo_ref[...]
            halo = jnp.where(pl.program_id(1) == 0, jnp.zeros((), halo.dtype), halo)
        else:
            halo = x_ref[gr.start - HALO:gr.start, :]
        conv = cur.astype(F32) * cw_ref[CONV_WIDTH - 1:CONV_WIDTH, cols]
        for s in range(1, CONV_WIDTH):
            shifted = _dot(shift_ref[s - 1], cur)
            head = shifted[:HALO] + _dot(hsel_ref[s - 1], halo)
            shifted = jnp.concatenate([head, shifted[HALO:]], axis=0)
            conv = conv + shifted * cw_ref[CONV_WIDTH - 1 - s:CONV_WIDTH - s, cols]
        return _silu(conv)

    def front(g):
        gr = slice(g * grp, (g + 1) * grp)
        q = _head_l2norm(conv_silu(q_ref, qh_ref, 0, gr, g == 0)) * (GDN_K_DIM ** -0.5)
        yield
        k = _head_l2norm(conv_silu(k_ref, kh_ref, 1, gr, g == 0))
        yield
        v = conv_silu(v_ref, vh_ref, 2, gr, g == 0)
        yield

        ab = ab_ref[gr, :]
        xg = ab + dtb_ref[...]
        softplus = jnp.maximum(xg, 0.0) + jnp.log1p(jnp.exp(-jnp.abs(xg)))
        g_all = -jnp.exp(alog_ref[...]) * softplus
        beta_all = _sigmoid(ab)

        staged = []
        for c in range(chunks):
            rows = slice(c * CHUNK, (c + 1) * CHUNK)
            qc, kc, vc = q[rows], k[rows], v[rows]
            g_c = g_all[rows]
            g_hi, g_lo = _split2(g_c)
            gc2 = _dot(lower_ones, jnp.concatenate([g_hi, g_lo], axis=1))
            gc_all = gc2[:, :LANES] + gc2[:, LANES:]
            gcb = _head_bcast(gc_all, GDN_K_DIM)
            betab = _head_bcast(beta_all[rows, GDN_HEADS:], GDN_K_DIM)
            exp_gc = jnp.exp(gcb)
            gl_row = exp_gc[CHUNK - 1:CHUNK]
            k_beta = kc * betab
            v_beta = vc * betab
            kbe = k_beta * exp_gc
            q_dec = qc * exp_gc
            k_tail = kc * jnp.exp(gcb[CHUNK - 1:CHUNK] - gcb)

            r_w = jnp.where(tri_strict, _head_bcast(g_c, CHUNK), 0.0)
            r_hi, r_lo = _split2(r_w)
            d2 = _dot(lower_ones, jnp.concatenate([r_hi, r_lo], axis=1))
            gamma = jnp.exp(jnp.where(tri_incl, d2[:, :WIDE] + d2[:, WIDE:], NEG_BIG))

            k16 = kc.astype(BF16)
            kbd = jnp.concatenate([k16] * GDN_HEADS, axis=0) * kbd_mask
            sc = _dot_nt(jnp.concatenate([k_beta.astype(BF16), qc.astype(BF16)], axis=0), kbd)
            a_strict = jnp.where(tri_strict, sc[:CHUNK] * gamma, 0.0)
            a_intra = jnp.where(tri_incl, sc[CHUNK:] * gamma, 0.0)
            staged.append((a_strict, a_intra, v_beta, kbe, q_dec, k_tail, gl_row))
            yield
        return staged

    def back(g, staged):
        t_inv = yield from _wide_unit_lower_inverse([st[0] for st in staged], eye_w, blk16, mm)
        pairs = [(c, h) for c in range(chunks) for h in range(GDN_HEADS)]
        hsl = lambda h: slice(h * GDN_K_DIM, (h + 1) * GDN_K_DIM)
        wsl = lambda h: slice(h * CHUNK, (h + 1) * CHUNK)
        uw16 = {}
        for c, h in pairs:
            _, _, v_beta, kbe, _, _, _ = staged[c]
            t_off = t_inv[c] - eye_w
            rhs = jnp.concatenate([v_beta[:, hsl(h)], kbe[:, hsl(h)]], axis=1)
            uw = rhs + _dot(t_off[:, wsl(h)].astype(BF16), rhs.astype(BF16))
            uw16[c, h] = uw.astype(BF16)
        yield
        q16, o0 = {}, {}
        for c, h in pairs:
            a_intra, q_dec = staged[c][1], staged[c][4]
            ai_uw = _dot(a_intra[:, wsl(h)].astype(BF16), uw16[c, h])
            q16[c, h] = (q_dec[:, hsl(h)] - ai_uw[:, GDN_V_DIM:]).astype(BF16)
            o0[c, h] = ai_uw[:, :GDN_V_DIM]
        yield
        ops = {}
        for c, h in pairs:
            k_tail = staged[c][5]
            kt_uw = _dot_tn(k_tail[:, hsl(h)].astype(BF16), uw16[c, h])
            pq16 = jnp.concatenate([kt_uw[:, GDN_V_DIM:].astype(BF16), q16[c, h]], axis=0)
            ops[c, h] = (pq16, kt_uw[:, :GDN_V_DIM], o0[c, h], staged[c][6][:, hsl(h)])
        yield
        return ops

    def scan(g, ops):
        hsl = lambda h: slice(h * GDN_K_DIM, (h + 1) * GDN_K_DIM)
        for c in range(chunks):
            rows = slice((g * chunks + c) * CHUNK, (g * chunks + c + 1) * CHUNK)
            for h in range(GDN_HEADS):
                pq16, n_c, o0_c, gl = ops[c, h]
                state = state_ref[h]
                r = _dot(pq16, state.astype(BF16))
                state_ref[h] = state * gl - r[:GDN_K_DIM] + n_c
                o = r[GDN_K_DIM:] + o0_c
                ms = jnp.mean(o * o, axis=-1, keepdims=True)
                y = o * lax.rsqrt(ms + EPS) * nw_ref[...]
                o_ref[rows, hsl(h)] = (y * z_ref[rows, hsl(h)].astype(F32)).astype(o_ref.dtype)
            yield

    groups = tr // grp
    staged, = _run_interleaved(front(0))
    ops = None
    for g in range(groups):
        nxt = front(g + 1) if g + 1 < groups else None
        prev_scan = scan(g - 1, ops) if g > 0 else None
        ops, staged, _ = _run_interleaved(back(g, staged), nxt, prev_scan)
    _run_interleaved(scan(groups - 1, ops))


def _gdn(qb, kb, vb, ab, zb, conv_w, alog_pad, dtb_pad, norm_w, tr, grp):
    b, s, _ = qb.shape
    assert tr % grp == 0 and grp % CHUNK == 0
    blk = lambda bi, i: (bi, i, 0)
    halo = lambda bi, i: (bi, jnp.maximum(i * (tr // HALO) - 1, 0), 0)
    const = lambda bi, i: (0, 0)
    const3 = lambda bi, i: (0, 0, 0)
    t_idx = jnp.arange(grp)
    h_idx = jnp.arange(HALO)
    shift = jnp.stack([(t_idx[:, None] - sft == t_idx[None, :]) for sft in range(1, CONV_WIDTH)]).astype(BF16)
    hsel = jnp.stack([(h_idx[:, None] - sft + HALO == h_idx[None, :])
                      for sft in range(1, CONV_WIDTH)]).astype(BF16)
    return pl.pallas_call(
        _gdn_kernel,
        grid=(b, s // tr),
        in_specs=[
            pl.BlockSpec((None, tr, BRANCH), blk),
            pl.BlockSpec((None, tr, BRANCH), blk),
            pl.BlockSpec((None, tr, BRANCH), blk),
            pl.BlockSpec((None, HALO, BRANCH), halo),
            pl.BlockSpec((None, HALO, BRANCH), halo),
            pl.BlockSpec((None, HALO, BRANCH), halo),
            pl.BlockSpec((None, tr, LANES), blk),
            pl.BlockSpec((None, tr, BRANCH), blk),
            pl.BlockSpec(conv_w.shape, const),
            pl.BlockSpec((1, LANES), const),
            pl.BlockSpec((1, LANES), const),
            pl.BlockSpec((1, GDN_V_DIM), const),
            pl.BlockSpec(shift.shape, const3),
            pl.BlockSpec(hsel.shape, const3),
        ],
        out_specs=pl.BlockSpec((None, tr, BRANCH), blk),
        out_shape=jax.ShapeDtypeStruct((b, s, BRANCH), BF16),
        scratch_shapes=[pltpu.VMEM((GDN_HEADS, GDN_K_DIM, GDN_V_DIM), F32)],
        compiler_params=pltpu.CompilerParams(
            dimension_semantics=("arbitrary", "arbitrary"), vmem_limit_bytes=VMEM_LIMIT),
        name="gdn",
    )(qb, kb, vb, qb, kb, vb, ab, zb, conv_w, alog_pad, dtb_pad, norm_w, shift, hsel)


def _out_proj_kernel(x_ref, ya_ref, yb_ref, gate_ref, wa_ref, wb_ref, wo_ref, o_ref):
    d = x_ref.shape[-1]
    up_a = _dot(ya_ref[...], wa_ref[...])
    up_b = _dot(yb_ref[...], wb_ref[...])
    merged = (gate_ref[:, :d].astype(F32) * up_a + gate_ref[:, d:].astype(F32) * up_b)
    o_ref[...] = x_ref[...] + _dot(merged.astype(BF16), wo_ref[...])


def _out_proj(x2d, ya, yb, gates, w_up_a, w_up_b, w_out, tm):
    n, d = x2d.shape
    const = lambda i: (0, 0)
    row = lambda i: (i, 0)
    return pl.pallas_call(
        _out_proj_kernel,
        grid=(n // tm,),
        in_specs=[
            pl.BlockSpec((tm, d), row),
            pl.BlockSpec((tm, BRANCH), row),
            pl.BlockSpec((tm, BRANCH), row),
            pl.BlockSpec((tm, 2 * d), row),
            pl.BlockSpec(w_up_a.shape, const),
            pl.BlockSpec(w_up_b.shape, const),
            pl.BlockSpec(w_out.shape, const),
        ],
        out_specs=pl.BlockSpec((tm, d), row),
        out_shape=jax.ShapeDtypeStruct((n, d), x2d.dtype),
        compiler_params=pltpu.CompilerParams(
            dimension_semantics=("arbitrary",), vmem_limit_bytes=VMEM_LIMIT),
        name="out_proj",
    )(x2d, ya, yb, gates, w_up_a, w_up_b, w_out)


def _lambda_init_for(layer):
    return 0.8 - 0.6 * math.exp(-0.3 * layer)


def kernel(x, norm_w, w_in, da_q_gain, da_k_gain, da_lambda, da_out_gain, gdn_conv_w, gdn_a_log,
           gdn_dt_bias, gdn_norm_w, w_up_da, w_up_gdn, w_out):
    b, s, d = x.shape
    depth = norm_w.shape[0]
    in_proj_tile = 1024
    out_proj_tile = 1024
    attn_q_tile = 1024
    attn_k_tile = 256
    gdn_tile = 512
    gdn_group = 256
    groups_per_seg = BRANCH // DA_HEAD_DIM
    q_scale = DA_HEAD_DIM ** -0.5 * math.log2(math.e)

    x2d = x.reshape(b * s, d)
    for l in range(depth):
        w_feat = w_in[l].T
        n_blocked = w_feat.shape[0] // BRANCH * BRANCH
        w_t = _weight_prep(w_feat, n_blocked)
        w_last = jnp.pad(w_feat[n_blocked:], ((0, -(w_feat.shape[0] - n_blocked) % LANES), (0, 0))).astype(BF16)
        q_gain = jnp.tile(da_q_gain[l] * q_scale, groups_per_seg)[None, :]
        k_gain = jnp.tile(da_k_gain[l], groups_per_seg)[None, :]
        qa, ka, vat, za, qb, kb, vb, zb, ab, gates = _in_proj(
            x2d, norm_w[l][None, :], w_t, w_last, q_gain, k_gain, in_proj_tile)

        r3 = lambda t: t.reshape(b, s, t.shape[-1])
        ya = _diff_attn(r3(qa), r3(ka), vat, r3(za), da_lambda[l], da_out_gain[l][None, :],
                        _lambda_init_for(l), attn_q_tile, attn_k_tile)

        alog_pad = jnp.pad(gdn_a_log[l], (0, LANES - GDN_HEADS))[None, :]
        dtb_pad = jnp.pad(gdn_dt_bias[l], (0, LANES - GDN_HEADS))[None, :]
        yb = _gdn(r3(qb), r3(kb), r3(vb), r3(ab), r3(zb), gdn_conv_w[l], alog_pad, dtb_pad,
                  gdn_norm_w[l][None, :], gdn_tile, gdn_group)

        x2d = _out_proj(x2d, ya.reshape(b * s, -1), yb.reshape(b * s, -1), gates,
                        w_up_da[l].astype(BF16), w_up_gdn[l].astype(BF16), w_out[l].astype(BF16),
                        out_proj_tile)
    return x2d.reshape(b, s, d)
```

```python
import functools
import math

import jax
import jax.numpy as jnp
from jax import lax
from jax.experimental import pallas as pl
from jax.experimental.pallas import tpu as pltpu

F32 = jnp.float32
BF16 = jnp.bfloat16

CHUNK = 64
EPS = 1e-6
DA_HEADS = 4
DA_HEAD_DIM = 64
DA_V_DIM = 2 * DA_HEAD_DIM
GDN_HEADS = 4
GDN_K_DIM = 128
GDN_V_DIM = 128
CONV_WIDTH = 4
BRANCH = 512
LANES = 128
NEG_BIG = -1e30
ONES_ROWS = 16
UNROLL_BLOCKS = 2
VMEM_LIMIT = 56 * 1024 * 1024


def _dot(a, b, precision=None):
    return jnp.dot(a, b, preferred_element_type=F32, precision=precision)


def _dot_nt(a, b, precision=None):
    return lax.dot_general(a, b, (((1,), (1,)), ((), ())),
                           preferred_element_type=F32, precision=precision)


def _dot_tn(a, b, precision=None):
    return lax.dot_general(a, b, (((0,), (0,)), ((), ())),
                           preferred_element_type=F32, precision=precision)


def _sigmoid(x):
    return 1.0 / (1.0 + jnp.exp(-x))


def _silu(x):
    return x * _sigmoid(x)


def _weight_prep_kernel(w_ref, wb_ref):
    wb_ref[...] = w_ref[...].astype(BF16)


def _weight_prep(wt, n_rows):
    d = wt.shape[1]
    blk = pl.BlockSpec((BRANCH, d), lambda j: (j, 0))
    return pl.pallas_call(
        _weight_prep_kernel,
        grid=(n_rows // BRANCH,),
        in_specs=[blk],
        out_specs=blk,
        out_shape=jax.ShapeDtypeStruct((n_rows, d), BF16),
        compiler_params=pltpu.CompilerParams(
            dimension_semantics=("arbitrary",), vmem_limit_bytes=VMEM_LIMIT),
        name="weight_prep",
    )(wt)


def _in_proj_kernel(x_ref, nw_ref, wt_ref, wlast_ref, qg_ref, kg_ref,
                    qa_ref, ka_ref, vat_ref, za_ref, qb_ref, kb_ref, vb_ref, zb_ref, ab_ref, gate_ref):
    x = x_ref[...]
    ms = jnp.mean(x * x, axis=-1, keepdims=True)
    h = (x * lax.rsqrt(ms + EPS) * nw_ref[...]).astype(BF16)

    def proj(j):
        return _dot_nt(h, wt_ref[j * BRANCH:(j + 1) * BRANCH, :])

    def qk_norm(t, gain):
        sq = t * t
        low = lax.broadcasted_iota(jnp.int32, (t.shape[0], LANES), 1) < DA_HEAD_DIM
        parts = []
        for j in range(t.shape[1] // LANES):
            blk = sq[:, j * LANES:(j + 1) * LANES]
            s_low = jnp.sum(jnp.where(low, blk, 0.0), axis=-1, keepdims=True)
            s_all = jnp.sum(blk, axis=-1, keepdims=True)
            parts.append(jnp.where(low, s_low, s_all - s_low))
        ssq = jnp.concatenate(parts, axis=1)
        return (t * lax.rsqrt(ssq * (1.0 / DA_HEAD_DIM) + EPS) * gain).astype(BF16)

    qa_ref[...] = qk_norm(proj(0), qg_ref[...])
    ka_ref[...] = qk_norm(proj(1), kg_ref[...])
    vat_ref[...] = _dot_nt(wt_ref[2 * BRANCH:3 * BRANCH, :], h).astype(BF16)
    za_ref[...] = _silu(proj(3)).astype(BF16)
    qb_ref[...] = proj(4).astype(BF16)
    kb_ref[...] = proj(5).astype(BF16)
    vb_ref[...] = proj(6).astype(BF16)
    zb_ref[...] = _silu(proj(7)).astype(BF16)
    tail = jnp.concatenate([_dot_nt(h, wt_ref[8 * BRANCH:, :]), _dot_nt(h, wlast_ref[...])], axis=1)
    ab_ref[...] = tail[:, :LANES]
    n_ab = 2 * GDN_HEADS
    gate_ref[...] = _sigmoid(tail[:, n_ab:n_ab + gate_ref.shape[1]]).astype(BF16)


def _in_proj(x2d, norm_w, w_t, w_last, q_gain, k_gain, tm):
    n, d = x2d.shape
    const = lambda i: (0, 0)
    row = lambda i: (i, 0)
    seg = jax.ShapeDtypeStruct((n, BRANCH), BF16)
    seg_spec = pl.BlockSpec((tm, BRANCH), row)
    out_shape = [seg, seg, jax.ShapeDtypeStruct((BRANCH, n), BF16)] + [seg] * 5 + [
        jax.ShapeDtypeStruct((n, LANES), F32), jax.ShapeDtypeStruct((n, 2 * d), BF16)]
    out_specs = ([seg_spec, seg_spec, pl.BlockSpec((BRANCH, tm), lambda i: (0, i))]
                 + [seg_spec] * 5
                 + [pl.BlockSpec((tm, LANES), row), pl.BlockSpec((tm, 2 * d), row)])
    return pl.pallas_call(
        _in_proj_kernel,
        grid=(n // tm,),
        in_specs=[
            pl.BlockSpec((tm, d), row),
            pl.BlockSpec((1, d), const),
            pl.BlockSpec(w_t.shape, const, pipeline_mode=pl.Buffered(1)),
            pl.BlockSpec(w_last.shape, const, pipeline_mode=pl.Buffered(1)),
            pl.BlockSpec((1, BRANCH), const),
            pl.BlockSpec((1, BRANCH), const),
        ],
        out_specs=out_specs,
        out_shape=out_shape,
        compiler_params=pltpu.CompilerParams(
            dimension_semantics=("arbitrary",), vmem_limit_bytes=VMEM_LIMIT),
        name="in_proj",
    )(x2d, norm_w, w_t, w_last, q_gain, k_gain)


def _diff_attn_kernel(lam_ref, bias_ref, q_ref, k_ref, vt_ref, z_ref, og_ref, o_ref, s_scr, m_scr, acc_scr,
                      *, tq, tk, lambda_init):
    i = pl.program_id(2)
    q = q_ref[...]
    lane = lax.broadcasted_iota(jnp.int32, q.shape, 1)
    zero = jnp.zeros_like(q)
    qq = jnp.concatenate([jnp.where(lane < DA_HEAD_DIM, q, zero),
                          jnp.where(lane >= DA_HEAD_DIM, q, zero)], axis=0)
    qq_t = qq.T
    sub_tiles = s_scr.shape[0]
    ones_rows = jnp.ones((ONES_ROWS, tk), BF16)
    m_scr[...] = jnp.full(m_scr.shape, NEG_BIG, F32)
    acc_scr[...] = jnp.zeros(acc_scr.shape, F32)

    def issue(start, slot, q0=0):
        wq = tq - q0
        qsel = qq_t if q0 == 0 else jnp.concatenate([qq_t[:, q0:tq], qq_t[:, tq + q0:]], axis=1)
        s = _dot(k_ref[pl.ds(start, tk), :], qsel)
        s_scr[slot, :, :2 * wq] = s
        return jnp.max(s, axis=0, keepdims=True)

    def consume(start, slot, tile_max, masked, q0=0):
        wq = tq - q0
        s = s_scr[slot, :, :2 * wq]
        if masked:
            bias = bias_ref[...]
            pieces = []
            for g in range(2):
                pieces.append(s[:, g * wq:g * wq + tk] + bias)
                if wq > tk:
                    pieces.append(s[:, g * wq + tk:(g + 1) * wq])
            s = jnp.concatenate(pieces, axis=1)
            tile_max = jnp.max(s, axis=0, keepdims=True)
        m_cols = [slice(g * tq + q0, (g + 1) * tq) for g in range(2)]
        m = jnp.concatenate([m_scr[:, mc] for mc in m_cols], axis=1)
        m_new = jnp.maximum(m, tile_max)
        alpha = jnp.exp2(m - m_new)
        p = jnp.exp2(s - m_new)
        pb = p.astype(BF16)
        vt = jnp.concatenate([vt_ref[:, pl.ds(start, tk)], ones_rows], axis=0)
        for g in range(2):
            cols = slice(g * wq, (g + 1) * wq)
            m_scr[:, m_cols[g]] = m_new[:, cols]
            acc_scr[g, :, q0:] = alpha[:, cols] * acc_scr[g, :, q0:] + _dot(vt, pb[:, cols])

    def make_body(blocks):
        def body(jj, tile_max):
            for u in range(blocks):
                for t in range(sub_tiles):
                    start = pl.multiple_of((jj * blocks + u) * tq + t * tk, tk)
                    next_max = issue(start + tk, (t + 1) % sub_tiles)
                    consume(start, t, tile_max, masked=False)
                    tile_max = next_max
            return tile_max
        return body

    paired = i // UNROLL_BLOCKS
    tile_max = lax.fori_loop(0, paired, make_body(UNROLL_BLOCKS), issue(0, 0))
    tile_max = lax.fori_loop(paired * UNROLL_BLOCKS, i, make_body(1), tile_max)
    for t in range(sub_tiles):
        start = pl.multiple_of(i * tq + t * tk, tk)
        if t + 1 < sub_tiles:
            issue(start + tk, t + 1, q0=(t + 1) * tk)
        consume(start, t, None, masked=True, q0=t * tk)
    acc1 = acc_scr[0]
    acc2 = acc_scr[1]

    lam_p = lam_ref[...]
    lam = (jnp.exp(jnp.sum(lam_p[0:1] * lam_p[1:2], axis=-1, keepdims=True))
           - jnp.exp(jnp.sum(lam_p[2:3] * lam_p[3:4], axis=-1, keepdims=True))
           + lambda_init)
    ot = (acc1[:DA_V_DIM] / acc1[DA_V_DIM:DA_V_DIM + 1]
          - lam * (acc2[:DA_V_DIM] / acc2[DA_V_DIM:DA_V_DIM + 1]))
    ms = jnp.mean(ot * ot, axis=0, keepdims=True)
    y = (ot * lax.rsqrt(ms + EPS)).T * og_ref[...] * (1.0 - lambda_init)
    o_ref[...] = (y * z_ref[...].astype(F32)).astype(o_ref.dtype)


def _diff_attn(qa, ka, vat, za, da_lambda, out_gain, lambda_init, tq, tk):
    b, s, _ = qa.shape
    sub_tiles = tq // tk
    assert sub_tiles >= 2 and sub_tiles * tk == tq
    blk = lambda bi, h, i: (bi, i, h)
    const = lambda bi, h, i: (0, 0)
    chunk_of = jnp.arange(tk) // CHUNK
    bias = jnp.where(chunk_of[:, None] <= chunk_of[None, :], 0.0, NEG_BIG).astype(F32)
    return pl.pallas_call(
        functools.partial(_diff_attn_kernel, tq=tq, tk=tk, lambda_init=lambda_init),
        scratch_shapes=[pltpu.VMEM((sub_tiles, tk, 2 * tq), F32),
                        pltpu.VMEM((1, 2 * tq), F32),
                        pltpu.VMEM((2, DA_V_DIM + ONES_ROWS, tq), F32)],
        grid=(b, DA_HEADS, s // tq),
        in_specs=[
            pl.BlockSpec(da_lambda.shape, const),
            pl.BlockSpec(bias.shape, const),
            pl.BlockSpec((None, tq, DA_V_DIM), blk),
            pl.BlockSpec((None, s, DA_V_DIM), lambda bi, h, i: (bi, 0, h)),
            pl.BlockSpec((DA_V_DIM, s), lambda bi, h, i: (h, bi)),
            pl.BlockSpec((None, tq, DA_V_DIM), blk),
            pl.BlockSpec((1, DA_V_DIM), const),
        ],
        out_specs=pl.BlockSpec((None, tq, DA_V_DIM), blk),
        out_shape=jax.ShapeDtypeStruct((b, s, DA_HEADS * DA_V_DIM), BF16),
        compiler_params=pltpu.CompilerParams(
            dimension_semantics=("arbitrary", "arbitrary", "arbitrary"),
            vmem_limit_bytes=VMEM_LIMIT),
        name="diff_attn",
    )(da_lambda, bias, qa, ka, vat, za, out_gain)


WIDE = GDN_HEADS * CHUNK
HALO = 16


def _split2(x):
    hi = x.astype(BF16)
    lo = (x - hi.astype(F32)).astype(BF16)
    return hi, lo


def _block_diag(y, mask01):
    return jnp.concatenate([y] * GDN_HEADS, axis=0) * mask01


def _wide_mm(x, y, bd_mask):
    return _dot(x.astype(BF16), _block_diag(y.astype(BF16), bd_mask))


def _wide_unit_lower_inverse(a_list, eye, blk16, mm):
    d = [jnp.where(blk16, a, 0.0) for a in a_list]
    off = [a - di for a, di in zip(a_list, d)]
    p = [-di for di in d]
    t = [eye + pi for pi in p]
    for _ in range(3):
        p = [mm(pi, pi) for pi in p]
        yield
        t = [ti + mm(ti, pi) for ti, pi in zip(t, p)]
        yield
    e = [mm(ti, oi) for ti, oi in zip(t, off)]
    yield
    e2 = [mm(ei, ei) for ei in e]
    yield
    f = [eye - ei + e2i - mm(ei, e2i) for ei, e2i in zip(e, e2)]
    yield
    return [mm(fi, ti) for fi, ti in zip(f, t)]


def _run_interleaved(*gens):
    results = [None] * len(gens)
    alive = [g is not None for g in gens]
    while any(alive):
        for n, gen in enumerate(gens):
            if alive[n]:
                try:
                    next(gen)
                except StopIteration as stop:
                    results[n] = stop.value
                    alive[n] = False
    return results


def _head_l2norm(x):
    parts = []
    for h in range(GDN_HEADS):
        xh = x[:, h * GDN_K_DIM:(h + 1) * GDN_K_DIM]
        parts.append(xh * lax.rsqrt(jnp.sum(xh * xh, axis=-1, keepdims=True) + EPS))
    return jnp.concatenate(parts, axis=1)


def _head_bcast(col4, width):
    rows = col4.shape[0]
    return jnp.concatenate([jnp.broadcast_to(col4[:, h:h + 1], (rows, width))
                            for h in range(GDN_HEADS)], axis=1)


def _gdn_kernel(q_ref, k_ref, v_ref, qh_ref, kh_ref, vh_ref, ab_ref, z_ref, cw_ref, alog_ref, dtb_ref,
                nw_ref, shift_ref, hsel_ref, o_ref, state_ref):
    tr = q_ref.shape[0]

    @pl.when(pl.program_id(1) == 0)
    def _():
        state_ref[...] = jnp.zeros_like(state_ref)

    grp = shift_ref.shape[1]
    chunks = grp // CHUNK

    row_w = lax.broadcasted_iota(jnp.int32, (CHUNK, WIDE), 0)
    col_w = lax.broadcasted_iota(jnp.int32, (CHUNK, WIDE), 1) % CHUNK
    tri_incl = col_w <= row_w
    tri_strict = col_w < row_w
    eye_w = jnp.where(col_w == row_w, 1.0, 0.0).astype(F32)
    blk16 = (row_w // 16) == (col_w // 16)
    bd_mask = jnp.where(lax.broadcasted_iota(jnp.int32, (WIDE, WIDE), 0) // CHUNK
                        == lax.broadcasted_iota(jnp.int32, (WIDE, WIDE), 1) // CHUNK,
                        1.0, 0.0).astype(BF16)
    kbd_mask = jnp.where(lax.broadcasted_iota(jnp.int32, (WIDE, BRANCH), 0) // CHUNK
                         == lax.broadcasted_iota(jnp.int32, (WIDE, BRANCH), 1) // GDN_K_DIM,
                         1.0, 0.0).astype(BF16)
    r64 = lax.broadcasted_iota(jnp.int32, (CHUNK, CHUNK), 0)
    c64 = lax.broadcasted_iota(jnp.int32, (CHUNK, CHUNK), 1)
    lower_ones = jnp.where(c64 <= r64, 1.0, 0.0).astype(BF16)
    mm = functools.partial(_wide_mm, bd_mask=bd_mask)

    def conv_silu(x_ref, x_halo_ref, seg, gr, first_group):
        cols = slice(seg * BRANCH, (seg + 1) * BRANCH)
        cur = x_ref[gr, :]
        if first_group:
            halo = x_halo_ref[...]
            halo = jnp.where(pl.program_id(1) == 0, jnp.zeros((), halo.dtype), halo)
        else:
            halo = x_ref[gr.start - HALO:gr.start, :]
        conv = cur.astype(F32) * cw_ref[CONV_WIDTH - 1:CONV_WIDTH, cols]
        for s in range(1, CONV_WIDTH):
            shifted = _dot(shift_ref[s - 1], cur)
            head = shifted[:HALO] + _dot(hsel_ref[s - 1], halo)
            shifted = jnp.concatenate([head, shifted[HALO:]], axis=0)
            conv = conv + shifted * cw_ref[CONV_WIDTH - 1 - s:CONV_WIDTH - s, cols]
        return _silu(conv)

    def front(g):
        gr = slice(g * grp, (g + 1) * grp)
        q = _head_l2norm(conv_silu(q_ref, qh_ref, 0, gr, g == 0)) * (GDN_K_DIM ** -0.5)
        yield
        k = _head_l2norm(conv_silu(k_ref, kh_ref, 1, gr, g == 0))
        yield
        v = conv_silu(v_ref, vh_ref, 2, gr, g == 0)
        yield

        ab = ab_ref[gr, :]
        xg = ab + dtb_ref[...]
        softplus = jnp.maximum(xg, 0.0) + jnp.log1p(jnp.exp(-jnp.abs(xg)))
        g_all = -jnp.exp(alog_ref[...]) * softplus
        beta_all = _sigmoid(ab)

        staged = []
        for c in range(chunks):
            rows = slice(c * CHUNK, (c + 1) * CHUNK)
            qc, kc, vc = q[rows], k[rows], v[rows]
            g_c = g_all[rows]
            g_hi, g_lo = _split2(g_c)
            gc2 = _dot(lower_ones, jnp.concatenate([g_hi, g_lo], axis=1))
            gc_all = gc2[:, :LANES] + gc2[:, LANES:]
            gcb = _head_bcast(gc_all, GDN_K_DIM)
            betab = _head_bcast(beta_all[rows, GDN_HEADS:], GDN_K_DIM)
            exp_gc = jnp.exp(gcb)
            gl_row = exp_gc[CHUNK - 1:CHUNK]
            k_beta = kc * betab
            v_beta = vc * betab
            kbe = k_beta * exp_gc
            q_dec = qc * exp_gc
            k_tail = kc * jnp.exp(gcb[CHUNK - 1:CHUNK] - gcb)

            r_w = jnp.where(tri_strict, _head_bcast(g_c, CHUNK), 0.0)
            r_hi, r_lo = _split2(r_w)
            d2 = _dot(lower_ones, jnp.concatenate([r_hi, r_lo], axis=1))
            gamma = jnp.exp(jnp.where(tri_incl, d2[:, :WIDE] + d2[:, WIDE:], NEG_BIG))

            k16 = kc.astype(BF16)
            kbd = jnp.concatenate([k16] * GDN_HEADS, axis=0) * kbd_mask
            sc = _dot_nt(jnp.concatenate([k_beta.astype(BF16), qc.astype(BF16)], axis=0), kbd)
            a_strict = jnp.where(tri_strict, sc[:CHUNK] * gamma, 0.0)
            a_intra = jnp.where(tri_incl, sc[CHUNK:] * gamma, 0.0)
            staged.append((a_strict, a_intra, v_beta, kbe, q_dec, k_tail, gl_row))
            yield
        return staged

    def back(g, staged):
        t_inv = yield from _wide_unit_lower_inverse([st[0] for st in staged], eye_w, blk16, mm)
        pairs = [(c, h) for c in range(chunks) for h in range(GDN_HEADS)]
        hsl = lambda h: slice(h * GDN_K_DIM, (h + 1) * GDN_K_DIM)
        wsl = lambda h: slice(h * CHUNK, (h + 1) * CHUNK)
        uw16 = {}
        for c, h in pairs:
            _, _, v_beta, kbe, _, _, _ = staged[c]
            t_off = t_inv[c] - eye_w
            rhs = jnp.concatenate([v_beta[:, hsl(h)], kbe[:, hsl(h)]], axis=1)
            uw = rhs + _dot(t_off[:, wsl(h)].astype(BF16), rhs.astype(BF16))
            uw16[c, h] = uw.astype(BF16)
        yield
        q16, o0 = {}, {}
        for c, h in pairs:
            a_intra, q_dec = staged[c][1], staged[c][4]
            ai_uw = _dot(a_intra[:, wsl(h)].astype(BF16), uw16[c, h])
            q16[c, h] = (q_dec[:, hsl(h)] - ai_uw[:, GDN_V_DIM:]).astype(BF16)
            o0[c, h] = ai_uw[:, :GDN_V_DIM]
        yield
        ops = {}
        for c, h in pairs:
            k_tail = staged[c][5]
            kt_uw = _dot_tn(k_tail[:, hsl(h)].astype(BF16), uw16[c, h])
            pq16 = jnp.concatenate([kt_uw[:, GDN_V_DIM:].astype(BF16), q16[c, h]], axis=0)
            ops[c, h] = (pq16, kt_uw[:, :GDN_V_DIM], o0[c, h], staged[c][6][:, hsl(h)])
        yield
        return ops

    def scan(g, ops):
        hsl = lambda h: slice(h * GDN_K_DIM, (h + 1) * GDN_K_DIM)
        for c in range(chunks):
            rows = slice((g * chunks + c) * CHUNK, (g * chunks + c + 1) * CHUNK)
            for h in range(GDN_HEADS):
                pq16, n_c, o0_c, gl = ops[c, h]
                state = state_ref[h]
                r = _dot(pq16, state.astype(BF16))
                state_ref[h] = state * gl - r[:GDN_K_DIM] + n_c
                o = r[GDN_K_DIM:] + o0_c
                ms = jnp.mean(o * o, axis=-1, keepdims=True)
                y = o * lax.rsqrt(ms + EPS) * nw_ref[...]
                o_ref[rows, hsl(h)] = (y * z_ref[rows, hsl(h)].astype(F32)).astype(o_ref.dtype)
            yield

    groups = tr // grp
    staged, = _run_interleaved(front(0))
    ops = None
    for g in range(groups):
        nxt = front(g + 1) if g + 1 < groups else None
        prev_scan = scan(g - 1, ops) if g > 0 else None
        ops, staged, _ = _run_interleaved(back(g, staged), nxt, prev_scan)
    _run_interleaved(scan(groups - 1, ops))


def _gdn(qb, kb, vb, ab, zb, conv_w, alog_pad, dtb_pad, norm_w, tr, grp):
    b, s, _ = qb.shape
    assert tr % grp == 0 and grp % CHUNK == 0
    blk = lambda bi, i: (bi, i, 0)
    halo = lambda bi, i: (bi, jnp.maximum(i * (tr // HALO) - 1, 0), 0)
    const = lambda bi, i: (0, 0)
    const3 = lambda bi, i: (0, 0, 0)
    t_idx = jnp.arange(grp)
    h_idx = jnp.arange(HALO)
    shift = jnp.stack([(t_idx[:, None] - sft == t_idx[None, :]) for sft in range(1, CONV_WIDTH)]).astype(BF16)
    hsel = jnp.stack([(h_idx[:, None] - sft + HALO == h_idx[None, :])
                      for sft in range(1, CONV_WIDTH)]).astype(BF16)
    return pl.pallas_call(
        _gdn_kernel,
        grid=(b, s // tr),
        in_specs=[
            pl.BlockSpec((None, tr, BRANCH), blk),
            pl.BlockSpec((None, tr, BRANCH), blk),
            pl.BlockSpec((None, tr, BRANCH), blk),
            pl.BlockSpec((None, HALO, BRANCH), halo),
            pl.BlockSpec((None, HALO, BRANCH), halo),
            pl.BlockSpec((None, HALO, BRANCH), halo),
            pl.BlockSpec((None, tr, LANES), blk),
            pl.BlockSpec((None, tr, BRANCH), blk),
            pl.BlockSpec(conv_w.shape, const),
            pl.BlockSpec((1, LANES), const),
            pl.BlockSpec((1, LANES), const),
            pl.BlockSpec((1, GDN_V_DIM), const),
            pl.BlockSpec(shift.shape, const3),
            pl.BlockSpec(hsel.shape, const3),
        ],
        out_specs=pl.BlockSpec((None, tr, BRANCH), blk),
        out_shape=jax.ShapeDtypeStruct((b, s, BRANCH), BF16),
        scratch_shapes=[pltpu.VMEM((GDN_HEADS, GDN_K_DIM, GDN_V_DIM), F32)],
        compiler_params=pltpu.CompilerParams(
            dimension_semantics=("arbitrary", "arbitrary"), vmem_limit_bytes=VMEM_LIMIT),
        name="gdn",
    )(qb, kb, vb, qb, kb, vb, ab, zb, conv_w, alog_pad, dtb_pad, norm_w, shift, hsel)


def _out_proj_kernel(x_ref, ya_ref, yb_ref, gate_ref, wa_ref, wb_ref, wo_ref, o_ref):
    d = x_ref.shape[-1]
    up_a = _dot(ya_ref[...], wa_ref[...])
    up_b = _dot(yb_ref[...], wb_ref[...])
    merged = (gate_ref[:, :d].astype(F32) * up_a + gate_ref[:, d:].astype(F32) * up_b)
    o_ref[...] = x_ref[...] + _dot(merged.astype(BF16), wo_ref[...])


def _out_proj(x2d, ya, yb, gates, w_up_a, w_up_b, w_out, tm):
    n, d = x2d.shape
    const = lambda i: (0, 0)
    row = lambda i: (i, 0)
    return pl.pallas_call(
        _out_proj_kernel,
        grid=(n // tm,),
        in_specs=[
            pl.BlockSpec((tm, d), row),
            pl.BlockSpec((tm, BRANCH), row),
            pl.BlockSpec((tm, BRANCH), row),
            pl.BlockSpec((tm, 2 * d), row),
            pl.BlockSpec(w_up_a.shape, const),
            pl.BlockSpec(w_up_b.shape, const),
            pl.BlockSpec(w_out.shape, const),
        ],
        out_specs=pl.BlockSpec((tm, d), row),
        out_shape=jax.ShapeDtypeStruct((n, d), x2d.dtype),
        compiler_params=pltpu.CompilerParams(
            dimension_semantics=("arbitrary",), vmem_limit_bytes=VMEM_LIMIT),
        name="out_proj",
    )(x2d, ya, yb, gates, w_up_a, w_up_b, w_out)


def _lambda_init_for(layer):
    return 0.8 - 0.6 * math.exp(-0.3 * layer)


def kernel(x, norm_w, w_in, da_q_gain, da_k_gain, da_lambda, da_out_gain, gdn_conv_w, gdn_a_log,
           gdn_dt_bias, gdn_norm_w, w_up_da, w_up_gdn, w_out):
    b, s, d = x.shape
    depth = norm_w.shape[0]
    in_proj_tile = 1024
    out_proj_tile = 1024
    attn_q_tile = 1024
    attn_k_tile = 256
    gdn_tile = 512
    gdn_group = 256
    groups_per_seg = BRANCH // DA_HEAD_DIM
    q_scale = DA_HEAD_DIM ** -0.5 * math.log2(math.e)

    x2d = x.reshape(b * s, d)
    for l in range(depth):
        w_feat = w_in[l].T
        n_blocked = w_feat.shape[0] // BRANCH * BRANCH
        w_t = _weight_prep(w_feat, n_blocked)
        w_last = jnp.pad(w_feat[n_blocked:], ((0, -(w_feat.shape[0] - n_blocked) % LANES), (0, 0))).astype(BF16)
        q_gain = jnp.tile(da_q_gain[l] * q_scale, groups_per_seg)[None, :]
        k_gain = jnp.tile(da_k_gain[l], groups_per_seg)[None, :]
        qa, ka, vat, za, qb, kb, vb, zb, ab, gates = _in_proj(
            x2d, norm_w[l][None, :], w_t, w_last, q_gain, k_gain, in_proj_tile)

        r3 = lambda t: t.reshape(b, s, t.shape[-1])
        ya = _diff_attn(r3(qa), r3(ka), vat, r3(za), da_lambda[l], da_out_gain[l][None, :],
                        _lambda_init_for(l), attn_q_tile, attn_k_tile)

        alog_pad = jnp.pad(gdn_a_log[l], (0, LANES - GDN_HEADS))[None, :]
        dtb_pad = jnp.pad(gdn_dt_bias[l], (0, LANES - GDN_HEADS))[None, :]
        yb = _gdn(r3(qb), r3(kb), r3(vb), r3(ab), r3(zb), gdn_conv_w[l], alog_pad, dtb_pad,
                  gdn_norm_w[l][None, :], gdn_tile, gdn_group)

        x2d = _out_proj(x2d, ya.reshape(b * s, -1), yb.reshape(b * s, -1), gates,
                        w_up_da[l].astype(BF16), w_up_gdn[l].astype(BF16), w_out[l].astype(BF16),
                        out_proj_tile)
    return x2d.reshape(b, s, d)
```

```python
import functools
import math

import jax
import jax.numpy as jnp
from jax import lax
from jax.experimental import pallas as pl
from jax.experimental.pallas import tpu as pltpu

F32 = jnp.float32
BF16 = jnp.bfloat16

CHUNK = 64
EPS = 1e-6
DA_HEADS = 4
DA_HEAD_DIM = 64
DA_V_DIM = 2 * DA_HEAD_DIM
GDN_HEADS = 4
GDN_K_DIM = 128
GDN_V_DIM = 128
CONV_WIDTH = 4
BRANCH = 512
LANES = 128
NEG_BIG = -1e30
ONES_ROWS = 16
UNROLL_BLOCKS = 2
VMEM_LIMIT = 56 * 1024 * 1024


def _dot(a, b, precision=None):
    return jnp.dot(a, b, preferred_element_type=F32, precision=precision)


def _dot_nt(a, b, precision=None):
    return lax.dot_general(a, b, (((1,), (1,)), ((), ())),
                           preferred_element_type=F32, precision=precision)


def _dot_tn(a, b, precision=None):
    return lax.dot_general(a, b, (((0,), (0,)), ((), ())),
                           preferred_element_type=F32, precision=precision)


def _sigmoid(x):
    return 1.0 / (1.0 + jnp.exp(-x))


def _silu(x):
    return x * _sigmoid(x)


def _weight_prep_kernel(w_ref, wb_ref):
    wb_ref[...] = w_ref[...].astype(BF16)


def _weight_prep(wt, n_rows):
    d = wt.shape[1]
    blk = pl.BlockSpec((BRANCH, d), lambda j: (j, 0))
    return pl.pallas_call(
        _weight_prep_kernel,
        grid=(n_rows // BRANCH,),
        in_specs=[blk],
        out_specs=blk,
        out_shape=jax.ShapeDtypeStruct((n_rows, d), BF16),
        compiler_params=pltpu.CompilerParams(
            dimension_semantics=("arbitrary",), vmem_limit_bytes=VMEM_LIMIT),
        name="weight_prep",
    )(wt)


def _in_proj_kernel(x_ref, nw_ref, wt_ref, wlast_ref, qg_ref, kg_ref,
                    qa_ref, ka_ref, vat_ref, za_ref, qb_ref, kb_ref, vb_ref, zb_ref, ab_ref, gate_ref):
    x = x_ref[...]
    ms = jnp.mean(x * x, axis=-1, keepdims=True)
    h = (x * lax.rsqrt(ms + EPS) * nw_ref[...]).astype(BF16)

    def proj(j):
        return _dot_nt(h, wt_ref[j * BRANCH:(j + 1) * BRANCH, :])

    def qk_norm(t, gain):
        sq = t * t
        low = lax.broadcasted_iota(jnp.int32, (t.shape[0], LANES), 1) < DA_HEAD_DIM
        parts = []
        for j in range(t.shape[1] // LANES):
            blk = sq[:, j * LANES:(j + 1) * LANES]
            s_low = jnp.sum(jnp.where(low, blk, 0.0), axis=-1, keepdims=True)
            s_all = jnp.sum(blk, axis=-1, keepdims=True)
            parts.append(jnp.where(low, s_low, s_all - s_low))
        ssq = jnp.concatenate(parts, axis=1)
        return (t * lax.rsqrt(ssq * (1.0 / DA_HEAD_DIM) + EPS) * gain).astype(BF16)

    qa_ref[...] = qk_norm(proj(0), qg_ref[...])
    ka_ref[...] = qk_norm(proj(1), kg_ref[...])
    vat_ref[...] = _dot_nt(wt_ref[2 * BRANCH:3 * BRANCH, :], h).astype(BF16)
    za_ref[...] = _silu(proj(3)).astype(BF16)
    qb_ref[...] = proj(4).astype(BF16)
    kb_ref[...] = proj(5).astype(BF16)
    vb_ref[...] = proj(6).astype(BF16)
    zb_ref[...] = _silu(proj(7)).astype(BF16)
    tail = jnp.concatenate([_dot_nt(h, wt_ref[8 * BRANCH:, :]), _dot_nt(h, wlast_ref[...])], axis=1)
    ab_ref[...] = tail[:, :LANES]
    n_ab = 2 * GDN_HEADS
    gate_ref[...] = _sigmoid(tail[:, n_ab:n_ab + gate_ref.shape[1]]).astype(BF16)


def _in_proj(x2d, norm_w, w_t, w_last, q_gain, k_gain, tm):
    n, d = x2d.shape
    const = lambda i: (0, 0)
    row = lambda i: (i, 0)
    seg = jax.ShapeDtypeStruct((n, BRANCH), BF16)
    seg_spec = pl.BlockSpec((tm, BRANCH), row)
    out_shape = [seg, seg, jax.ShapeDtypeStruct((BRANCH, n), BF16)] + [seg] * 5 + [
        jax.ShapeDtypeStruct((n, LANES), F32), jax.ShapeDtypeStruct((n, 2 * d), BF16)]
    out_specs = ([seg_spec, seg_spec, pl.BlockSpec((BRANCH, tm), lambda i: (0, i))]
                 + [seg_spec] * 5
                 + [pl.BlockSpec((tm, LANES), row), pl.BlockSpec((tm, 2 * d), row)])
    return pl.pallas_call(
        _in_proj_kernel,
        grid=(n // tm,),
        in_specs=[
            pl.BlockSpec((tm, d), row),
            pl.BlockSpec((1, d), const),
            pl.BlockSpec(w_t.shape, const, pipeline_mode=pl.Buffered(1)),
            pl.BlockSpec(w_last.shape, const, pipeline_mode=pl.Buffered(1)),
            pl.BlockSpec((1, BRANCH), const),
            pl.BlockSpec((1, BRANCH), const),
        ],
        out_specs=out_specs,
        out_shape=out_shape,
        compiler_params=pltpu.CompilerParams(
            dimension_semantics=("arbitrary",), vmem_limit_bytes=VMEM_LIMIT),
        name="in_proj",
    )(x2d, norm_w, w_t, w_last, q_gain, k_gain)


def _diff_attn_kernel(lam_ref, bias_ref, q_ref, k_ref, vt_ref, z_ref, og_ref, o_ref, s_scr, m_scr, acc_scr,
                      *, tq, tk, lambda_init):
    i = pl.program_id(2)
    q = q_ref[...]
    lane = lax.broadcasted_iota(jnp.int32, q.shape, 1)
    zero = jnp.zeros_like(q)
    qq = jnp.concatenate([jnp.where(lane < DA_HEAD_DIM, q, zero),
                          jnp.where(lane >= DA_HEAD_DIM, q, zero)], axis=0)
    qq_t = qq.T
    sub_tiles = s_scr.shape[0]
    ones_rows = jnp.ones((ONES_ROWS, tk), BF16)
    m_scr[...] = jnp.full(m_scr.shape, NEG_BIG, F32)
    acc_scr[...] = jnp.zeros(acc_scr.shape, F32)

    def issue(start, slot, q0=0):
        wq = tq - q0
        qsel = qq_t if q0 == 0 else jnp.concatenate([qq_t[:, q0:tq], qq_t[:, tq + q0:]], axis=1)
        s = _dot(k_ref[pl.ds(start, tk), :], qsel)
        s_scr[slot, :, :2 * wq] = s
        return jnp.max(s, axis=0, keepdims=True)

    def consume(start, slot, tile_max, masked, q0=0):
        wq = tq - q0
        s = s_scr[slot, :, :2 * wq]
        if masked:
            bias = bias_ref[...]
            pieces = []
            for g in range(2):
                pieces.append(s[:, g * wq:g * wq + tk] + bias)
                if wq > tk:
                    pieces.append(s[:, g * wq + tk:(g + 1) * wq])
            s = jnp.concatenate(pieces, axis=1)
            tile_max = jnp.max(s, axis=0, keepdims=True)
        m_cols = [slice(g * tq + q0, (g + 1) * tq) for g in range(2)]
        m = jnp.concatenate([m_scr[:, mc] for mc in m_cols], axis=1)
        m_new = jnp.maximum(m, tile_max)
        alpha = jnp.exp2(m - m_new)
        p = jnp.exp2(s - m_new)
        pb = p.astype(BF16)
        vt = jnp.concatenate([vt_ref[:, pl.ds(start, tk)], ones_rows], axis=0)
        for g in range(2):
            cols = slice(g * wq, (g + 1) * wq)
            m_scr[:, m_cols[g]] = m_new[:, cols]
            acc_scr[g, :, q0:] = alpha[:, cols] * acc_scr[g, :, q0:] + _dot(vt, pb[:, cols])

    def make_body(blocks):
        def body(jj, tile_max):
            for u in range(blocks):
                for t in range(sub_tiles):
                    start = pl.multiple_of((jj * blocks + u) * tq + t * tk, tk)
                    next_max = issue(start + tk, (t + 1) % sub_tiles)
                    consume(start, t, tile_max, masked=False)
                    tile_max = next_max
            return tile_max
        return body

    paired = i // UNROLL_BLOCKS
    tile_max = lax.fori_loop(0, paired, make_body(UNROLL_BLOCKS), issue(0, 0))
    tile_max = lax.fori_loop(paired * UNROLL_BLOCKS, i, make_body(1), tile_max)
    for t in range(sub_tiles):
        start = pl.multiple_of(i * tq + t * tk, tk)
        if t + 1 < sub_tiles:
            issue(start + tk, t + 1, q0=(t + 1) * tk)
        consume(start, t, None, masked=True, q0=t * tk)
    acc1 = acc_scr[0]
    acc2 = acc_scr[1]

    lam_p = lam_ref[...]
    lam = (jnp.exp(jnp.sum(lam_p[0:1] * lam_p[1:2], axis=-1, keepdims=True))
           - jnp.exp(jnp.sum(lam_p[2:3] * lam_p[3:4], axis=-1, keepdims=True))
           + lambda_init)
    ot = (acc1[:DA_V_DIM] / acc1[DA_V_DIM:DA_V_DIM + 1]
          - lam * (acc2[:DA_V_DIM] / acc2[DA_V_DIM:DA_V_DIM + 1]))
    ms = jnp.mean(ot * ot, axis=0, keepdims=True)
    y = (ot * lax.rsqrt(ms + EPS)).T * og_ref[...] * (1.0 - lambda_init)
    o_ref[...] = (y * z_ref[...].astype(F32)).astype(o_ref.dtype)


def _diff_attn(qa, ka, vat, za, da_lambda, out_gain, lambda_init, tq, tk):
    b, s, _ = qa.shape
    sub_tiles = tq // tk
    assert sub_tiles >= 2 and sub_tiles * tk == tq
    blk = lambda bi, h, i: (bi, i, h)
    const = lambda bi, h, i: (0, 0)
    chunk_of = jnp.arange(tk) // CHUNK
    bias = jnp.where(chunk_of[:, None] <= chunk_of[None, :], 0.0, NEG_BIG).astype(F32)
    return pl.pallas_call(
        functools.partial(_diff_attn_kernel, tq=tq, tk=tk, lambda_init=lambda_init),
        scratch_shapes=[pltpu.VMEM((sub_tiles, tk, 2 * tq), F32),
                        pltpu.VMEM((1, 2 * tq), F32),
                        pltpu.VMEM((2, DA_V_DIM + ONES_ROWS, tq), F32)],
        grid=(b, DA_HEADS, s // tq),
        in_specs=[
            pl.BlockSpec(da_lambda.shape, const),
            pl.BlockSpec(bias.shape, const),
            pl.BlockSpec((None, tq, DA_V_DIM), blk),
            pl.BlockSpec((None, s, DA_V_DIM), lambda bi, h, i: (bi, 0, h)),
            pl.BlockSpec((DA_V_DIM, s), lambda bi, h, i: (h, bi)),
            pl.BlockSpec((None, tq, DA_V_DIM), blk),
            pl.BlockSpec((1, DA_V_DIM), const),
        ],
        out_specs=pl.BlockSpec((None, tq, DA_V_DIM), blk),
        out_shape=jax.ShapeDtypeStruct((b, s, DA_HEADS * DA_V_DIM), BF16),
        compiler_params=pltpu.CompilerParams(
            dimension_semantics=("arbitrary", "arbitrary", "arbitrary"),
            vmem_limit_bytes=VMEM_LIMIT),
        name="diff_attn",
    )(da_lambda, bias, qa, ka, vat, za, out_gain)


WIDE = GDN_HEADS * CHUNK
HALO = 16


def _split2(x):
    hi = x.astype(BF16)
    lo = (x - hi.astype(F32)).astype(BF16)
    return hi, lo


def _block_diag(y, mask01):
    return jnp.concatenate([y] * GDN_HEADS, axis=0) * mask01


def _wide_mm(x, y, bd_mask):
    return _dot(x.astype(BF16), _block_diag(y.astype(BF16), bd_mask))


def _wide_unit_lower_inverse(a_list, eye, blk16, mm):
    d = [jnp.where(blk16, a, 0.0) for a in a_list]
    off = [a - di for a, di in zip(a_list, d)]
    p = [-di for di in d]
    t = [eye + pi for pi in p]
    for _ in range(3):
        p = [mm(pi, pi) for pi in p]
        yield
        t = [ti + mm(ti, pi) for ti, pi in zip(t, p)]
        yield
    e = [mm(ti, oi) for ti, oi in zip(t, off)]
    yield
    e2 = [mm(ei, ei) for ei in e]
    yield
    f = [eye - ei + e2i - mm(ei, e2i) for ei, e2i in zip(e, e2)]
    yield
    return [mm(fi, ti) for fi, ti in zip(f, t)]


def _run_interleaved(*gens):
    results = [None] * len(gens)
    alive = [g is not None for g in gens]
    while any(alive):
        for n, gen in enumerate(gens):
            if alive[n]:
                try:
                    next(gen)
                except StopIteration as stop:
                    results[n] = stop.value
                    alive[n] = False
    return results


def _head_l2norm(x):
    parts = []
    for h in range(GDN_HEADS):
        xh = x[:, h * GDN_K_DIM:(h + 1) * GDN_K_DIM]
        parts.append(xh * lax.rsqrt(jnp.sum(xh * xh, axis=-1, keepdims=True) + EPS))
    return jnp.concatenate(parts, axis=1)


def _head_bcast(col4, width):
    rows = col4.shape[0]
    return jnp.concatenate([jnp.broadcast_to(col4[:, h:h + 1], (rows, width))
                            for h in range(GDN_HEADS)], axis=1)


def _gdn_kernel(q_ref, k_ref, v_ref, qh_ref, kh_ref, vh_ref, ab_ref, z_ref, cw_ref, alog_ref, dtb_ref,
                nw_ref, shift_ref, hsel_ref, o_ref, state_ref):
    tr = q_ref.shape[0]

    @pl.when(pl.program_id(1) == 0)
    def _():
        state_ref[...] = jnp.zeros_like(state_ref)

    grp = shift_ref.shape[1]
    chunks = grp // CHUNK

    row_w = lax.broadcasted_iota(jnp.int32, (CHUNK, WIDE), 0)
    col_w = lax.broadcasted_iota(jnp.int32, (CHUNK, WIDE), 1) % CHUNK
    tri_incl = col_w <= row_w
    tri_strict = col_w < row_w
    eye_w = jnp.where(col_w == row_w, 1.0, 0.0).astype(F32)
    blk16 = (row_w // 16) == (col_w // 16)
    bd_mask = jnp.where(lax.broadcasted_iota(jnp.int32, (WIDE, WIDE), 0) // CHUNK
                        == lax.broadcasted_iota(jnp.int32, (WIDE, WIDE), 1) // CHUNK,
                        1.0, 0.0).astype(BF16)
    kbd_mask = jnp.where(lax.broadcasted_iota(jnp.int32, (WIDE, BRANCH), 0) // CHUNK
                         == lax.broadcasted_iota(jnp.int32, (WIDE, BRANCH), 1) // GDN_K_DIM,
                         1.0, 0.0).astype(BF16)
    r64 = lax.broadcasted_iota(jnp.int32, (CHUNK, CHUNK), 0)
    c64 = lax.broadcasted_iota(jnp.int32, (CHUNK, CHUNK), 1)
    lower_ones = jnp.where(c64 <= r64, 1.0, 0.0).astype(BF16)
    mm = functools.partial(_wide_mm, bd_mask=bd_mask)

    def conv_silu(x_ref, x_halo_ref, seg, gr, first_group):
        cols = slice(seg * BRANCH, (seg + 1) * BRANCH)
        cur = x_ref[gr, :]
        if first_group:
            halo = x_halo_ref[...]
            halo = jnp.where(pl.program_id(1) == 0, jnp.zeros((), halo.dtype), halo)
        else:
            halo = x_ref[gr.start - HALO:gr.start, :]
        conv = cur.astype(F32) * cw_ref[CONV_WIDTH - 1:CONV_WIDTH, cols]
        for s in range(1, CONV_WIDTH):
            shifted = _dot(shift_ref[s - 1], cur)
            head = shifted[:HALO] + _dot(hsel_ref[s - 1], halo)
            shifted = jnp.concatenate([head, shifted[HALO:]], axis=0)
            conv = conv + shifted * cw_ref[CONV_WIDTH - 1 - s:CONV_WIDTH - s, cols]
        return _silu(conv)

    def front(g):
        gr = slice(g * grp, (g + 1) * grp)
        q = _head_l2norm(conv_silu(q_ref, qh_ref, 0, gr, g == 0)) * (GDN_K_DIM ** -0.5)
        yield
        k = _head_l2norm(conv_silu(k_ref, kh_ref, 1, gr, g == 0))
        yield
        v = conv_silu(v_ref, vh_ref, 2, gr, g == 0)
        yield

        ab = ab_ref[gr, :]
        xg = ab + dtb_ref[...]
        softplus = jnp.maximum(xg, 0.0) + jnp.log1p(jnp.exp(-jnp.abs(xg)))
        g_all = -jnp.exp(alog_ref[...]) * softplus
        beta_all = _sigmoid(ab)

        staged = []
        for c in range(chunks):
            rows = slice(c * CHUNK, (c + 1) * CHUNK)
            qc, kc, vc = q[rows], k[rows], v[rows]
            g_c = g_all[rows]
            g_hi, g_lo = _split2(g_c)
            gc2 = _dot(lower_ones, jnp.concatenate([g_hi, g_lo], axis=1))
            gc_all = gc2[:, :LANES] + gc2[:, LANES:]
            gcb = _head_bcast(gc_all, GDN_K_DIM)
            betab = _head_bcast(beta_all[rows, GDN_HEADS:], GDN_K_DIM)
            exp_gc = jnp.exp(gcb)
            gl_row = exp_gc[CHUNK - 1:CHUNK]
            k_beta = kc * betab
            v_beta = vc * betab
            kbe = k_beta * exp_gc
            q_dec = qc * exp_gc
            k_tail = kc * jnp.exp(gcb[CHUNK - 1:CHUNK] - gcb)

            r_w = jnp.where(tri_strict, _head_bcast(g_c, CHUNK), 0.0)
            r_hi, r_lo = _split2(r_w)
            d2 = _dot(lower_ones, jnp.concatenate([r_hi, r_lo], axis=1))
            gamma = jnp.exp(jnp.where(tri_incl, d2[:, :WIDE] + d2[:, WIDE:], NEG_BIG))

            k16 = kc.astype(BF16)
            kbd = jnp.concatenate([k16] * GDN_HEADS, axis=0) * kbd_mask
            sc = _dot_nt(jnp.concatenate([k_beta.astype(BF16), qc.astype(BF16)], axis=0), kbd)
            a_strict = jnp.where(tri_strict, sc[:CHUNK] * gamma, 0.0)
            a_intra = jnp.where(tri_incl, sc[CHUNK:] * gamma, 0.0)
            staged.append((a_strict, a_intra, v_beta, kbe, q_dec, k_tail, gl_row))
            yield
        return staged

    def back(g, staged):
        t_inv = yield from _wide_unit_lower_inverse([st[0] for st in staged], eye_w, blk16, mm)
        pairs = [(c, h) for c in range(chunks) for h in range(GDN_HEADS)]
        hsl = lambda h: slice(h * GDN_K_DIM, (h + 1) * GDN_K_DIM)
        wsl = lambda h: slice(h * CHUNK, (h + 1) * CHUNK)
        uw16 = {}
        for c, h in pairs:
            _, _, v_beta, kbe, _, _, _ = staged[c]
            t_off = t_inv[c] - eye_w
            rhs = jnp.concatenate([v_beta[:, hsl(h)], kbe[:, hsl(h)]], axis=1)
            uw = rhs + _dot(t_off[:, wsl(h)].astype(BF16), rhs.astype(BF16))
            uw16[c, h] = uw.astype(BF16)
        yield
        q16, o0 = {}, {}
        for c, h in pairs:
            a_intra, q_dec = staged[c][1], staged[c][4]
            ai_uw = _dot(a_intra[:, wsl(h)].astype(BF16), uw16[c, h])
            q16[c, h] = (q_dec[:, hsl(h)] - ai_uw[:, GDN_V_DIM:]).astype(BF16)
            o0[c, h] = ai_uw[:, :GDN_V_DIM]
        yield
        ops = {}
        for c, h in pairs:
            k_tail = staged[c][5]
            kt_uw = _dot_tn(k_tail[:, hsl(h)].astype(BF16), uw16[c, h])
            pq16 = jnp.concatenate([kt_uw[:, GDN_V_DIM:].astype(BF16), q16[c, h]], axis=0)
            ops[c, h] = (pq16, kt_uw[:, :GDN_V_DIM], o0[c, h], staged[c][6][:, hsl(h)])
        yield
        return ops

    def scan(g, ops):
        hsl = lambda h: slice(h * GDN_K_DIM, (h + 1) * GDN_K_DIM)
        for c in range(chunks):
            rows = slice((g * chunks + c) * CHUNK, (g * chunks + c + 1) * CHUNK)
            for h in range(GDN_HEADS):
                pq16, n_c, o0_c, gl = ops[c, h]
                state = state_ref[h]
                r = _dot(pq16, state.astype(BF16))
                state_ref[h] = state * gl - r[:GDN_K_DIM] + n_c
                o = r[GDN_K_DIM:] + o0_c
                ms = jnp.mean(o * o, axis=-1, keepdims=True)
                y = o * lax.rsqrt(ms + EPS) * nw_ref[...]
                o_ref[rows, hsl(h)] = (y * z_ref[rows, hsl(h)].astype(F32)).astype(o_ref.dtype)
            yield

    groups = tr // grp
    staged, = _run_interleaved(front(0))
    ops = None
    for g in range(groups):
        nxt = front(g + 1) if g + 1 < groups else None
        prev_scan = scan(g - 1, ops) if g > 0 else None
        ops, staged, _ = _run_interleaved(back(g, staged), nxt, prev_scan)
    _run_interleaved(scan(groups - 1, ops))


def _gdn(qb, kb, vb, ab, zb, conv_w, alog_pad, dtb_pad, norm_w, tr, grp):
    b, s, _ = qb.shape
    assert tr % grp == 0 and grp % CHUNK == 0
    blk = lambda bi, i: (bi, i, 0)
    halo = lambda bi, i: (bi, jnp.maximum(i * (tr // HALO) - 1, 0), 0)
    const = lambda bi, i: (0, 0)
    const3 = lambda bi, i: (0, 0, 0)
    t_idx = jnp.arange(grp)
    h_idx = jnp.arange(HALO)
    shift = jnp.stack([(t_idx[:, None] - sft == t_idx[None, :]) for sft in range(1, CONV_WIDTH)]).astype(BF16)
    hsel = jnp.stack([(h_idx[:, None] - sft + HALO == h_idx[None, :])
                      for sft in range(1, CONV_WIDTH)]).astype(BF16)
    return pl.pallas_call(
        _gdn_kernel,
        grid=(b, s // tr),
        in_specs=[
            pl.BlockSpec((None, tr, BRANCH), blk),
            pl.BlockSpec((None, tr, BRANCH), blk),
            pl.BlockSpec((None, tr, BRANCH), blk),
            pl.BlockSpec((None, HALO, BRANCH), halo),
            pl.BlockSpec((None, HALO, BRANCH), halo),
            pl.BlockSpec((None, HALO, BRANCH), halo),
            pl.BlockSpec((None, tr, LANES), blk),
            pl.BlockSpec((None, tr, BRANCH), blk),
            pl.BlockSpec(conv_w.shape, const),
            pl.BlockSpec((1, LANES), const),
            pl.BlockSpec((1, LANES), const),
            pl.BlockSpec((1, GDN_V_DIM), const),
            pl.BlockSpec(shift.shape, const3),
            pl.BlockSpec(hsel.shape, const3),
        ],
        out_specs=pl.BlockSpec((None, tr, BRANCH), blk),
        out_shape=jax.ShapeDtypeStruct((b, s, BRANCH), BF16),
        scratch_shapes=[pltpu.VMEM((GDN_HEADS, GDN_K_DIM, GDN_V_DIM), F32)],
        compiler_params=pltpu.CompilerParams(
            dimension_semantics=("arbitrary", "arbitrary"), vmem_limit_bytes=VMEM_LIMIT),
        name="gdn",
    )(qb, kb, vb, qb, kb, vb, ab, zb, conv_w, alog_pad, dtb_pad, norm_w, shift, hsel)


RING = 3


def _out_proj_kernel(x_hbm, ya_hbm, yb_hbm, gate_hbm, wa_ref, wb_ref, wo_ref, o_ref,
                     x_buf, ya_buf, yb_buf, gate_buf, sems, *, tm, steps):
    s = pl.program_id(0)
    d = x_buf.shape[-1]
    streams = ((x_hbm, x_buf), (ya_hbm, ya_buf), (yb_hbm, yb_buf), (gate_hbm, gate_buf))

    def tile_copies(step, slot):
        rows = pl.ds(pl.multiple_of(step * tm, tm), tm)
        return [pltpu.make_async_copy(src.at[rows], buf.at[slot], sems.at[k, slot])
                for k, (src, buf) in enumerate(streams)]

    @pl.when(s == 0)
    def _():
        for st in range(min(RING - 1, steps)):
            for cp in tile_copies(st, st):
                cp.start()

    ahead = s + (RING - 1)

    @pl.when(ahead < steps)
    def _():
        for cp in tile_copies(ahead, ahead % RING):
            cp.start()

    slot = s % RING
    for cp in tile_copies(s, slot):
        cp.wait()
    up_a = _dot(ya_buf[slot], wa_ref[...])
    up_b = _dot(yb_buf[slot], wb_ref[...])
    gate = gate_buf[slot]
    merged = gate[:, :d].astype(F32) * up_a + gate[:, d:].astype(F32) * up_b
    o_ref[...] = x_buf[slot] + _dot(merged.astype(BF16), wo_ref[...])


def _out_proj(x2d, ya, yb, gates, w_up_a, w_up_b, w_out, tm):
    n, d = x2d.shape
    steps = n // tm
    const = lambda i: (0, 0)
    hbm = pl.BlockSpec(memory_space=pl.ANY)
    return pl.pallas_call(
        functools.partial(_out_proj_kernel, tm=tm, steps=steps),
        grid=(steps,),
        in_specs=[
            hbm, hbm, hbm, hbm,
            pl.BlockSpec(w_up_a.shape, const),
            pl.BlockSpec(w_up_b.shape, const),
            pl.BlockSpec(w_out.shape, const),
        ],
        out_specs=pl.BlockSpec((tm, d), lambda i: (i, 0)),
        out_shape=jax.ShapeDtypeStruct((n, d), x2d.dtype),
        scratch_shapes=[pltpu.VMEM((RING, tm, d), x2d.dtype),
                        pltpu.VMEM((RING, tm, BRANCH), ya.dtype),
                        pltpu.VMEM((RING, tm, BRANCH), yb.dtype),
                        pltpu.VMEM((RING, tm, 2 * d), gates.dtype),
                        pltpu.SemaphoreType.DMA((4, RING))],
        compiler_params=pltpu.CompilerParams(
            dimension_semantics=("arbitrary",), vmem_limit_bytes=VMEM_LIMIT),
        name="out_proj",
    )(x2d, ya, yb, gates, w_up_a, w_up_b, w_out)


def _lambda_init_for(layer):
    return 0.8 - 0.6 * math.exp(-0.3 * layer)


def kernel(x, norm_w, w_in, da_q_gain, da_k_gain, da_lambda, da_out_gain, gdn_conv_w, gdn_a_log,
           gdn_dt_bias, gdn_norm_w, w_up_da, w_up_gdn, w_out):
    b, s, d = x.shape
    depth = norm_w.shape[0]
    in_proj_tile = 1024
    out_proj_tile = 1024
    attn_q_tile = 1024
    attn_k_tile = 256
    gdn_tile = 512
    gdn_group = 256
    groups_per_seg = BRANCH // DA_HEAD_DIM
    q_scale = DA_HEAD_DIM ** -0.5 * math.log2(math.e)

    x2d = x.reshape(b * s, d)
    for l in range(depth):
        w_feat = w_in[l].T
        n_blocked = w_feat.shape[0] // BRANCH * BRANCH
        w_t = _weight_prep(w_feat, n_blocked)
        w_last = jnp.pad(w_feat[n_blocked:], ((0, -(w_feat.shape[0] - n_blocked) % LANES), (0, 0))).astype(BF16)
        q_gain = jnp.tile(da_q_gain[l] * q_scale, groups_per_seg)[None, :]
        k_gain = jnp.tile(da_k_gain[l], groups_per_seg)[None, :]
        qa, ka, vat, za, qb, kb, vb, zb, ab, gates = _in_proj(
            x2d, norm_w[l][None, :], w_t, w_last, q_gain, k_gain, in_proj_tile)

        r3 = lambda t: t.reshape(b, s, t.shape[-1])
        ya = _diff_attn(r3(qa), r3(ka), vat, r3(za), da_lambda[l], da_out_gain[l][None, :],
                        _lambda_init_for(l), attn_q_tile, attn_k_tile)

        alog_pad = jnp.pad(gdn_a_log[l], (0, LANES - GDN_HEADS))[None, :]
        dtb_pad = jnp.pad(gdn_dt_bias[l], (0, LANES - GDN_HEADS))[None, :]
        yb = _gdn(r3(qb), r3(kb), r3(vb), r3(ab), r3(zb), gdn_conv_w[l], alog_pad, dtb_pad,
                  gdn_norm_w[l][None, :], gdn_tile, gdn_group)

        x2d = _out_proj(x2d, ya.reshape(b * s, -1), yb.reshape(b * s, -1), gates,
                        w_up_da[l].astype(BF16), w_up_gdn[l].astype(BF16), w_out[l].astype(BF16),
                        out_proj_tile)
    return x2d.reshape(b, s, d)
```

```python
import functools
import math

import jax
import jax.numpy as jnp
from jax import lax
from jax.experimental import pallas as pl
from jax.experimental.pallas import tpu as pltpu

F32 = jnp.float32
BF16 = jnp.bfloat16

CHUNK = 64
EPS = 1e-6
DA_HEADS = 4
DA_HEAD_DIM = 64
DA_V_DIM = 2 * DA_HEAD_DIM
GDN_HEADS = 4
GDN_K_DIM = 128
GDN_V_DIM = 128
CONV_WIDTH = 4
BRANCH = 512
LANES = 128
NEG_BIG = -1e30
ONES_ROWS = 16
UNROLL_BLOCKS = 2
VMEM_LIMIT = 56 * 1024 * 1024


def _dot(a, b, precision=None):
    return jnp.dot(a, b, preferred_element_type=F32, precision=precision)


def _dot_nt(a, b, precision=None):
    return lax.dot_general(a, b, (((1,), (1,)), ((), ())),
                           preferred_element_type=F32, precision=precision)


def _dot_tn(a, b, precision=None):
    return lax.dot_general(a, b, (((0,), (0,)), ((), ())),
                           preferred_element_type=F32, precision=precision)


def _sigmoid(x):
    return 1.0 / (1.0 + jnp.exp(-x))


def _silu(x):
    return x * _sigmoid(x)


def _weight_prep_kernel(w_ref, wb_ref):
    wb_ref[...] = w_ref[...].astype(BF16)


def _weight_prep(wt, n_rows):
    d = wt.shape[1]
    blk = pl.BlockSpec((BRANCH, d), lambda j: (j, 0))
    return pl.pallas_call(
        _weight_prep_kernel,
        grid=(n_rows // BRANCH,),
        in_specs=[blk],
        out_specs=blk,
        out_shape=jax.ShapeDtypeStruct((n_rows, d), BF16),
        compiler_params=pltpu.CompilerParams(
            dimension_semantics=("arbitrary",), vmem_limit_bytes=VMEM_LIMIT),
        name="weight_prep",
    )(wt)


def _in_proj_kernel(x_ref, nw_ref, wt_ref, wlast_ref, qg_ref, kg_ref,
                    qa_ref, ka_ref, vat_ref, za_ref, qb_ref, kb_ref, vb_ref, zb_ref, ab_ref, gate_ref):
    x = x_ref[...]
    ms = jnp.mean(x * x, axis=-1, keepdims=True)
    h = (x * lax.rsqrt(ms + EPS) * nw_ref[...]).astype(BF16)

    def proj(j):
        return _dot_nt(h, wt_ref[j * BRANCH:(j + 1) * BRANCH, :].astype(BF16))

    def qk_norm(t, gain):
        sq = t * t
        low = lax.broadcasted_iota(jnp.int32, (t.shape[0], LANES), 1) < DA_HEAD_DIM
        parts = []
        for j in range(t.shape[1] // LANES):
            blk = sq[:, j * LANES:(j + 1) * LANES]
            s_low = jnp.sum(jnp.where(low, blk, 0.0), axis=-1, keepdims=True)
            s_all = jnp.sum(blk, axis=-1, keepdims=True)
            parts.append(jnp.where(low, s_low, s_all - s_low))
        ssq = jnp.concatenate(parts, axis=1)
        return (t * lax.rsqrt(ssq * (1.0 / DA_HEAD_DIM) + EPS) * gain).astype(BF16)

    qa_ref[...] = qk_norm(proj(0), qg_ref[...])
    ka_ref[...] = qk_norm(proj(1), kg_ref[...])
    vat_ref[...] = _dot_nt(wt_ref[2 * BRANCH:3 * BRANCH, :].astype(BF16), h).astype(BF16)
    za_ref[...] = _silu(proj(3)).astype(BF16)
    qb_ref[...] = proj(4).astype(BF16)
    kb_ref[...] = proj(5).astype(BF16)
    vb_ref[...] = proj(6).astype(BF16)
    zb_ref[...] = _silu(proj(7)).astype(BF16)
    tail = jnp.concatenate([_dot_nt(h, wt_ref[8 * BRANCH:, :].astype(BF16)), _dot_nt(h, wlast_ref[...])], axis=1)
    ab_ref[...] = tail[:, :LANES]
    n_ab = 2 * GDN_HEADS
    gate_ref[...] = _sigmoid(tail[:, n_ab:n_ab + gate_ref.shape[1]]).astype(BF16)


def _in_proj(x2d, norm_w, w_t, w_last, q_gain, k_gain, tm):
    n, d = x2d.shape
    const = lambda i: (0, 0)
    row = lambda i: (i, 0)
    seg = jax.ShapeDtypeStruct((n, BRANCH), BF16)
    seg_spec = pl.BlockSpec((tm, BRANCH), row)
    out_shape = [seg, seg, jax.ShapeDtypeStruct((BRANCH, n), BF16)] + [seg] * 5 + [
        jax.ShapeDtypeStruct((n, LANES), F32), jax.ShapeDtypeStruct((n, 2 * d), BF16)]
    out_specs = ([seg_spec, seg_spec, pl.BlockSpec((BRANCH, tm), lambda i: (0, i))]
                 + [seg_spec] * 5
                 + [pl.BlockSpec((tm, LANES), row), pl.BlockSpec((tm, 2 * d), row)])
    return pl.pallas_call(
        _in_proj_kernel,
        grid=(n // tm,),
        in_specs=[
            pl.BlockSpec((tm, d), row),
            pl.BlockSpec((1, d), const),
            pl.BlockSpec((w_t.shape[0] // BRANCH * BRANCH, d), const, pipeline_mode=pl.Buffered(1)),
            pl.BlockSpec(w_last.shape, const, pipeline_mode=pl.Buffered(1)),
            pl.BlockSpec((1, BRANCH), const),
            pl.BlockSpec((1, BRANCH), const),
        ],
        out_specs=out_specs,
        out_shape=out_shape,
        compiler_params=pltpu.CompilerParams(
            dimension_semantics=("arbitrary",), vmem_limit_bytes=VMEM_LIMIT),
        name="in_proj",
    )(x2d, norm_w, w_t, w_last, q_gain, k_gain)


def _diff_attn_kernel(lam_ref, bias_ref, q_ref, k_ref, vt_ref, z_ref, og_ref, o_ref, s_scr, m_scr, acc_scr,
                      *, tq, tk, lambda_init):
    i = pl.program_id(2)
    q = q_ref[...]
    lane = lax.broadcasted_iota(jnp.int32, q.shape, 1)
    zero = jnp.zeros_like(q)
    qq = jnp.concatenate([jnp.where(lane < DA_HEAD_DIM, q, zero),
                          jnp.where(lane >= DA_HEAD_DIM, q, zero)], axis=0)
    qq_t = qq.T
    sub_tiles = s_scr.shape[0]
    ones_rows = jnp.ones((ONES_ROWS, tk), BF16)
    m_scr[...] = jnp.full(m_scr.shape, NEG_BIG, F32)
    acc_scr[...] = jnp.zeros(acc_scr.shape, F32)

    def issue(start, slot, q0=0):
        wq = tq - q0
        qsel = qq_t if q0 == 0 else jnp.concatenate([qq_t[:, q0:tq], qq_t[:, tq + q0:]], axis=1)
        s = _dot(k_ref[pl.ds(start, tk), :], qsel)
        s_scr[slot, :, :2 * wq] = s
        return jnp.max(s, axis=0, keepdims=True)

    def consume(start, slot, tile_max, masked, q0=0):
        wq = tq - q0
        s = s_scr[slot, :, :2 * wq]
        if masked:
            bias = bias_ref[...]
            pieces = []
            for g in range(2):
                pieces.append(s[:, g * wq:g * wq + tk] + bias)
                if wq > tk:
                    pieces.append(s[:, g * wq + tk:(g + 1) * wq])
            s = jnp.concatenate(pieces, axis=1)
            tile_max = jnp.max(s, axis=0, keepdims=True)
        m_cols = [slice(g * tq + q0, (g + 1) * tq) for g in range(2)]
        m = jnp.concatenate([m_scr[:, mc] for mc in m_cols], axis=1)
        m_new = jnp.maximum(m, tile_max)
        alpha = jnp.exp2(m - m_new)
        p = jnp.exp2(s - m_new)
        pb = p.astype(BF16)
        vt = jnp.concatenate([vt_ref[:, pl.ds(start, tk)], ones_rows], axis=0)
        for g in range(2):
            cols = slice(g * wq, (g + 1) * wq)
            m_scr[:, m_cols[g]] = m_new[:, cols]
            acc_scr[g, :, q0:] = alpha[:, cols] * acc_scr[g, :, q0:] + _dot(vt, pb[:, cols])

    def make_body(blocks):
        def body(jj, tile_max):
            for u in range(blocks):
                for t in range(sub_tiles):
                    start = pl.multiple_of((jj * blocks + u) * tq + t * tk, tk)
                    next_max = issue(start + tk, (t + 1) % sub_tiles)
                    consume(start, t, tile_max, masked=False)
                    tile_max = next_max
            return tile_max
        return body

    paired = i // UNROLL_BLOCKS
    tile_max = lax.fori_loop(0, paired, make_body(UNROLL_BLOCKS), issue(0, 0))
    tile_max = lax.fori_loop(paired * UNROLL_BLOCKS, i, make_body(1), tile_max)
    for t in range(sub_tiles):
        start = pl.multiple_of(i * tq + t * tk, tk)
        if t + 1 < sub_tiles:
            issue(start + tk, t + 1, q0=(t + 1) * tk)
        consume(start, t, None, masked=True, q0=t * tk)
    acc1 = acc_scr[0]
    acc2 = acc_scr[1]

    lam_p = lam_ref[...]
    lam = (jnp.exp(jnp.sum(lam_p[0:1] * lam_p[1:2], axis=-1, keepdims=True))
           - jnp.exp(jnp.sum(lam_p[2:3] * lam_p[3:4], axis=-1, keepdims=True))
           + lambda_init)
    ot = (acc1[:DA_V_DIM] / acc1[DA_V_DIM:DA_V_DIM + 1]
          - lam * (acc2[:DA_V_DIM] / acc2[DA_V_DIM:DA_V_DIM + 1]))
    ms = jnp.mean(ot * ot, axis=0, keepdims=True)
    y = (ot * lax.rsqrt(ms + EPS)).T * og_ref[...] * (1.0 - lambda_init)
    o_ref[...] = (y * z_ref[...].astype(F32)).astype(o_ref.dtype)


def _diff_attn(qa, ka, vat, za, da_lambda, out_gain, lambda_init, tq, tk):
    b, s, _ = qa.shape
    sub_tiles = tq // tk
    assert sub_tiles >= 2 and sub_tiles * tk == tq
    blk = lambda bi, h, i: (bi, i, h)
    const = lambda bi, h, i: (0, 0)
    chunk_of = jnp.arange(tk) // CHUNK
    bias = jnp.where(chunk_of[:, None] <= chunk_of[None, :], 0.0, NEG_BIG).astype(F32)
    return pl.pallas_call(
        functools.partial(_diff_attn_kernel, tq=tq, tk=tk, lambda_init=lambda_init),
        scratch_shapes=[pltpu.VMEM((sub_tiles, tk, 2 * tq), F32),
                        pltpu.VMEM((1, 2 * tq), F32),
                        pltpu.VMEM((2, DA_V_DIM + ONES_ROWS, tq), F32)],
        grid=(b, DA_HEADS, s // tq),
        in_specs=[
            pl.BlockSpec(da_lambda.shape, const),
            pl.BlockSpec(bias.shape, const),
            pl.BlockSpec((None, tq, DA_V_DIM), blk),
            pl.BlockSpec((None, s, DA_V_DIM), lambda bi, h, i: (bi, 0, h)),
            pl.BlockSpec((DA_V_DIM, s), lambda bi, h, i: (h, bi)),
            pl.BlockSpec((None, tq, DA_V_DIM), blk),
            pl.BlockSpec((1, DA_V_DIM), const),
        ],
        out_specs=pl.BlockSpec((None, tq, DA_V_DIM), blk),
        out_shape=jax.ShapeDtypeStruct((b, s, DA_HEADS * DA_V_DIM), BF16),
        compiler_params=pltpu.CompilerParams(
            dimension_semantics=("arbitrary", "arbitrary", "arbitrary"),
            vmem_limit_bytes=VMEM_LIMIT),
        name="diff_attn",
    )(da_lambda, bias, qa, ka, vat, za, out_gain)


WIDE = GDN_HEADS * CHUNK
HALO = 16


def _split2(x):
    hi = x.astype(BF16)
    lo = (x - hi.astype(F32)).astype(BF16)
    return hi, lo


def _block_diag(y, mask01):
    return jnp.concatenate([y] * GDN_HEADS, axis=0) * mask01


def _wide_mm(x, y, bd_mask):
    return _dot(x.astype(BF16), _block_diag(y.astype(BF16), bd_mask))


def _wide_unit_lower_inverse(a_list, eye, blk16, mm):
    d = [jnp.where(blk16, a, 0.0) for a in a_list]
    off = [a - di for a, di in zip(a_list, d)]
    p = [-di for di in d]
    t = [eye + pi for pi in p]
    for _ in range(3):
        p = [mm(pi, pi) for pi in p]
        yield
        t = [ti + mm(ti, pi) for ti, pi in zip(t, p)]
        yield
    e = [mm(ti, oi) for ti, oi in zip(t, off)]
    yield
    e2 = [mm(ei, ei) for ei in e]
    yield
    f = [eye - ei + e2i - mm(ei, e2i) for ei, e2i in zip(e, e2)]
    yield
    return [mm(fi, ti) for fi, ti in zip(f, t)]


def _run_interleaved(*gens):
    results = [None] * len(gens)
    alive = [g is not None for g in gens]
    while any(alive):
        for n, gen in enumerate(gens):
            if alive[n]:
                try:
                    next(gen)
                except StopIteration as stop:
                    results[n] = stop.value
                    alive[n] = False
    return results


def _head_l2norm(x):
    parts = []
    for h in range(GDN_HEADS):
        xh = x[:, h * GDN_K_DIM:(h + 1) * GDN_K_DIM]
        parts.append(xh * lax.rsqrt(jnp.sum(xh * xh, axis=-1, keepdims=True) + EPS))
    return jnp.concatenate(parts, axis=1)


def _head_bcast(col4, width):
    rows = col4.shape[0]
    return jnp.concatenate([jnp.broadcast_to(col4[:, h:h + 1], (rows, width))
                            for h in range(GDN_HEADS)], axis=1)


def _gdn_kernel(q_ref, k_ref, v_ref, qh_ref, kh_ref, vh_ref, ab_ref, z_ref, cw_ref, alog_ref, dtb_ref,
                nw_ref, shift_ref, hsel_ref, o_ref, state_ref):
    tr = q_ref.shape[0]

    @pl.when(pl.program_id(1) == 0)
    def _():
        state_ref[...] = jnp.zeros_like(state_ref)

    grp = shift_ref.shape[1]
    chunks = grp // CHUNK

    row_w = lax.broadcasted_iota(jnp.int32, (CHUNK, WIDE), 0)
    col_w = lax.broadcasted_iota(jnp.int32, (CHUNK, WIDE), 1) % CHUNK
    tri_incl = col_w <= row_w
    tri_strict = col_w < row_w
    eye_w = jnp.where(col_w == row_w, 1.0, 0.0).astype(F32)
    blk16 = (row_w // 16) == (col_w // 16)
    bd_mask = jnp.where(lax.broadcasted_iota(jnp.int32, (WIDE, WIDE), 0) // CHUNK
                        == lax.broadcasted_iota(jnp.int32, (WIDE, WIDE), 1) // CHUNK,
                        1.0, 0.0).astype(BF16)
    kbd_mask = jnp.where(lax.broadcasted_iota(jnp.int32, (WIDE, BRANCH), 0) // CHUNK
                         == lax.broadcasted_iota(jnp.int32, (WIDE, BRANCH), 1) // GDN_K_DIM,
                         1.0, 0.0).astype(BF16)
    r64 = lax.broadcasted_iota(jnp.int32, (CHUNK, CHUNK), 0)
    c64 = lax.broadcasted_iota(jnp.int32, (CHUNK, CHUNK), 1)
    lower_ones = jnp.where(c64 <= r64, 1.0, 0.0).astype(BF16)
    mm = functools.partial(_wide_mm, bd_mask=bd_mask)

    def conv_silu(x_ref, x_halo_ref, seg, gr, first_group):
        cols = slice(seg * BRANCH, (seg + 1) * BRANCH)
        cur = x_ref[gr, :]
        if first_group:
            halo = x_halo_ref[...]
            halo = jnp.where(pl.program_id(1) == 0, jnp.zeros((), halo.dtype), halo)
        else:
            halo = x_ref[gr.start - HALO:gr.start, :]
        conv = cur.astype(F32) * cw_ref[CONV_WIDTH - 1:CONV_WIDTH, cols]
        for s in range(1, CONV_WIDTH):
            shifted = _dot(shift_ref[s - 1], cur)
            head = shifted[:HALO] + _dot(hsel_ref[s - 1], halo)
            shifted = jnp.concatenate([head, shifted[HALO:]], axis=0)
            conv = conv + shifted * cw_ref[CONV_WIDTH - 1 - s:CONV_WIDTH - s, cols]
        return _silu(conv)

    def front(g):
        gr = slice(g * grp, (g + 1) * grp)
        q = _head_l2norm(conv_silu(q_ref, qh_ref, 0, gr, g == 0)) * (GDN_K_DIM ** -0.5)
        yield
        k = _head_l2norm(conv_silu(k_ref, kh_ref, 1, gr, g == 0))
        yield
        v = conv_silu(v_ref, vh_ref, 2, gr, g == 0)
        yield

        ab = ab_ref[gr, :]
        xg = ab + dtb_ref[...]
        softplus = jnp.maximum(xg, 0.0) + jnp.log1p(jnp.exp(-jnp.abs(xg)))
        g_all = -jnp.exp(alog_ref[...]) * softplus
        beta_all = _sigmoid(ab)

        staged = []
        for c in range(chunks):
            rows = slice(c * CHUNK, (c + 1) * CHUNK)
            qc, kc, vc = q[rows], k[rows], v[rows]
            g_c = g_all[rows]
            g_hi, g_lo = _split2(g_c)
            gc2 = _dot(lower_ones, jnp.concatenate([g_hi, g_lo], axis=1))
            gc_all = gc2[:, :LANES] + gc2[:, LANES:]
            gcb = _head_bcast(gc_all, GDN_K_DIM)
            betab = _head_bcast(beta_all[rows, GDN_HEADS:], GDN_K_DIM)
            exp_gc = jnp.exp(gcb)
            gl_row = exp_gc[CHUNK - 1:CHUNK]
            k_beta = kc * betab
            v_beta = vc * betab
            kbe = k_beta * exp_gc
            q_dec = qc * exp_gc
            k_tail = kc * jnp.exp(gcb[CHUNK - 1:CHUNK] - gcb)

            r_w = jnp.where(tri_strict, _head_bcast(g_c, CHUNK), 0.0)
            r_hi, r_lo = _split2(r_w)
            d2 = _dot(lower_ones, jnp.concatenate([r_hi, r_lo], axis=1))
            gamma = jnp.exp(jnp.where(tri_incl, d2[:, :WIDE] + d2[:, WIDE:], NEG_BIG))

            k16 = kc.astype(BF16)
            kbd = jnp.concatenate([k16] * GDN_HEADS, axis=0) * kbd_mask
            sc = _dot_nt(jnp.concatenate([k_beta.astype(BF16), qc.astype(BF16)], axis=0), kbd)
            a_strict = jnp.where(tri_strict, sc[:CHUNK] * gamma, 0.0)
            a_intra = jnp.where(tri_incl, sc[CHUNK:] * gamma, 0.0)
            staged.append((a_strict, a_intra, v_beta, kbe, q_dec, k_tail, gl_row))
            yield
        return staged

    def back(g, staged):
        t_inv = yield from _wide_unit_lower_inverse([st[0] for st in staged], eye_w, blk16, mm)
        pairs = [(c, h) for c in range(chunks) for h in range(GDN_HEADS)]
        hsl = lambda h: slice(h * GDN_K_DIM, (h + 1) * GDN_K_DIM)
        wsl = lambda h: slice(h * CHUNK, (h + 1) * CHUNK)
        uw16 = {}
        for c, h in pairs:
            _, _, v_beta, kbe, _, _, _ = staged[c]
            t_off = t_inv[c] - eye_w
            rhs = jnp.concatenate([v_beta[:, hsl(h)], kbe[:, hsl(h)]], axis=1)
            uw = rhs + _dot(t_off[:, wsl(h)].astype(BF16), rhs.astype(BF16))
            uw16[c, h] = uw.astype(BF16)
        yield
        q16, o0 = {}, {}
        for c, h in pairs:
            a_intra, q_dec = staged[c][1], staged[c][4]
            ai_uw = _dot(a_intra[:, wsl(h)].astype(BF16), uw16[c, h])
            q16[c, h] = (q_dec[:, hsl(h)] - ai_uw[:, GDN_V_DIM:]).astype(BF16)
            o0[c, h] = ai_uw[:, :GDN_V_DIM]
        yield
        ops = {}
        for c, h in pairs:
            k_tail = staged[c][5]
            kt_uw = _dot_tn(k_tail[:, hsl(h)].astype(BF16), uw16[c, h])
            pq16 = jnp.concatenate([kt_uw[:, GDN_V_DIM:].astype(BF16), q16[c, h]], axis=0)
            ops[c, h] = (pq16, kt_uw[:, :GDN_V_DIM], o0[c, h], staged[c][6][:, hsl(h)])
        yield
        return ops

    def scan(g, ops):
        hsl = lambda h: slice(h * GDN_K_DIM, (h + 1) * GDN_K_DIM)
        for c in range(chunks):
            rows = slice((g * chunks + c) * CHUNK, (g * chunks + c + 1) * CHUNK)
            for h in range(GDN_HEADS):
                pq16, n_c, o0_c, gl = ops[c, h]
                state = state_ref[h]
                r = _dot(pq16, state.astype(BF16))
                state_ref[h] = state * gl - r[:GDN_K_DIM] + n_c
                o = r[GDN_K_DIM:] + o0_c
                ms = jnp.mean(o * o, axis=-1, keepdims=True)
                y = o * lax.rsqrt(ms + EPS) * nw_ref[...]
                o_ref[rows, hsl(h)] = (y * z_ref[rows, hsl(h)].astype(F32)).astype(o_ref.dtype)
            yield

    groups = tr // grp
    staged, = _run_interleaved(front(0))
    ops = None
    for g in range(groups):
        nxt = front(g + 1) if g + 1 < groups else None
        prev_scan = scan(g - 1, ops) if g > 0 else None
        ops, staged, _ = _run_interleaved(back(g, staged), nxt, prev_scan)
    _run_interleaved(scan(groups - 1, ops))


def _gdn(qb, kb, vb, ab, zb, conv_w, alog_pad, dtb_pad, norm_w, tr, grp):
    b, s, _ = qb.shape
    assert tr % grp == 0 and grp % CHUNK == 0
    blk = lambda bi, i: (bi, i, 0)
    halo = lambda bi, i: (bi, jnp.maximum(i * (tr // HALO) - 1, 0), 0)
    const = lambda bi, i: (0, 0)
    const3 = lambda bi, i: (0, 0, 0)
    t_idx = jnp.arange(grp)
    h_idx = jnp.arange(HALO)
    shift = jnp.stack([(t_idx[:, None] - sft == t_idx[None, :]) for sft in range(1, CONV_WIDTH)]).astype(BF16)
    hsel = jnp.stack([(h_idx[:, None] - sft + HALO == h_idx[None, :])
                      for sft in range(1, CONV_WIDTH)]).astype(BF16)
    return pl.pallas_call(
        _gdn_kernel,
        grid=(b, s // tr),
        in_specs=[
            pl.BlockSpec((None, tr, BRANCH), blk),
            pl.BlockSpec((None, tr, BRANCH), blk),
            pl.BlockSpec((None, tr, BRANCH), blk),
            pl.BlockSpec((None, HALO, BRANCH), halo),
            pl.BlockSpec((None, HALO, BRANCH), halo),
            pl.BlockSpec((None, HALO, BRANCH), halo),
            pl.BlockSpec((None, tr, LANES), blk),
            pl.BlockSpec((None, tr, BRANCH), blk),
            pl.BlockSpec(conv_w.shape, const),
            pl.BlockSpec((1, LANES), const),
            pl.BlockSpec((1, LANES), const),
            pl.BlockSpec((1, GDN_V_DIM), const),
            pl.BlockSpec(shift.shape, const3),
            pl.BlockSpec(hsel.shape, const3),
        ],
        out_specs=pl.BlockSpec((None, tr, BRANCH), blk),
        out_shape=jax.ShapeDtypeStruct((b, s, BRANCH), BF16),
        scratch_shapes=[pltpu.VMEM((GDN_HEADS, GDN_K_DIM, GDN_V_DIM), F32)],
        compiler_params=pltpu.CompilerParams(
            dimension_semantics=("arbitrary", "arbitrary"), vmem_limit_bytes=VMEM_LIMIT),
        name="gdn",
    )(qb, kb, vb, qb, kb, vb, ab, zb, conv_w, alog_pad, dtb_pad, norm_w, shift, hsel)


def _out_proj_kernel(x_ref, ya_ref, yb_ref, gate_ref, wa_ref, wb_ref, wo_ref, o_ref):
    d = x_ref.shape[-1]
    up_a = _dot(ya_ref[...], wa_ref[...])
    up_b = _dot(yb_ref[...], wb_ref[...])
    merged = (gate_ref[:, :d].astype(F32) * up_a + gate_ref[:, d:].astype(F32) * up_b)
    o_ref[...] = x_ref[...] + _dot(merged.astype(BF16), wo_ref[...])


def _out_proj(x2d, ya, yb, gates, w_up_a, w_up_b, w_out, tm):
    n, d = x2d.shape
    const = lambda i: (0, 0)
    row = lambda i: (i, 0)
    return pl.pallas_call(
        _out_proj_kernel,
        grid=(n // tm,),
        in_specs=[
            pl.BlockSpec((tm, d), row),
            pl.BlockSpec((tm, BRANCH), row),
            pl.BlockSpec((tm, BRANCH), row),
            pl.BlockSpec((tm, 2 * d), row),
            pl.BlockSpec(w_up_a.shape, const),
            pl.BlockSpec(w_up_b.shape, const),
            pl.BlockSpec(w_out.shape, const),
        ],
        out_specs=pl.BlockSpec((tm, d), row),
        out_shape=jax.ShapeDtypeStruct((n, d), x2d.dtype),
        compiler_params=pltpu.CompilerParams(
            dimension_semantics=("arbitrary",), vmem_limit_bytes=VMEM_LIMIT),
        name="out_proj",
    )(x2d, ya, yb, gates, w_up_a, w_up_b, w_out)


def _lambda_init_for(layer):
    return 0.8 - 0.6 * math.exp(-0.3 * layer)


def kernel(x, norm_w, w_in, da_q_gain, da_k_gain, da_lambda, da_out_gain, gdn_conv_w, gdn_a_log,
           gdn_dt_bias, gdn_norm_w, w_up_da, w_up_gdn, w_out):
    b, s, d = x.shape
    depth = norm_w.shape[0]
    in_proj_tile = 512
    out_proj_tile = 1024
    attn_q_tile = 1024
    attn_k_tile = 256
    gdn_tile = 512
    gdn_group = 256
    groups_per_seg = BRANCH // DA_HEAD_DIM
    q_scale = DA_HEAD_DIM ** -0.5 * math.log2(math.e)

    x2d = x.reshape(b * s, d)
    for l in range(depth):
        w_feat = w_in[l].T
        n_blocked = w_feat.shape[0] // BRANCH * BRANCH
        w_t = w_feat
        w_last = jnp.pad(w_feat[n_blocked:], ((0, -(w_feat.shape[0] - n_blocked) % LANES), (0, 0))).astype(BF16)
        q_gain = jnp.tile(da_q_gain[l] * q_scale, groups_per_seg)[None, :]
        k_gain = jnp.tile(da_k_gain[l], groups_per_seg)[None, :]
        qa, ka, vat, za, qb, kb, vb, zb, ab, gates = _in_proj(
            x2d, norm_w[l][None, :], w_t, w_last, q_gain, k_gain, in_proj_tile)

        r3 = lambda t: t.reshape(b, s, t.shape[-1])
        ya = _diff_attn(r3(qa), r3(ka), vat, r3(za), da_lambda[l], da_out_gain[l][None, :],
                        _lambda_init_for(l), attn_q_tile, attn_k_tile)

        alog_pad = jnp.pad(gdn_a_log[l], (0, LANES - GDN_HEADS))[None, :]
        dtb_pad = jnp.pad(gdn_dt_bias[l], (0, LANES - GDN_HEADS))[None, :]
        yb = _gdn(r3(qb), r3(kb), r3(vb), r3(ab), r3(zb), gdn_conv_w[l], alog_pad, dtb_pad,
                  gdn_norm_w[l][None, :], gdn_tile, gdn_group)

        x2d = _out_proj(x2d, ya.reshape(b * s, -1), yb.reshape(b * s, -1), gates,
                        w_up_da[l].astype(BF16), w_up_gdn[l].astype(BF16), w_out[l].astype(BF16),
                        out_proj_tile)
    return x2d.reshape(b, s, d)
```
